```python
import math, functools
import jax, jax.numpy as jnp
from jax import lax
import numpy as np

D_MODEL = 1024
BATCH = 8
SEQ = 2048
DEPTH = 1
DEC_BATCH = 32
DEC_SEQ = 8
PAST_LEN = 8192
PAGE_SIZE = 128

N_HEADS = 8
N_KV_HEADS = 2
HEAD_DIM = 64
GROUP = N_HEADS // N_KV_HEADS
ATTN_WIDTH = N_HEADS * HEAD_DIM
KV_WIDTH = N_KV_HEADS * HEAD_DIM
ATTN_SCALE = HEAD_DIM ** -0.5
N_IDX_HEADS = 4
IDX_DIM = 64
IDX_SCALE = (IDX_DIM * N_IDX_HEADS) ** -0.5
TOPK_MAX = 256
Q_BLOCK = 128
N_BUCKETS = 32
MAX_DISTANCE = 128
D_CONV = 512
CONV_WIDTH = 3
N_KEYS = 128
N_EXPERTS = N_KEYS * N_KEYS
PEER_HEADS = 8
PEER_KEY_DIM = 128
PEER_HALF = PEER_KEY_DIM // 2
PEER_TOPK_HALF = 16
PEER_TOPK = 16
PEER_BLOCK = 128
DN_ALPHA = (2 * DEPTH) ** 0.25
DN_BETA = (8 * DEPTH) ** -0.25
LN_EPS = 1e-5
MIX_WIDTHS = (ATTN_WIDTH, KV_WIDTH, KV_WIDTH, N_IDX_HEADS * IDX_DIM, N_IDX_HEADS, IDX_DIM,
              D_CONV, D_CONV, D_CONV, D_MODEL, D_MODEL)
D_MIX_IN = sum(MIX_WIDTHS)

kernel_name = "dsa_shortconv_peer_deepnorm_adaln_step"


def layer_norm(x, g, b):
    xf = x.astype(jnp.float32)
    mu = jnp.mean(xf, axis=-1, keepdims=True)
    var = jnp.mean(jnp.square(xf - mu), axis=-1, keepdims=True)
    return ((xf - mu) * lax.rsqrt(var + LN_EPS)).astype(x.dtype) * g + b


def split_columns(proj):
    parts, start = [], 0
    for w in MIX_WIDTHS:
        parts.append(proj[..., start:start + w])
        start += w
    return parts


def t5_bucket(dist):
    n = jnp.maximum(dist, 0)
    max_exact = N_BUCKETS // 2
    nf = jnp.maximum(n, 1).astype(jnp.float32)
    large = max_exact + (jnp.log(nf / max_exact) / math.log(MAX_DISTANCE / max_exact)
                         * (N_BUCKETS - max_exact)).astype(jnp.int32)
    large = jnp.minimum(large, N_BUCKETS - 1)
    return jnp.where(n < max_exact, n, large)


def indexer_scores(qi, wi, ki, q_pos, k_pos):
    dots = jnp.einsum('bqhd,bld->bqhl', qi, ki).astype(jnp.float32)
    s = jnp.einsum('bqhl,bqh->bql', jax.nn.relu(dots), wi.astype(jnp.float32)) * IDX_SCALE
    mask = k_pos[None, :] <= q_pos[:, None]
    return jnp.where(mask[None], s, -jnp.inf)


def sparse_attend(q, kg, vg, q_pos, sel, rel_bias):
    logits = jnp.einsum('bqgrd,bqkgd->bqgrk', q, kg).astype(jnp.float32) * ATTN_SCALE
    dist = q_pos[None, :, None] - sel
    bias = rel_bias[t5_bucket(dist)]
    bias = bias.reshape(*sel.shape, N_KV_HEADS, GROUP).transpose(0, 1, 3, 4, 2)
    logits = jnp.where((dist >= 0)[:, :, None, None, :], logits + bias.astype(jnp.float32), -jnp.inf)
    p = jax.nn.softmax(logits, axis=-1).astype(vg.dtype)
    o = jnp.einsum('bqgrk,bqkgd->bqgrd', p, vg)
    return o.reshape(o.shape[0], o.shape[1], ATTN_WIDTH)


def attn_prompt(q, k, v, qi, wi, ki, rel_bias):
    B, S = q.shape[0], q.shape[1]
    n_sel = min(TOPK_MAX, S // 4)
    nblk = S // Q_BLOCK
    k_pos = jnp.arange(S)

    def block(args):
        qb, qib, wib, q_pos = args
        scores = indexer_scores(qib, wib, ki, q_pos, k_pos)
        _, sel = lax.top_k(scores, n_sel)
        kg = jax.vmap(lambda kk, ii: kk[ii])(k, sel)
        vg = jax.vmap(lambda vv, ii: vv[ii])(v, sel)
        return sparse_attend(qb, kg, vg, q_pos, sel, rel_bias)

    def to_blocks(a):
        return jnp.swapaxes(a.reshape(B, nblk, Q_BLOCK, *a.shape[2:]), 0, 1)

    pos = jnp.arange(S).reshape(nblk, Q_BLOCK)
    out = lax.map(block, (to_blocks(q), to_blocks(qi), to_blocks(wi), pos))
    return jnp.swapaxes(out, 0, 1).reshape(B, S, ATTN_WIDTH)


def gather_paged(pool, page_table, new_rows, sel):
    B, T, K = sel.shape
    past = page_table.shape[1] * PAGE_SIZE
    is_past = sel < past
    p = jnp.minimum(sel, past - 1)
    phys = jnp.take_along_axis(page_table, (p // PAGE_SIZE).reshape(B, T * K), axis=1).reshape(B, T, K)
    from_past = pool[phys, p % PAGE_SIZE]
    j = jnp.clip(sel - past, 0, new_rows.shape[1] - 1)
    from_new = jax.vmap(lambda n, i: n[i])(new_rows, j)
    return jnp.where(is_past[..., None, None], from_past, from_new)


def attn_sample(q, k_new, v_new, qi, wi, ki_new, cache_k, cache_v, cache_ki, page_table, rel_bias):
    B, T = q.shape[0], q.shape[1]
    past = page_table.shape[1] * PAGE_SIZE
    L = past + T
    n_sel = min(TOPK_MAX, L // 4)
    ki_past = cache_ki[page_table].reshape(B, past, IDX_DIM)
    ki_all = jnp.concatenate([ki_past, ki_new], axis=1)
    q_pos = past + jnp.arange(T)
    scores = indexer_scores(qi, wi, ki_all, q_pos, jnp.arange(L))
    _, sel = lax.top_k(scores, n_sel)
    kg = gather_paged(cache_k, page_table, k_new, sel)
    vg = gather_paged(cache_v, page_table, v_new, sel)
    return sparse_attend(q, kg, vg, q_pos, sel, rel_bias)


def short_conv(bg, cg, xin, prev, conv_w, conv_b):
    u = cg * xin
    up = jnp.concatenate([prev, u], axis=1)
    T = u.shape[1]
    y = conv_b + sum(up[:, j:j + T] * conv_w[j] for j in range(CONV_WIDTH))
    return bg * y, up[:, up.shape[1] - (CONV_WIDTH - 1):]


def peer_ffn(h, wq, k1, k2, u_tab, v_tab):
    lead = h.shape[:-1]
    hf = h.reshape(-1, D_MODEL)
    n = hf.shape[0]
    hf = jnp.pad(hf, ((0, (-n) % PEER_BLOCK), (0, 0)))

    def block(hb):
        q = (hb @ wq).reshape(-1, PEER_HEADS, 2, PEER_HALF)
        s1 = jnp.einsum('nhd,kd->nhk', q[:, :, 0], k1).astype(jnp.float32)
        s2 = jnp.einsum('nhd,kd->nhk', q[:, :, 1], k2).astype(jnp.float32)
        v1, i1 = lax.top_k(s1, PEER_TOPK_HALF)
        v2, i2 = lax.top_k(s2, PEER_TOPK_HALF)
        cand = (v1[..., :, None] + v2[..., None, :]).reshape(*v1.shape[:-1], -1)
        cidx = (i1[..., :, None] * N_KEYS + i2[..., None, :]).reshape(*i1.shape[:-1], -1)
        sv, si = lax.top_k(cand, PEER_TOPK)
        eidx = jnp.take_along_axis(cidx, si, axis=-1)
        g = jax.nn.softmax(sv, axis=-1)
        a = jnp.einsum('nd,nhkd->nhk', hb, u_tab[eidx])
        act = (jax.nn.gelu(a.astype(jnp.float32)) * g).astype(hb.dtype)
        return jnp.einsum('nhk,nhkd->nd', act, v_tab[eidx])

    out = lax.map(block, hf.reshape(-1, PEER_BLOCK, D_MODEL))
    return out.reshape(-1, D_MODEL)[:n].reshape(*lead, D_MODEL)


def decoder_layer(x, c, prev_conv, attn_fn, w_ada, b_ada, w_in, conv_w, conv_b, w_o_attn, w_o_conv,
                  w_out, ln1_g, ln1_b, ln2_g, ln2_b, peer_wq, peer_k1, peer_k2, peer_u, peer_v):
    B, T = x.shape[0], x.shape[1]
    mod = (c @ w_ada + b_ada)[:, None, :]
    sh1, sc1, g1, sh2, sc2, g2 = jnp.split(mod, 6, axis=-1)
    h = x * (1 + sc1) + sh1
    q, k, v, qi, wi, ki, bg, cg, xin, ga, gb = split_columns(h @ w_in)
    q = q.reshape(B, T, N_KV_HEADS, GROUP, HEAD_DIM)
    k = k.reshape(B, T, N_KV_HEADS, HEAD_DIM)
    v = v.reshape(B, T, N_KV_HEADS, HEAD_DIM)
    qi = qi.reshape(B, T, N_IDX_HEADS, IDX_DIM)
    o_attn = attn_fn(q, k, v, qi, wi, ki)
    o_conv, conv_state = short_conv(bg, cg, xin, prev_conv, conv_w, conv_b)
    merged = jax.nn.sigmoid(ga) * (o_attn @ w_o_attn) + jax.nn.sigmoid(gb) * (o_conv @ w_o_conv)
    x = layer_norm(DN_ALPHA * x + g1 * (merged @ w_out), ln1_g, ln1_b)
    h2 = x * (1 + sc2) + sh2
    x = layer_norm(DN_ALPHA * x + g2 * peer_ffn(h2, peer_wq, peer_k1, peer_k2, peer_u, peer_v), ln2_g, ln2_b)
    return x, k, v, ki, conv_state


def setup_inputs(seed: int = 0) -> dict:
    key = jax.random.key(seed)
    ks = jax.random.split(key, 32)
    f32 = jnp.float32

    def nrm(k, shape, s=1.0):
        return jax.random.normal(k, shape, f32) * s

    n_pages = PAST_LEN // PAGE_SIZE
    n_used = DEC_BATCH * n_pages
    n_pool = n_used + max(1, n_used // 4)
    page_table = jax.random.permutation(ks[0], n_pool)[:n_used].reshape(DEC_BATCH, n_pages).astype(jnp.int32)
    return {
        "x_prompt": nrm(ks[1], (BATCH, SEQ, D_MODEL)),
        "x_sample": nrm(ks[2], (DEC_BATCH, DEC_SEQ, D_MODEL)),
        "c_prompt": nrm(ks[3], (BATCH, D_MODEL)),
        "c_sample": nrm(ks[4], (DEC_BATCH, D_MODEL)),
        "cache_k": nrm(ks[5], (DEPTH, n_pool, PAGE_SIZE, N_KV_HEADS, HEAD_DIM)),
        "cache_v": nrm(ks[6], (DEPTH, n_pool, PAGE_SIZE, N_KV_HEADS, HEAD_DIM)),
        "cache_kidx": nrm(ks[7], (DEPTH, n_pool, PAGE_SIZE, IDX_DIM)),
        "state_conv": nrm(ks[8], (DEPTH, DEC_BATCH, CONV_WIDTH - 1, D_CONV)),
        "page_table": page_table,
        "rel_bias": nrm(ks[9], (N_BUCKETS, N_HEADS), 0.5),
        "w_ada": nrm(ks[10], (DEPTH, D_MODEL, 6 * D_MODEL), 0.5 * D_MODEL ** -0.5),
        "b_ada": nrm(ks[11], (DEPTH, 6 * D_MODEL), 0.01),
        "w_in": nrm(ks[12], (DEPTH, D_MODEL, D_MIX_IN), D_MODEL ** -0.5),
        "conv_w": nrm(ks[13], (DEPTH, CONV_WIDTH, D_CONV), CONV_WIDTH ** -0.5),
        "conv_b": nrm(ks[14], (DEPTH, D_CONV), 0.01),
        "w_o_attn": nrm(ks[15], (DEPTH, ATTN_WIDTH, D_MODEL), DN_BETA * ATTN_WIDTH ** -0.5),
        "w_o_conv": nrm(ks[16], (DEPTH, D_CONV, D_MODEL), DN_BETA * D_CONV ** -0.5),
        "w_out": nrm(ks[17], (DEPTH, D_MODEL, D_MODEL), DN_BETA * D_MODEL ** -0.5),
        "ln1_g": 1.0 + nrm(ks[18], (DEPTH, D_MODEL), 0.02),
        "ln1_b": nrm(ks[19], (DEPTH, D_MODEL), 0.02),
        "ln2_g": 1.0 + nrm(ks[20], (DEPTH, D_MODEL), 0.02),
        "ln2_b": nrm(ks[21], (DEPTH, D_MODEL), 0.02),
        "peer_wq": nrm(ks[22], (DEPTH, D_MODEL, PEER_HEADS * PEER_KEY_DIM), D_MODEL ** -0.5),
        "peer_k1": nrm(ks[23], (DEPTH, N_KEYS, PEER_HALF), PEER_HALF ** -0.5),
        "peer_k2": nrm(ks[24], (DEPTH, N_KEYS, PEER_HALF), PEER_HALF ** -0.5),
        "peer_u": nrm(ks[25], (DEPTH, N_EXPERTS, D_MODEL), D_MODEL ** -0.5),
        "peer_v": nrm(ks[26], (DEPTH, N_EXPERTS, D_MODEL), DN_BETA * (PEER_HEADS * PEER_TOPK) ** -0.5),
    }


def reference(x_prompt, x_sample, c_prompt, c_sample, cache_k, cache_v, cache_kidx, state_conv, page_table,
              rel_bias, w_ada, b_ada, w_in, conv_w, conv_b, w_o_attn, w_o_conv, w_out, ln1_g, ln1_b,
              ln2_g, ln2_b, peer_wq, peer_k1, peer_k2, peer_u, peer_v):
    xp, xs = x_prompt, x_sample
    kp_l, vp_l, kip_l, cp_l, ks_l, vs_l, kis_l, cs_l = [], [], [], [], [], [], [], []
    for l in range(DEPTH):
        lw = (w_ada[l], b_ada[l], w_in[l], conv_w[l], conv_b[l], w_o_attn[l], w_o_conv[l], w_out[l],
              ln1_g[l], ln1_b[l], ln2_g[l], ln2_b[l], peer_wq[l], peer_k1[l], peer_k2[l], peer_u[l], peer_v[l])
        prev0 = jnp.zeros((xp.shape[0], CONV_WIDTH - 1, D_CONV), xp.dtype)
        fn_p = functools.partial(attn_prompt, rel_bias=rel_bias)
        xp, kp, vp, kip, cp = decoder_layer(xp, c_prompt, prev0, fn_p, *lw)
        fn_s = functools.partial(attn_sample, cache_k=cache_k[l], cache_v=cache_v[l], cache_ki=cache_kidx[l],
                                 page_table=page_table, rel_bias=rel_bias)
        xs, ks_, vs_, kis, cs = decoder_layer(xs, c_sample, state_conv[l], fn_s, *lw)
        kp_l.append(kp); vp_l.append(vp); kip_l.append(kip); cp_l.append(cp)
        ks_l.append(ks_); vs_l.append(vs_); kis_l.append(kis); cs_l.append(cs)
    return (xp, xs, jnp.stack(kp_l), jnp.stack(vp_l), jnp.stack(kip_l), jnp.stack(cp_l),
            jnp.stack(ks_l), jnp.stack(vs_l), jnp.stack(kis_l), jnp.stack(cs_l))
```

```python
import functools
import math

import jax
import jax.numpy as jnp
from jax import lax
from jax.experimental import pallas as pl
from jax.experimental.pallas import tpu as pltpu

F32 = jnp.float32
BF16 = jnp.bfloat16
I32 = jnp.int32

N_HEADS = 8
N_KV_HEADS = 2
HEAD_DIM = 64
GROUP = N_HEADS // N_KV_HEADS
ATTN_WIDTH = N_HEADS * HEAD_DIM
KV_WIDTH = N_KV_HEADS * HEAD_DIM
ATTN_SCALE = HEAD_DIM ** -0.5
N_IDX_HEADS = 4
IDX_DIM = 64
IDX_SCALE = (IDX_DIM * N_IDX_HEADS) ** -0.5
TOPK_MAX = 256
N_BUCKETS = 32
MAX_DISTANCE = 128
D_CONV = 512
CONV_WIDTH = 3
N_KEYS = 128
PEER_HEADS = 8
PEER_HALF = 64
PEER_TOPK = 16
LN_EPS = 1e-5
PAGE_SIZE = 128

LANES = 128
VMEM_LIMIT = 56 * 1024 * 1024

INT_MIN = -2 ** 31
NEG_BIG = -1e30

_KIWI = LANES
_OFF_Q = 0
_OFF_K = _OFF_Q + ATTN_WIDTH
_OFF_V = _OFF_K + KV_WIDTH
_OFF_QI = _OFF_V + KV_WIDTH
_OFF_KIWI = _OFF_QI + N_IDX_HEADS * IDX_DIM
_OFF_BG = _OFF_KIWI + _KIWI
_OFF_CG = _OFF_BG + D_CONV
_OFF_XIN = _OFF_CG + D_CONV
_OFF_GA = _OFF_XIN + D_CONV


def _cparams(*sem):
    return pltpu.CompilerParams(dimension_semantics=sem, vmem_limit_bytes=VMEM_LIMIT)


def _dot(a, b):
    return jnp.dot(a, b, preferred_element_type=F32)


def _dot_nt(a, b):
    return lax.dot_general(a, b, (((1,), (1,)), ((), ())), preferred_element_type=F32)


def _sortable(s):
    b = pltpu.bitcast(s + 0.0, I32)
    return b ^ ((b >> 31) & 0x7FFFFFFF)


def _radix_select(count_ge, k, shape):
    t0 = jnp.where(count_ge(jnp.zeros(shape, I32)) >= k, 0, INT_MIN).astype(I32)

    def body(i, t):
        cand = t | (jnp.int32(1) << (30 - i))
        return jnp.where(count_ge(cand) >= k, cand, t)

    return lax.fori_loop(0, 31, body, t0)


def _t5_bias(dist, relb_ref, h):
    n = jnp.maximum(dist, 0)
    max_exact = N_BUCKETS // 2
    nf = jnp.maximum(n, 1).astype(F32)
    large = max_exact + (jnp.log(nf / max_exact) / math.log(MAX_DISTANCE / max_exact)
                         * (N_BUCKETS - max_exact)).astype(I32)
    large = jnp.minimum(large, N_BUCKETS - 1)
    bucket = jnp.where(n < max_exact, n, large)
    out = jnp.zeros(dist.shape, F32)
    for b in range(N_BUCKETS):
        out = jnp.where(bucket == b, relb_ref[b, h], out)
    return out


def _layer_norm(x, g, b):
    mu = jnp.mean(x, axis=-1, keepdims=True)
    var = jnp.mean(jnp.square(x - mu), axis=-1, keepdims=True)
    return (x - mu) * lax.rsqrt(var + LN_EPS) * g + b


def _ada_kernel(c_ref, w_ref, b_ref, o_ref):
    o_ref[...] = _dot(c_ref[...].astype(BF16), w_ref[...].astype(BF16)) + b_ref[...]


def _ada(c, w, b):
    m, d = c.shape
    n = w.shape[1]
    tn = n // 4
    return pl.pallas_call(
        _ada_kernel,
        grid=(n // tn,),
        in_specs=[pl.BlockSpec((m, d), lambda j: (0, 0)),
                  pl.BlockSpec((d, tn), lambda j: (0, j)),
                  pl.BlockSpec((1, tn), lambda j: (0, j))],
        out_specs=pl.BlockSpec((m, tn), lambda j: (0, j)),
        out_shape=jax.ShapeDtypeStruct((m, n), F32),
        compiler_params=_cparams("arbitrary"),
        name="ada_mod",
    )(c, w, b.reshape(1, n))


def _conv_out(bg, u, u1, u2, cw_ref, cb_ref):
    y = cb_ref[...] + u2 * cw_ref[0:1, :] + u1 * cw_ref[1:2, :] + u * cw_ref[2:3, :]
    return bg * y


def _inproj_prompt_kernel(x_ref, sc_ref, sh_ref, w_ref, cw_ref, cb_ref,
                          qT_ref, k_ref, v_ref, kb_ref, vTb_ref, qiT_ref, ki_ref, kib_ref, wiT_ref,
                          oconv_ref, ga_ref, gb_ref, ulast_ref, uprev_scr):
    s = pl.program_id(1)
    tm = x_ref.shape[1]
    h = (x_ref[0] * (1.0 + sc_ref[0]) + sh_ref[0]).astype(BF16)

    def proj(a, width):
        return _dot(h, w_ref[:, a:a + width])

    qT_ref[0] = proj(_OFF_Q, ATTN_WIDTH).T.astype(BF16)
    k = proj(_OFF_K, KV_WIDTH)
    k_ref[0] = k
    kb_ref[0] = k.astype(BF16)
    v = proj(_OFF_V, KV_WIDTH)
    v_ref[0] = v
    for i in range(tm // LANES):
        vTb_ref[0, i] = v[i * LANES:(i + 1) * LANES].T.astype(BF16)
    qiT_ref[0] = proj(_OFF_QI, N_IDX_HEADS * IDX_DIM).T.astype(BF16)
    kiwi = proj(_OFF_KIWI, _KIWI)
    ki_ref[0] = kiwi[:, :IDX_DIM]
    kib_ref[0] = kiwi[:, :IDX_DIM].astype(BF16)
    wiT_ref[0] = kiwi.T[IDX_DIM:IDX_DIM + 8]

    @pl.when(s == 0)
    def _():
        uprev_scr[...] = jnp.zeros(uprev_scr.shape, F32)

    bg = proj(_OFF_BG, D_CONV)
    u = proj(_OFF_CG, D_CONV) * proj(_OFF_XIN, D_CONV)
    row = lax.broadcasted_iota(I32, u.shape, 0)
    up = uprev_scr[...]
    u1 = jnp.where(row < 1, pltpu.roll(up, 1, 0), pltpu.roll(u, 1, 0))
    u2 = jnp.where(row < 2, pltpu.roll(up, 2, 0), pltpu.roll(u, 2, 0))
    oconv_ref[0] = _conv_out(bg, u, u1, u2, cw_ref, cb_ref).astype(BF16)
    uprev_scr[...] = u
    ulast_ref[0] = u[tm - 8:, :]

    ga_ref[0] = proj(_OFF_GA, w_ref.shape[0])
    gb_ref[0] = proj(_OFF_GA + w_ref.shape[0], w_ref.shape[0])


def _inproj_prompt(x, sc, sh, w_all, conv_w, conv_b):
    b, s, d = x.shape
    tm = min(256, s)
    nkb = tm // LANES
    row = lambda width, dt: jax.ShapeDtypeStruct((b, s, width), dt)
    out_shape = (
        jax.ShapeDtypeStruct((b, ATTN_WIDTH, s), BF16),
        row(KV_WIDTH, F32), row(KV_WIDTH, F32), row(KV_WIDTH, BF16),
        jax.ShapeDtypeStruct((b, s // LANES, KV_WIDTH, LANES), BF16),
        jax.ShapeDtypeStruct((b, N_IDX_HEADS * IDX_DIM, s), BF16),
        row(IDX_DIM, F32), row(IDX_DIM, BF16),
        jax.ShapeDtypeStruct((b, 8, s), F32),
        row(D_CONV, BF16),
        row(d, F32), row(d, F32),
        jax.ShapeDtypeStruct((b, 8, D_CONV), F32),
    )
    tile = lambda width: pl.BlockSpec((1, tm, width), lambda i, j: (i, j, 0))
    tileT = lambda rows: pl.BlockSpec((1, rows, tm), lambda i, j: (i, 0, j))
    out_specs = (
        tileT(ATTN_WIDTH), tile(KV_WIDTH), tile(KV_WIDTH), tile(KV_WIDTH),
        pl.BlockSpec((1, nkb, KV_WIDTH, LANES), lambda i, j: (i, j, 0, 0)),
        tileT(N_IDX_HEADS * IDX_DIM), tile(IDX_DIM), tile(IDX_DIM), tileT(8),
        tile(D_CONV), tile(d), tile(d),
        pl.BlockSpec((1, 8, D_CONV), lambda i, j: (i, 0, 0)),
    )
    vec = pl.BlockSpec((1, 1, d), lambda i, j: (i, 0, 0))
    return pl.pallas_call(
        _inproj_prompt_kernel,
        grid=(b, s // tm),
        in_specs=[tile(d), vec, vec,
                  pl.BlockSpec(w_all.shape, lambda i, j: (0, 0)),
                  pl.BlockSpec(conv_w.shape, lambda i, j: (0, 0)),
                  pl.BlockSpec((1, D_CONV), lambda i, j: (0, 0))],
        out_specs=out_specs,
        out_shape=out_shape,
        scratch_shapes=[pltpu.VMEM((tm, D_CONV), F32)],
        compiler_params=_cparams("arbitrary", "arbitrary"),
        name="inproj_prompt",
    )(x, sc, sh, w_all, conv_w, conv_b.reshape(1, D_CONV))


def _inproj_sample_kernel(t_seq, x_ref, sc_ref, sh_ref, w_ref, cw_ref, cb_ref, s0_ref, s1_ref,
                          q_ref, k_ref, v_ref, qi_ref, kiwi_ref, oconv_ref, ga_ref, gb_ref, u_ref):
    h = (x_ref[...] * (1.0 + sc_ref[...]) + sh_ref[...]).astype(BF16)

    def proj(a, width):
        return _dot(h, w_ref[:, a:a + width])

    q_ref[...] = proj(_OFF_Q, ATTN_WIDTH)
    k_ref[...] = proj(_OFF_K, KV_WIDTH)
    v_ref[...] = proj(_OFF_V, KV_WIDTH)
    qi_ref[...] = proj(_OFF_QI, N_IDX_HEADS * IDX_DIM)
    kiwi_ref[...] = proj(_OFF_KIWI, _KIWI)
    bg = proj(_OFF_BG, D_CONV)
    u = proj(_OFF_CG, D_CONV) * proj(_OFF_XIN, D_CONV)
    t = lax.broadcasted_iota(I32, u.shape, 0) % t_seq
    u1 = jnp.where(t == 0, s1_ref[...], pltpu.roll(u, 1, 0))
    u2 = jnp.where(t == 0, s0_ref[...], jnp.where(t == 1, s1_ref[...], pltpu.roll(u, 2, 0)))
    oconv_ref[...] = _conv_out(bg, u, u1, u2, cw_ref, cb_ref).astype(BF16)
    u_ref[...] = u
    ga_ref[...] = proj(_OFF_GA, w_ref.shape[0])
    gb_ref[...] = proj(_OFF_GA + w_ref.shape[0], w_ref.shape[0])


def _inproj_sample(x, sc_rows, sh_rows, w_all, conv_w, conv_b, st0_rows, st1_rows, t_seq):
    n, d = x.shape
    full = lambda a: pl.BlockSpec(a.shape, lambda i: (0,) * a.ndim)
    o = lambda width, dt: jax.ShapeDtypeStruct((n, width), dt)
    out_shape = (o(ATTN_WIDTH, F32), o(KV_WIDTH, F32), o(KV_WIDTH, F32), o(N_IDX_HEADS * IDX_DIM, F32),
                 o(_KIWI, F32), o(D_CONV, BF16), o(d, F32), o(d, F32), o(D_CONV, F32))
    args = (x, sc_rows, sh_rows, w_all, conv_w, conv_b.reshape(1, D_CONV), st0_rows, st1_rows)
    return pl.pallas_call(
        functools.partial(_inproj_sample_kernel, t_seq),
        grid=(1,),
        in_specs=[full(a) for a in args],
        out_specs=tuple(pl.BlockSpec(s.shape, lambda i: (0, 0)) for s in out_shape),
        out_shape=out_shape,
        compiler_params=_cparams("arbitrary"),
        name="inproj_sample",
    )(*args)


def _attn_prompt_kernel(n_sel, relb_ref, qT_ref, qiT_ref, wiT_ref, kb_ref, vTb_ref, kib_ref,
                        o_ref, bias_scr, key_scr, sel_scr):
    qb = pl.program_id(1)
    nkb = qb + 1
    blk = LANES

    @pl.when((pl.program_id(0) == 0) & (qb == 0))
    def _():
        j = lax.broadcasted_iota(I32, (blk, blk), 0)
        i = lax.broadcasted_iota(I32, (blk, blk), 1)
        for delta in range(3):
            dist = i - j + blk * delta
            for h in range(N_HEADS):
                g, r = divmod(h, GROUP)
                bias_scr[delta, g, :, r * blk:(r + 1) * blk] = _t5_bias(dist, relb_ref, h)

    row = lax.broadcasted_iota(I32, (blk, blk), 0)
    lane = lax.broadcasted_iota(I32, (blk, blk), 1)

    def valid_mask(kb):
        return (kb * blk + row) <= (qb * blk + lane)

    def key_rows(kb):
        return pl.ds(pl.multiple_of(kb * blk, blk), blk)

    qiT = jnp.concatenate([qiT_ref[0, h * IDX_DIM:(h + 1) * IDX_DIM, :] for h in range(N_IDX_HEADS)], axis=1)
    wi = [wiT_ref[0, h:h + 1, :] for h in range(N_IDX_HEADS)]

    def score_blk(kb, c):
        d = _dot(kib_ref[0, key_rows(kb), :], qiT)
        s = jnp.maximum(d[:, 0:blk], 0.0) * wi[0]
        for h in range(1, N_IDX_HEADS):
            s = s + jnp.maximum(d[:, h * blk:(h + 1) * blk], 0.0) * wi[h]
        key = _sortable(s * IDX_SCALE)
        key_scr[key_rows(kb), :] = jnp.where(valid_mask(kb), key, INT_MIN)
        return c

    lax.fori_loop(0, nkb, score_blk, 0)

    def count(pred):
        def body(kb, acc):
            return acc + jnp.where(pred(key_scr[key_rows(kb), :]), 1, 0)
        acc = lax.fori_loop(0, nkb, body, jnp.zeros((blk, blk), I32))
        return jnp.sum(acc, axis=0, keepdims=True)

    thr = _radix_select(lambda cand: count(lambda kv: kv >= cand), n_sel, (1, blk))

    need = (n_sel - count(lambda kv: kv > thr)).astype(F32)
    ltri = (lax.broadcasted_iota(I32, (blk, blk), 1) <= lax.broadcasted_iota(I32, (blk, blk), 0)).astype(BF16)

    def sel_blk(kb, carry):
        kv = key_scr[key_rows(kb), :]
        eq = kv == thr
        prefix = _dot(ltri, eq.astype(BF16)) + carry
        sel = ((kv > thr) | (eq & (prefix <= need))) & valid_mask(kb)
        sel_scr[key_rows(kb), :] = jnp.where(sel, 1.0, 0.0)
        return carry + jnp.sum(jnp.where(eq, 1.0, 0.0), axis=0, keepdims=True)

    lax.fori_loop(0, nkb, sel_blk, jnp.zeros((1, blk), F32))

    zero = jnp.zeros((HEAD_DIM, blk), BF16)
    outs = []
    for g in range(N_KV_HEADS):
        cols = []
        for r in range(GROUP):
            base = g * GROUP * HEAD_DIM + r * HEAD_DIM
            slab = qT_ref[0, base:base + HEAD_DIM, :]
            parts = [zero] * N_KV_HEADS
            parts[g] = slab
            cols.append(jnp.concatenate(parts, axis=0))
        qTg = jnp.concatenate(cols, axis=1)
        width = GROUP * blk

        def att_blk(kb, carry, g=g, qTg=qTg):
            m, l, acc = carry
            lg = _dot(kb_ref[0, key_rows(kb), :], qTg) * ATTN_SCALE
            lg = lg + bias_scr[jnp.minimum(qb - kb, 2), g]
            sel = jnp.concatenate([sel_scr[key_rows(kb), :]] * GROUP, axis=1) > 0.0
            m_new = jnp.maximum(m, jnp.max(jnp.where(sel, lg, NEG_BIG), axis=0, keepdims=True))
            p = jnp.where(sel, jnp.exp(lg - m_new), 0.0)
            alpha = jnp.exp(m - m_new)
            l = alpha * l + jnp.sum(p, axis=0, keepdims=True)
            vt = vTb_ref[0, kb, g * HEAD_DIM:(g + 1) * HEAD_DIM, :]
            acc = alpha * acc + _dot(vt, p.astype(BF16))
            return m_new, l, acc

        m0 = jnp.full((1, width), NEG_BIG, F32)
        l0 = jnp.zeros((1, width), F32)
        a0 = jnp.zeros((HEAD_DIM, width), F32)
        _, l, acc = lax.fori_loop(0, nkb, att_blk, (m0, l0, a0))
        og = acc / l
        outs += [og[:, r * blk:(r + 1) * blk] for r in range(GROUP)]
    o_ref[0] = jnp.concatenate(outs, axis=0).T.astype(BF16)


def _attn_prompt(rel_bias, qT, qiT, wiT, kb, vTb, kib):
    b, _, s = qT.shape
    n_sel = min(TOPK_MAX, s // 4)
    blk = LANES
    per_b = lambda a: pl.BlockSpec((1,) + a.shape[1:], lambda i, j: (i,) + (0,) * (a.ndim - 1))
    qtile = lambda rows: pl.BlockSpec((1, rows, blk), lambda i, j: (i, 0, j))
    return pl.pallas_call(
        functools.partial(_attn_prompt_kernel, n_sel),
        grid=(b, s // blk),
        in_specs=[pl.BlockSpec(memory_space=pltpu.SMEM),
                  qtile(ATTN_WIDTH), qtile(N_IDX_HEADS * IDX_DIM), qtile(8),
                  per_b(kb), per_b(vTb), per_b(kib)],
        out_specs=pl.BlockSpec((1, blk, ATTN_WIDTH), lambda i, j: (i, j, 0)),
        out_shape=jax.ShapeDtypeStruct((b, s, ATTN_WIDTH), BF16),
        scratch_shapes=[pltpu.VMEM((3, N_KV_HEADS, blk, GROUP * blk), F32),
                        pltpu.VMEM((s, blk), I32),
                        pltpu.VMEM((s, blk), F32)],
        compiler_params=_cparams("arbitrary", "arbitrary"),
        name="attn_prompt",
    )(rel_bias, qT, qiT, wiT, kb, vTb, kib)


def _attn_sample_kernel(n_sel, past, t_seq, pt_ref, relb_ref, qpad_ref, qi_ref, wi_ref,
                        knew_ref, vnew_ref, kinew_ref, ck_hbm, cv_hbm, cki_hbm,
                        o_ref, kbuf, vbuf, kibuf, sem):
    b = pl.program_id(0)
    nb = pl.num_programs(0)
    n_pages = past // PAGE_SIZE
    lp = kbuf.shape[1]
    blk = LANES
    nblk = lp // blk
    slot = b % 2

    def page_copies(seq, sl, p):
        phys = pt_ref[seq, p]
        rows = pl.ds(pl.multiple_of(p * PAGE_SIZE, PAGE_SIZE), PAGE_SIZE)
        return (pltpu.make_async_copy(ck_hbm.at[phys], kbuf.at[sl, rows, :], sem.at[sl, 0]),
                pltpu.make_async_copy(cv_hbm.at[phys], vbuf.at[sl, rows, :], sem.at[sl, 1]),
                pltpu.make_async_copy(cki_hbm.at[phys], kibuf.at[sl, rows, :], sem.at[sl, 2]))

    def start_all(seq, sl):
        def body(p, c):
            for cp in page_copies(seq, sl, p):
                cp.start()
            return c
        lax.fori_loop(0, n_pages, body, 0)

    def wait_all(seq, sl):
        def body(p, c):
            for cp in page_copies(seq, sl, p):
                cp.wait()
            return c
        lax.fori_loop(0, n_pages, body, 0)

    @pl.when(b == 0)
    def _():
        for sl in range(2):
            kbuf[sl, past:, :] = jnp.zeros((blk, KV_WIDTH), F32)
            vbuf[sl, past:, :] = jnp.zeros((blk, KV_WIDTH), F32)
            kibuf[sl, past:, :] = jnp.zeros((blk, IDX_DIM), F32)
        start_all(0, 0)

    @pl.when(b + 1 < nb)
    def _():
        start_all(b + 1, 1 - slot)

    kbuf[slot, past:past + t_seq, :] = knew_ref[0]
    vbuf[slot, past:past + t_seq, :] = vnew_ref[0]
    kibuf[slot, past:past + t_seq, :] = kinew_ref[0]
    wait_all(b, slot)

    d = _dot_nt(qi_ref[0], kibuf[slot].astype(BF16))
    s = jnp.maximum(d[0:t_seq], 0.0) * wi_ref[0, 0:t_seq, 0:1]
    for h in range(1, N_IDX_HEADS):
        s = s + jnp.maximum(d[h * t_seq:(h + 1) * t_seq], 0.0) * wi_ref[0, h * t_seq:(h + 1) * t_seq, 0:1]
    kpos = lax.broadcasted_iota(I32, (t_seq, lp), 1)
    qpos = past + lax.broadcasted_iota(I32, (t_seq, lp), 0)
    valid = kpos <= qpos
    key = jnp.where(valid, _sortable(s * IDX_SCALE), INT_MIN)

    def count(pred):
        return jnp.sum(jnp.where(pred, 1, 0), axis=1, keepdims=True)

    thr = _radix_select(lambda cand: count(key >= cand), n_sel, (t_seq, 1))
    need = (n_sel - count(key > thr)).astype(F32)
    eq = key == thr
    eqf = jnp.where(eq, 1.0, 0.0)
    utri = (lax.broadcasted_iota(I32, (blk, blk), 0) <= lax.broadcasted_iota(I32, (blk, blk), 1)).astype(BF16)
    carry = jnp.zeros((t_seq, 1), F32)
    ranks = []
    for kb in range(nblk):
        e = eqf[:, kb * blk:(kb + 1) * blk]
        ranks.append(_dot(e.astype(BF16), utri) + carry)
        carry = carry + jnp.sum(e, axis=1, keepdims=True)
    prefix = jnp.concatenate(ranks, axis=1)
    sel = jnp.where(((key > thr) | (eq & (prefix <= need))) & valid, 1.0, 0.0)

    lg = _dot_nt(qpad_ref[0], kbuf[slot].astype(BF16)) * ATTN_SCALE
    far = lp - 2 * blk
    near_dist = (past + lax.broadcasted_iota(I32, (t_seq, 2 * blk), 0)
                 - (far + lax.broadcasted_iota(I32, (t_seq, 2 * blk), 1)))
    bias_rows = []
    for h in range(N_HEADS):
        far_bias = jnp.full((t_seq, far), relb_ref[N_BUCKETS - 1, h], F32)
        bias_rows.append(jnp.concatenate([far_bias, _t5_bias(near_dist, relb_ref, h)], axis=1))
    lg = lg + jnp.concatenate(bias_rows, axis=0)
    selh = jnp.concatenate([sel] * N_HEADS, axis=0) > 0.0
    m = jnp.max(jnp.where(selh, lg, NEG_BIG), axis=1, keepdims=True)
    p = jnp.where(selh, jnp.exp(lg - m), 0.0)
    l = jnp.sum(p, axis=1, keepdims=True)
    o = _dot(p.astype(BF16), vbuf[slot].astype(BF16))
    o_ref[0] = o / l


def _attn_sample(page_table, rel_bias, qpad, qi_r, wi_r, k_new, v_new, ki_new, cache_k, cache_v, cache_ki,
                 past):
    nb, t_seq, _ = k_new.shape
    n_sel = min(TOPK_MAX, (past + t_seq) // 4)
    lp = past + LANES
    rows = N_HEADS * t_seq
    per_b = lambda a: pl.BlockSpec((1,) + a.shape[1:], lambda i, pt: (i,) + (0,) * (a.ndim - 1))
    hbm = pl.BlockSpec(memory_space=pl.ANY)
    grid_spec = pltpu.PrefetchScalarGridSpec(
        num_scalar_prefetch=1,
        grid=(nb,),
        in_specs=[pl.BlockSpec(memory_space=pltpu.SMEM),
                  per_b(qpad), per_b(qi_r), per_b(wi_r), per_b(k_new), per_b(v_new), per_b(ki_new),
                  hbm, hbm, hbm],
        out_specs=pl.BlockSpec((1, rows, KV_WIDTH), lambda i, pt: (i, 0, 0)),
        scratch_shapes=[pltpu.VMEM((2, lp, KV_WIDTH), F32),
                        pltpu.VMEM((2, lp, KV_WIDTH), F32),
                        pltpu.VMEM((2, lp, IDX_DIM), F32),
                        pltpu.SemaphoreType.DMA((2, 3))],
    )
    return pl.pallas_call(
        functools.partial(_attn_sample_kernel, n_sel, past, t_seq),
        grid_spec=grid_spec,
        out_shape=jax.ShapeDtypeStruct((nb, rows, KV_WIDTH), F32),
        compiler_params=_cparams("arbitrary"),
        name="attn_sample",
    )(page_table, rel_bias, qpad, qi_r, wi_r, k_new, v_new, ki_new, cache_k, cache_v, cache_ki)


def _merge_kernel(alpha, x_ref, oa_ref, oc_ref, ga_ref, gb_ref, g1_ref, sc2_ref, sh2_ref,
                  woa_ref, woc_ref, wout_ref, wq_ref, lng_ref, lnb_ref,
                  x1_ref, h2T_ref, pqT_ref):
    ta = _dot(oa_ref[...], woa_ref[...])
    tc = _dot(oc_ref[...], woc_ref[...])
    merged = jax.nn.sigmoid(ga_ref[...]) * ta + jax.nn.sigmoid(gb_ref[...]) * tc
    out = _dot(merged.astype(BF16), wout_ref[...])
    x1 = _layer_norm(alpha * x_ref[...] + g1_ref[0] * out, lng_ref[...], lnb_ref[...])
    x1_ref[...] = x1
    h2 = x1 * (1.0 + sc2_ref[0]) + sh2_ref[0]
    h2b = h2.astype(BF16)
    h2T_ref[...] = h2.T.astype(BF16)
    pqT_ref[...] = _dot(h2b, wq_ref[...]).T.astype(BF16)


def _merge(alpha, x, oattn, oconv, ga, gb, g1, sc2, sh2, woa, woc, wout, wq, ln_g, ln_b, rows_per_vec):
    n, d = x.shape
    tm = min(512, rows_per_vec) if g1.shape[1] == 1 else g1.shape[1]
    tiles_per_vec = rows_per_vec // tm
    tile = lambda width: pl.BlockSpec((tm, width), lambda i: (i, 0))
    vec = pl.BlockSpec((1,) + g1.shape[1:], lambda i: (i // tiles_per_vec, 0, 0))
    full = lambda a: pl.BlockSpec(a.shape, lambda i: (0,) * a.ndim)
    pq_w = wq.shape[1]
    return pl.pallas_call(
        functools.partial(_merge_kernel, alpha),
        grid=(n // tm,),
        in_specs=[tile(d), tile(ATTN_WIDTH), tile(D_CONV), tile(d), tile(d), vec, vec, vec,
                  full(woa), full(woc), full(wout), full(wq), full(ln_g), full(ln_b)],
        out_specs=(tile(d), pl.BlockSpec((d, tm), lambda i: (0, i)), pl.BlockSpec((pq_w, tm), lambda i: (0, i))),
        out_shape=(jax.ShapeDtypeStruct((n, d), F32), jax.ShapeDtypeStruct((d, n), BF16),
                   jax.ShapeDtypeStruct((pq_w, n), BF16)),
        compiler_params=_cparams("arbitrary"),
        name="merge_ln1_peerq",
    )(x, oattn, oconv, ga, gb, g1, sc2, sh2, woa, woc, wout, wq, ln_g, ln_b)


def _top16(s, v_scr):
    iota = lax.broadcasted_iota(I32, s.shape, 0)

    def body(r, carry):
        x, rank = carry
        m = jnp.max(x, axis=0, keepdims=True)
        idx = jnp.min(jnp.where(x == m, iota, N_KEYS), axis=0, keepdims=True)
        hit = iota == idx
        v_scr[pl.ds(r, 1), :] = m
        return jnp.where(hit, -jnp.inf, x), jnp.where(hit, r, rank)

    _, rank = lax.fori_loop(0, PEER_TOPK, body, (s, jnp.full(s.shape, PEER_TOPK, I32)))
    return rank


def _peer_select_kernel(pqT_ref, k1_ref, k2_ref, cnt1_ref, w1_ref, rank2_ref, e2_ref, v1_scr, v2_scr):
    tn = pqT_ref.shape[1]
    k = PEER_TOPK
    r16 = lax.broadcasted_iota(I32, (k, tn), 0)
    r8 = lax.broadcasted_iota(I32, (8, tn), 0)
    flat = jnp.concatenate([r16] + [a * k + r8 for a in range(1, 8)] + [(8 + r8) * k], axis=0)

    def head(h, c):
        base = pl.multiple_of(h * 2 * PEER_HALF, 2 * PEER_HALF)
        s1 = _dot(k1_ref[...], pqT_ref[pl.ds(base, PEER_HALF), :])
        s2 = _dot(k2_ref[...], pqT_ref[pl.ds(base + PEER_HALF, PEER_HALF), :])
        rank1 = _top16(s1, v1_scr)
        rank2 = _top16(s2, v2_scr)
        v1 = v1_scr[...]
        v2 = v2_scr[...]
        cand = jnp.concatenate([v1[0:1] + v2] + [v1[a:a + 1] + v2[0:8] for a in range(1, 8)]
                               + [v1[8:16] + v2[0:1]], axis=0)
        top = v1[0:1] + v2[0:1]

        def pick(r, carry):
            x, cnt, z = carry
            m = jnp.max(x, axis=0, keepdims=True)
            idx = jnp.min(jnp.where(x == m, flat, k * k), axis=0, keepdims=True)
            cnt = cnt + jnp.where(r16 == (idx >> 4), 1.0, 0.0)
            return jnp.where(flat == idx, -jnp.inf, x), cnt, z + jnp.exp(m - top)

        _, cnt, z = lax.fori_loop(0, k, pick, (cand, jnp.zeros((k, tn), F32), jnp.zeros((1, tn), F32)))
        cnt1 = jnp.zeros(s1.shape, F32)
        for a in range(k):
            cnt1 = jnp.where(rank1 == a, cnt[a:a + 1], cnt1)
        cnt1_ref[h] = cnt1
        w1_ref[h] = jnp.exp(s1 - v1[0:1]) / z
        rank2_ref[h] = rank2.astype(F32)
        e2_ref[h] = jnp.exp(s2 - v2[0:1])
        return c

    lax.fori_loop(0, PEER_HEADS, head, 0)


def _peer_select(pqT, k1, k2):
    _, n = pqT.shape
    tn = LANES
    tab = jax.ShapeDtypeStruct((PEER_HEADS, N_KEYS, n), F32)
    tab_spec = pl.BlockSpec((PEER_HEADS, N_KEYS, tn), lambda i: (0, 0, i))
    return pl.pallas_call(
        _peer_select_kernel,
        grid=(n // tn,),
        in_specs=[pl.BlockSpec((pqT.shape[0], tn), lambda i: (0, i)),
                  pl.BlockSpec(k1.shape, lambda i: (0, 0)),
                  pl.BlockSpec(k2.shape, lambda i: (0, 0))],
        out_specs=(tab_spec,) * 4,
        out_shape=(tab,) * 4,
        scratch_shapes=[pltpu.VMEM((PEER_TOPK, tn), F32), pltpu.VMEM((PEER_TOPK, tn), F32)],
        compiler_params=_cparams("arbitrary"),
        name="peer_select",
    )(pqT, k1, k2)


def _peer_dense_kernel(alpha, i1_per_step, h2T_ref, u_ref, vT_ref, cnt1_ref, w1_ref, rank2_ref, e2_ref,
                       x1_ref, g2_ref, lng_ref, lnb_ref, y_ref, acc_scr):
    j = pl.program_id(1)

    @pl.when(j == 0)
    def _():
        acc_scr[...] = jnp.zeros(acc_scr.shape, F32)

    aT = _dot(u_ref[...], h2T_ref[...])
    acts = []
    for ii in range(i1_per_step):
        i1 = j * i1_per_step + ii
        gate = jnp.zeros((N_KEYS, aT.shape[1]), F32)
        for h in range(PEER_HEADS):
            c1 = cnt1_ref[h, pl.ds(i1, 1), :]
            w1 = w1_ref[h, pl.ds(i1, 1), :]
            gate = gate + jnp.where(rank2_ref[h] < c1, e2_ref[h], 0.0) * w1
        a = aT[ii * N_KEYS:(ii + 1) * N_KEYS]
        acts.append((jax.nn.gelu(a) * gate).astype(BF16))
    acc_scr[...] += _dot(vT_ref[...], jnp.concatenate(acts, axis=0))

    @pl.when(j == pl.num_programs(1) - 1)
    def _():
        peer = acc_scr[...].T
        y_ref[...] = _layer_norm(alpha * x1_ref[...] + g2_ref[0] * peer, lng_ref[...], lnb_ref[...])


def _peer_dense(alpha, h2T, u_b, vT_b, tabs, x1, g2, ln_g, ln_b, rows_per_vec):
    d, n = h2T.shape
    tn = min(512, rows_per_vec) if g2.shape[1] == 1 else g2.shape[1]
    tiles_per_vec = rows_per_vec // tn
    i1_per_step = 4
    te = i1_per_step * N_KEYS
    n_exp = u_b.shape[0]
    tab_spec = pl.BlockSpec((PEER_HEADS, N_KEYS, tn), lambda i, j: (0, 0, i))
    return pl.pallas_call(
        functools.partial(_peer_dense_kernel, alpha, i1_per_step),
        grid=(n // tn, n_exp // te),
        in_specs=[pl.BlockSpec((d, tn), lambda i, j: (0, i)),
                  pl.BlockSpec((te, d), lambda i, j: (j, 0)),
                  pl.BlockSpec((d, te), lambda i, j: (0, j)),
                  tab_spec, tab_spec, tab_spec, tab_spec,
                  pl.BlockSpec((tn, d), lambda i, j: (i, 0)),
                  pl.BlockSpec((1,) + g2.shape[1:], lambda i, j: (i // tiles_per_vec, 0, 0)),
                  pl.BlockSpec(ln_g.shape, lambda i, j: (0, 0)),
                  pl.BlockSpec(ln_b.shape, lambda i, j: (0, 0))],
        out_specs=pl.BlockSpec((tn, d), lambda i, j: (i, 0)),
        out_shape=jax.ShapeDtypeStruct((n, d), F32),
        scratch_shapes=[pltpu.VMEM((d, tn), F32)],
        compiler_params=_cparams("arbitrary", "arbitrary"),
        name="peer_dense_ln2",
    )(h2T, u_b, vT_b, *tabs, x1, g2, ln_g, ln_b)


def _pack_w_in(w):
    widths = (ATTN_WIDTH, KV_WIDTH, KV_WIDTH, N_IDX_HEADS * IDX_DIM, N_IDX_HEADS, IDX_DIM,
              D_CONV, D_CONV, D_CONV, w.shape[0], w.shape[0])
    parts, start = [], 0
    for wd in widths:
        parts.append(w[:, start:start + wd])
        start += wd
    q, k, v, qi, wi, ki, bg, cg, xin, ga, gb = parts
    pad = jnp.zeros((w.shape[0], _KIWI - IDX_DIM - N_IDX_HEADS), w.dtype)
    return jnp.concatenate([q, k, v, qi, ki, wi, pad, bg, cg, xin, ga, gb], axis=1).astype(BF16)


def kernel(x_prompt, x_sample, c_prompt, c_sample, cache_k, cache_v, cache_kidx, state_conv, page_table,
           rel_bias, w_ada, b_ada, w_in, conv_w, conv_b, w_o_attn, w_o_conv, w_out, ln1_g, ln1_b,
           ln2_g, ln2_b, peer_wq, peer_k1, peer_k2, peer_u, peer_v):
    depth = w_in.shape[0]
    alpha = (2 * depth) ** 0.25
    bp, sp, d = x_prompt.shape
    bs, ts, _ = x_sample.shape
    n_pool = cache_k.shape[1]
    past = page_table.shape[1] * PAGE_SIZE
    ns = bs * ts

    xp, xs = x_prompt, x_sample.reshape(ns, d)
    outs = [[] for _ in range(8)]
    for l in range(depth):
        c_all = jnp.concatenate([c_prompt, c_sample], axis=0)
        c_all = jnp.pad(c_all, ((0, (-c_all.shape[0]) % 16), (0, 0)))
        mod = _ada(c_all, w_ada[l], b_ada[l])[:bp + bs]
        sh1, sc1, g1, sh2, sc2, g2 = jnp.split(mod, 6, axis=-1)
        as_p = lambda a: a[:bp].reshape(bp, 1, d)
        as_s = lambda a: jnp.repeat(a[bp:], ts, axis=0).reshape(1, ns, d)

        w_all = _pack_w_in(w_in[l])
        woa, woc, wout = w_o_attn[l].astype(BF16), w_o_conv[l].astype(BF16), w_out[l].astype(BF16)
        wq = peer_wq[l].astype(BF16)
        k1, k2 = peer_k1[l].astype(BF16), peer_k2[l].astype(BF16)
        u_b = peer_u[l].astype(BF16)
        vT_b = peer_v[l].T.astype(BF16)
        lg1, lb1 = ln1_g[l].reshape(1, d), ln1_b[l].reshape(1, d)
        lg2, lb2 = ln2_g[l].reshape(1, d), ln2_b[l].reshape(1, d)

        (qT, k_p, v_p, kb_p, vTb_p, qiT, ki_p, kib_p, wiT, oconv_p, ga_p, gb_p, ulast_p) = _inproj_prompt(
            xp, as_p(sc1), as_p(sh1), w_all, conv_w[l], conv_b[l])
        oattn_p = _attn_prompt(rel_bias, qT, qiT, wiT, kb_p, vTb_p, kib_p)
        x1_p, h2T_p, pqT_p = _merge(alpha, xp.reshape(bp * sp, d), oattn_p.reshape(bp * sp, ATTN_WIDTH),
                                    oconv_p.reshape(bp * sp, D_CONV), ga_p.reshape(bp * sp, d),
                                    gb_p.reshape(bp * sp, d), as_p(g1), as_p(sc2), as_p(sh2),
                                    woa, woc, wout, wq, lg1, lb1, sp)
        tabs_p = _peer_select(pqT_p, k1, k2)
        y_p = _peer_dense(alpha, h2T_p, u_b, vT_b, tabs_p, x1_p, as_p(g2), lg2, lb2, sp)
        xp = y_p.reshape(bp, sp, d)

        st = state_conv[l]
        st0 = jnp.repeat(st[:, 0], ts, axis=0)
        st1 = jnp.repeat(st[:, 1], ts, axis=0)
        (q_s, k_s, v_s, qi_s, kiwi_s, oconv_s, ga_s, gb_s, u_s) = _inproj_sample(
            xs, as_s(sc1)[0], as_s(sh1)[0], w_all, conv_w[l], conv_b[l], st0, st1, ts)
        ki_s = kiwi_s[:, :IDX_DIM]
        wi_s = kiwi_s[:, IDX_DIM:IDX_DIM + N_IDX_HEADS]
        q5 = q_s.reshape(bs, ts, N_KV_HEADS, GROUP, HEAD_DIM).transpose(0, 2, 3, 1, 4)
        qpad = jnp.zeros((bs, N_KV_HEADS, GROUP * ts, N_KV_HEADS, HEAD_DIM), F32)
        for g in range(N_KV_HEADS):
            qpad = qpad.at[:, g, :, g, :].set(q5[:, g].reshape(bs, GROUP * ts, HEAD_DIM))
        qpad = qpad.reshape(bs, N_HEADS * ts, KV_WIDTH).astype(BF16)
        qi_r = qi_s.reshape(bs, ts, N_IDX_HEADS, IDX_DIM).transpose(0, 2, 1, 3).reshape(
            bs, N_IDX_HEADS * ts, IDX_DIM).astype(BF16)
        wi_r = jnp.broadcast_to(wi_s.reshape(bs, ts, N_IDX_HEADS).transpose(0, 2, 1).reshape(
            bs, N_IDX_HEADS * ts, 1), (bs, N_IDX_HEADS * ts, LANES))
        o_s = _attn_sample(page_table, rel_bias, qpad, qi_r, wi_r,
                           k_s.reshape(bs, ts, KV_WIDTH), v_s.reshape(bs, ts, KV_WIDTH),
                           ki_s.reshape(bs, ts, IDX_DIM),
                           cache_k[l].reshape(n_pool, PAGE_SIZE, KV_WIDTH),
                           cache_v[l].reshape(n_pool, PAGE_SIZE, KV_WIDTH), cache_kidx[l], past)
        o5 = o_s.reshape(bs, N_KV_HEADS, GROUP, ts, N_KV_HEADS, HEAD_DIM)
        oattn_s = jnp.stack([o5[:, g, :, :, g, :] for g in range(N_KV_HEADS)], axis=1)
        oattn_s = oattn_s.transpose(0, 3, 1, 2, 4).reshape(ns, ATTN_WIDTH).astype(BF16)
        x1_s, h2T_s, pqT_s = _merge(alpha, xs, oattn_s, oconv_s, ga_s, gb_s, as_s(g1), as_s(sc2), as_s(sh2),
                                    woa, woc, wout, wq, lg1, lb1, ns)
        tabs_s = _peer_select(pqT_s, k1, k2)
        xs = _peer_dense(alpha, h2T_s, u_b, vT_b, tabs_s, x1_s, as_s(g2), lg2, lb2, ns)

        for lst, val in zip(outs, (
                k_p.reshape(bp, sp, N_KV_HEADS, HEAD_DIM), v_p.reshape(bp, sp, N_KV_HEADS, HEAD_DIM), ki_p,
                ulast_p[:, 8 - (CONV_WIDTH - 1):],
                k_s.reshape(bs, ts, N_KV_HEADS, HEAD_DIM), v_s.reshape(bs, ts, N_KV_HEADS, HEAD_DIM),
                ki_s.reshape(bs, ts, IDX_DIM),
                u_s.reshape(bs, ts, D_CONV)[:, ts - (CONV_WIDTH - 1):])):
            lst.append(val)

    return (xp, xs.reshape(bs, ts, d)) + tuple(jnp.stack(o) for o in outs)
```

```python
import functools
import math

import jax
import jax.numpy as jnp
from jax import lax
from jax.experimental import pallas as pl
from jax.experimental.pallas import tpu as pltpu

F32 = jnp.float32
BF16 = jnp.bfloat16
I32 = jnp.int32

N_HEADS = 8
N_KV_HEADS = 2
HEAD_DIM = 64
GROUP = N_HEADS // N_KV_HEADS
ATTN_WIDTH = N_HEADS * HEAD_DIM
KV_WIDTH = N_KV_HEADS * HEAD_DIM
ATTN_SCALE = HEAD_DIM ** -0.5
N_IDX_HEADS = 4
IDX_DIM = 64
IDX_SCALE = (IDX_DIM * N_IDX_HEADS) ** -0.5
TOPK_MAX = 256
N_BUCKETS = 32
MAX_DISTANCE = 128
D_CONV = 512
CONV_WIDTH = 3
N_KEYS = 128
PEER_HEADS = 8
PEER_HALF = 64
PEER_TOPK = 16
LN_EPS = 1e-5
PAGE_SIZE = 128

LANES = 128
BF16_ROWS = 16
VMEM_LIMIT = 56 * 1024 * 1024

_GELU_C1 = math.sqrt(2.0 / math.pi)
_GELU_C2 = _GELU_C1 * 0.044715

INT_MIN = -2 ** 31
NEG_BIG = -1e30

_KIWI = LANES
_OFF_Q = 0
_OFF_K = _OFF_Q + ATTN_WIDTH
_OFF_V = _OFF_K + KV_WIDTH
_OFF_QI = _OFF_V + KV_WIDTH
_OFF_KIWI = _OFF_QI + N_IDX_HEADS * IDX_DIM
_OFF_BG = _OFF_KIWI + _KIWI
_OFF_CG = _OFF_BG + D_CONV
_OFF_XIN = _OFF_CG + D_CONV
_OFF_GA = _OFF_XIN + D_CONV


def _cparams(*sem):
    return pltpu.CompilerParams(dimension_semantics=sem, vmem_limit_bytes=VMEM_LIMIT)


def _dot(a, b):
    return jnp.dot(a, b, preferred_element_type=F32)


def _dot_nt(a, b):
    return lax.dot_general(a, b, (((1,), (1,)), ((), ())), preferred_element_type=F32)


def _sortable(s):
    b = pltpu.bitcast(s + 0.0, I32)
    return b ^ ((b >> 31) & 0x7FFFFFFF)


def _radix_select(count_ge, k, shape):
    t0 = jnp.where(count_ge(jnp.zeros(shape, I32)) >= k, 0, INT_MIN).astype(I32)

    def body(i, t):
        cand = t | (jnp.int32(1) << (30 - i))
        return jnp.where(count_ge(cand) >= k, cand, t)

    return lax.fori_loop(0, 31, body, t0)


def _t5_bias(dist, relb_ref, h):
    n = jnp.maximum(dist, 0)
    max_exact = N_BUCKETS // 2
    nf = jnp.maximum(n, 1).astype(F32)
    large = max_exact + (jnp.log(nf / max_exact) / math.log(MAX_DISTANCE / max_exact)
                         * (N_BUCKETS - max_exact)).astype(I32)
    large = jnp.minimum(large, N_BUCKETS - 1)
    bucket = jnp.where(n < max_exact, n, large)
    out = jnp.zeros(dist.shape, F32)
    for b in range(N_BUCKETS):
        out = jnp.where(bucket == b, relb_ref[b, h], out)
    return out


def _layer_norm(x, g, b):
    mu = jnp.mean(x, axis=-1, keepdims=True)
    var = jnp.mean(jnp.square(x - mu), axis=-1, keepdims=True)
    return (x - mu) * lax.rsqrt(var + LN_EPS) * g + b


def _ada_kernel(c_ref, w_ref, b_ref, o_ref):
    o_ref[...] = _dot(c_ref[...].astype(BF16), w_ref[...].astype(BF16)) + b_ref[...]


def _ada(c, w, b):
    m, d = c.shape
    n = w.shape[1]
    tn = n // 4
    return pl.pallas_call(
        _ada_kernel,
        grid=(n // tn,),
        in_specs=[pl.BlockSpec((m, d), lambda j: (0, 0)),
                  pl.BlockSpec((d, tn), lambda j: (0, j)),
                  pl.BlockSpec((1, tn), lambda j: (0, j))],
        out_specs=pl.BlockSpec((m, tn), lambda j: (0, j)),
        out_shape=jax.ShapeDtypeStruct((m, n), F32),
        compiler_params=_cparams("arbitrary"),
        name="ada_mod",
    )(c, w, b.reshape(1, n))


def _conv_out(bg, u, u1, u2, cw_ref, cb_ref):
    y = cb_ref[...] + u2 * cw_ref[0:1, :] + u1 * cw_ref[1:2, :] + u * cw_ref[2:3, :]
    return bg * y


def _inproj_prompt_kernel(x_ref, sc_ref, sh_ref, w_ref, cw_ref, cb_ref,
                          qT_ref, k_ref, v_ref, kb_ref, vTb_ref, qiT_ref, ki_ref, kib_ref, wiT_ref,
                          oconv_ref, ga_ref, gb_ref, ulast_ref, uprev_scr):
    s = pl.program_id(1)
    tm = x_ref.shape[1]
    h = (x_ref[0] * (1.0 + sc_ref[0]) + sh_ref[0]).astype(BF16)

    def proj(a, width):
        return _dot(h, w_ref[:, a:a + width])

    qT_ref[0] = proj(_OFF_Q, ATTN_WIDTH).T.astype(BF16)
    k = proj(_OFF_K, KV_WIDTH)
    k_ref[0] = k
    kb_ref[0] = k.astype(BF16)
    v = proj(_OFF_V, KV_WIDTH)
    v_ref[0] = v
    for i in range(tm // LANES):
        vTb_ref[0, i] = v[i * LANES:(i + 1) * LANES].T.astype(BF16)
    qiT_ref[0] = proj(_OFF_QI, N_IDX_HEADS * IDX_DIM).T.astype(BF16)
    kiwi = proj(_OFF_KIWI, _KIWI)
    ki_ref[0] = kiwi[:, :IDX_DIM]
    kib_ref[0] = kiwi[:, :IDX_DIM].astype(BF16)
    wiT_ref[0] = kiwi.T[IDX_DIM:IDX_DIM + 8]

    @pl.when(s == 0)
    def _():
        uprev_scr[...] = jnp.zeros(uprev_scr.shape, F32)

    bg = proj(_OFF_BG, D_CONV)
    u = proj(_OFF_CG, D_CONV) * proj(_OFF_XIN, D_CONV)
    row = lax.broadcasted_iota(I32, u.shape, 0)
    up = uprev_scr[...]
    u1 = jnp.where(row < 1, pltpu.roll(up, 1, 0), pltpu.roll(u, 1, 0))
    u2 = jnp.where(row < 2, pltpu.roll(up, 2, 0), pltpu.roll(u, 2, 0))
    oconv_ref[0] = _conv_out(bg, u, u1, u2, cw_ref, cb_ref).astype(BF16)
    uprev_scr[...] = u
    ulast_ref[0] = u[tm - 8:, :]

    ga_ref[0] = proj(_OFF_GA, w_ref.shape[0])
    gb_ref[0] = proj(_OFF_GA + w_ref.shape[0], w_ref.shape[0])


def _inproj_prompt(x, sc, sh, w_all, conv_w, conv_b):
    b, s, d = x.shape
    tm = min(256, s)
    nkb = tm // LANES
    row = lambda width, dt: jax.ShapeDtypeStruct((b, s, width), dt)
    out_shape = (
        jax.ShapeDtypeStruct((b, ATTN_WIDTH, s), BF16),
        row(KV_WIDTH, F32), row(KV_WIDTH, F32), row(KV_WIDTH, BF16),
        jax.ShapeDtypeStruct((b, s // LANES, KV_WIDTH, LANES), BF16),
        jax.ShapeDtypeStruct((b, N_IDX_HEADS * IDX_DIM, s), BF16),
        row(IDX_DIM, F32), row(IDX_DIM, BF16),
        jax.ShapeDtypeStruct((b, 8, s), F32),
        row(D_CONV, BF16),
        row(d, F32), row(d, F32),
        jax.ShapeDtypeStruct((b, 8, D_CONV), F32),
    )
    tile = lambda width: pl.BlockSpec((1, tm, width), lambda i, j: (i, j, 0))
    tileT = lambda rows: pl.BlockSpec((1, rows, tm), lambda i, j: (i, 0, j))
    out_specs = (
        tileT(ATTN_WIDTH), tile(KV_WIDTH), tile(KV_WIDTH), tile(KV_WIDTH),
        pl.BlockSpec((1, nkb, KV_WIDTH, LANES), lambda i, j: (i, j, 0, 0)),
        tileT(N_IDX_HEADS * IDX_DIM), tile(IDX_DIM), tile(IDX_DIM), tileT(8),
        tile(D_CONV), tile(d), tile(d),
        pl.BlockSpec((1, 8, D_CONV), lambda i, j: (i, 0, 0)),
    )
    vec = pl.BlockSpec((1, 1, d), lambda i, j: (i, 0, 0))
    return pl.pallas_call(
        _inproj_prompt_kernel,
        grid=(b, s // tm),
        in_specs=[tile(d), vec, vec,
                  pl.BlockSpec(w_all.shape, lambda i, j: (0, 0)),
                  pl.BlockSpec(conv_w.shape, lambda i, j: (0, 0)),
                  pl.BlockSpec((1, D_CONV), lambda i, j: (0, 0))],
        out_specs=out_specs,
        out_shape=out_shape,
        scratch_shapes=[pltpu.VMEM((tm, D_CONV), F32)],
        compiler_params=_cparams("arbitrary", "arbitrary"),
        name="inproj_prompt",
    )(x, sc, sh, w_all, conv_w, conv_b.reshape(1, D_CONV))


def _inproj_sample_kernel(t_seq, x_ref, sc_ref, sh_ref, w_ref, cw_ref, cb_ref, s0_ref, s1_ref,
                          q_ref, k_ref, v_ref, qi_ref, kiwi_ref, oconv_ref, ga_ref, gb_ref, u_ref):
    h = (x_ref[...] * (1.0 + sc_ref[...]) + sh_ref[...]).astype(BF16)

    def proj(a, width):
        return _dot(h, w_ref[:, a:a + width])

    q_ref[...] = proj(_OFF_Q, ATTN_WIDTH)
    k_ref[...] = proj(_OFF_K, KV_WIDTH)
    v_ref[...] = proj(_OFF_V, KV_WIDTH)
    qi_ref[...] = proj(_OFF_QI, N_IDX_HEADS * IDX_DIM)
    kiwi_ref[...] = proj(_OFF_KIWI, _KIWI)
    bg = proj(_OFF_BG, D_CONV)
    u = proj(_OFF_CG, D_CONV) * proj(_OFF_XIN, D_CONV)
    t = lax.broadcasted_iota(I32, u.shape, 0) % t_seq
    u1 = jnp.where(t == 0, s1_ref[...], pltpu.roll(u, 1, 0))
    u2 = jnp.where(t == 0, s0_ref[...], jnp.where(t == 1, s1_ref[...], pltpu.roll(u, 2, 0)))
    oconv_ref[...] = _conv_out(bg, u, u1, u2, cw_ref, cb_ref).astype(BF16)
    u_ref[...] = u
    ga_ref[...] = proj(_OFF_GA, w_ref.shape[0])
    gb_ref[...] = proj(_OFF_GA + w_ref.shape[0], w_ref.shape[0])


def _inproj_sample(x, sc_rows, sh_rows, w_all, conv_w, conv_b, st0_rows, st1_rows, t_seq):
    n, d = x.shape
    full = lambda a: pl.BlockSpec(a.shape, lambda i: (0,) * a.ndim)
    o = lambda width, dt: jax.ShapeDtypeStruct((n, width), dt)
    out_shape = (o(ATTN_WIDTH, F32), o(KV_WIDTH, F32), o(KV_WIDTH, F32), o(N_IDX_HEADS * IDX_DIM, F32),
                 o(_KIWI, F32), o(D_CONV, BF16), o(d, F32), o(d, F32), o(D_CONV, F32))
    args = (x, sc_rows, sh_rows, w_all, conv_w, conv_b.reshape(1, D_CONV), st0_rows, st1_rows)
    return pl.pallas_call(
        functools.partial(_inproj_sample_kernel, t_seq),
        grid=(1,),
        in_specs=[full(a) for a in args],
        out_specs=tuple(pl.BlockSpec(s.shape, lambda i: (0, 0)) for s in out_shape),
        out_shape=out_shape,
        compiler_params=_cparams("arbitrary"),
        name="inproj_sample",
    )(*args)


def _attn_prompt_kernel(n_sel, relb_ref, qT_ref, qiT_ref, wiT_ref, kb_ref, vTb_ref, kib_ref,
                        o_ref, bias_scr, key_scr, sel_scr):
    qb = pl.program_id(1)
    blk = LANES
    kblk = 2 * blk
    ntrip = qb // 2 + 1

    @pl.when((pl.program_id(0) == 0) & (qb == 0))
    def _():
        j = lax.broadcasted_iota(I32, (blk, blk), 0)
        i = lax.broadcasted_iota(I32, (blk, blk), 1)
        for delta in range(3):
            dist = i - j + blk * delta
            for h in range(N_HEADS):
                g, r = divmod(h, GROUP)
                bias_scr[delta, g, :, r * blk:(r + 1) * blk] = _t5_bias(dist, relb_ref, h)

    row = lax.broadcasted_iota(I32, (kblk, blk), 0)
    lane = lax.broadcasted_iota(I32, (kblk, blk), 1)

    def valid_mask(t):
        return (t * kblk + row) <= (qb * blk + lane)

    def key_rows(t):
        return pl.ds(pl.multiple_of(t * kblk, kblk), kblk)

    qiT = jnp.concatenate([qiT_ref[0, h * IDX_DIM:(h + 1) * IDX_DIM, :] for h in range(N_IDX_HEADS)], axis=1)
    wi = [wiT_ref[0, h:h + 1, :] for h in range(N_IDX_HEADS)]

    def score_blk(t, c):
        d = _dot(kib_ref[0, key_rows(t), :], qiT)
        s = jnp.maximum(d[:, 0:blk], 0.0) * wi[0]
        for h in range(1, N_IDX_HEADS):
            s = s + jnp.maximum(d[:, h * blk:(h + 1) * blk], 0.0) * wi[h]
        key = _sortable(s * IDX_SCALE)
        key_scr[key_rows(t), :] = jnp.where(valid_mask(t), key, INT_MIN)
        return c

    lax.fori_loop(0, ntrip, score_blk, 0)

    def count(pred):
        def body(t, acc):
            return acc + jnp.where(pred(key_scr[key_rows(t), :]), 1, 0)
        acc = lax.fori_loop(0, ntrip, body, jnp.zeros((kblk, blk), I32))
        return jnp.sum(acc, axis=0, keepdims=True)

    thr = _radix_select(lambda cand: count(lambda kv: kv >= cand), n_sel, (1, blk))

    need = (n_sel - count(lambda kv: kv > thr)).astype(F32)
    ltri = (lax.broadcasted_iota(I32, (kblk, kblk), 1) <= lax.broadcasted_iota(I32, (kblk, kblk), 0)).astype(BF16)

    def sel_blk(t, carry):
        kv = key_scr[key_rows(t), :]
        eq = kv == thr
        prefix = _dot(ltri, eq.astype(BF16)) + carry
        sel = ((kv > thr) | (eq & (prefix <= need))) & valid_mask(t)
        sel_scr[key_rows(t), :] = jnp.where(sel, 0.0, NEG_BIG)
        return prefix[kblk - 1:kblk, :]

    lax.fori_loop(0, ntrip, sel_blk, jnp.zeros((1, blk), F32))

    zero = jnp.zeros((HEAD_DIM, blk), BF16)
    scale = jnp.asarray(ATTN_SCALE, BF16)
    qTg = []
    for g in range(N_KV_HEADS):
        cols = []
        for r in range(GROUP):
            base = g * GROUP * HEAD_DIM + r * HEAD_DIM
            parts = [zero] * N_KV_HEADS
            parts[g] = qT_ref[0, base:base + HEAD_DIM, :] * scale
            cols.append(jnp.concatenate(parts, axis=0))
        qTg.append(jnp.concatenate(cols, axis=1))
    width = GROUP * blk

    def att_blk(t, carry):
        kmat = kb_ref[0, key_rows(t), :]
        mask = jnp.concatenate([sel_scr[key_rows(t), :]] * GROUP, axis=1)
        d0 = jnp.clip(qb - 2 * t, 0, 2)
        d1 = jnp.clip(qb - 2 * t - 1, 0, 2)
        new = []
        for g in range(N_KV_HEADS):
            m, l, acc = carry[g]
            bias = jnp.concatenate([bias_scr[d0, g], bias_scr[d1, g]], axis=0)
            lg = _dot(kmat, qTg[g]) + bias + mask
            m_new = jnp.maximum(m, jnp.max(lg, axis=0, keepdims=True))
            p = jnp.exp(lg - m_new)
            alpha = jnp.exp(m - m_new)
            l = alpha * l + jnp.sum(p, axis=0, keepdims=True)
            rows = slice(g * HEAD_DIM, (g + 1) * HEAD_DIM)
            vt = jnp.concatenate([vTb_ref[0, 2 * t, rows, :], vTb_ref[0, 2 * t + 1, rows, :]], axis=1)
            acc = alpha * acc + _dot(vt, p.astype(BF16))
            new.append((m_new, l, acc))
        return tuple(new)

    init = (jnp.full((1, width), NEG_BIG, F32), jnp.zeros((1, width), F32), jnp.zeros((HEAD_DIM, width), F32))
    res = lax.fori_loop(0, ntrip, att_blk, (init,) * N_KV_HEADS)
    outs = []
    for g in range(N_KV_HEADS):
        _, l, acc = res[g]
        og = acc / l
        outs += [og[:, r * blk:(r + 1) * blk] for r in range(GROUP)]
    o_ref[0] = jnp.concatenate(outs, axis=0).T.astype(BF16)


def _attn_prompt(rel_bias, qT, qiT, wiT, kb, vTb, kib):
    b, _, s = qT.shape
    n_sel = min(TOPK_MAX, s // 4)
    blk = LANES
    per_b = lambda a: pl.BlockSpec((1,) + a.shape[1:], lambda i, j: (i,) + (0,) * (a.ndim - 1))
    qtile = lambda rows: pl.BlockSpec((1, rows, blk), lambda i, j: (i, 0, j))
    return pl.pallas_call(
        functools.partial(_attn_prompt_kernel, n_sel),
        grid=(b, s // blk),
        in_specs=[pl.BlockSpec(memory_space=pltpu.SMEM),
                  qtile(ATTN_WIDTH), qtile(N_IDX_HEADS * IDX_DIM), qtile(8),
                  per_b(kb), per_b(vTb), per_b(kib)],
        out_specs=pl.BlockSpec((1, blk, ATTN_WIDTH), lambda i, j: (i, j, 0)),
        out_shape=jax.ShapeDtypeStruct((b, s, ATTN_WIDTH), BF16),
        scratch_shapes=[pltpu.VMEM((3, N_KV_HEADS, blk, GROUP * blk), F32),
                        pltpu.VMEM((s, blk), I32),
                        pltpu.VMEM((s, blk), F32)],
        compiler_params=_cparams("arbitrary", "arbitrary"),
        name="attn_prompt",
    )(rel_bias, qT, qiT, wiT, kb, vTb, kib)


def _attn_sample_kernel(n_sel, past, t_seq, pt_ref, relb_ref, qpad_ref, qi_ref, wi_ref,
                        knew_ref, vnew_ref, kinew_ref, ck_hbm, cv_hbm, cki_hbm,
                        o_ref, kbuf, vbuf, kibuf, sem):
    b = pl.program_id(0)
    nb = pl.num_programs(0)
    n_pages = past // PAGE_SIZE
    lp = kbuf.shape[1]
    blk = LANES
    nblk = lp // blk
    slot = b % 2

    def page_copies(seq, sl, p):
        phys = pt_ref[seq, p]
        rows = pl.ds(pl.multiple_of(p * PAGE_SIZE, PAGE_SIZE), PAGE_SIZE)
        return (pltpu.make_async_copy(ck_hbm.at[phys], kbuf.at[sl, rows, :], sem.at[sl, 0]),
                pltpu.make_async_copy(cv_hbm.at[phys], vbuf.at[sl, rows, :], sem.at[sl, 1]),
                pltpu.make_async_copy(cki_hbm.at[phys], kibuf.at[sl, rows, :], sem.at[sl, 2]))

    def start_all(seq, sl):
        def body(p, c):
            for cp in page_copies(seq, sl, p):
                cp.start()
            return c
        lax.fori_loop(0, n_pages, body, 0)

    def wait_all(seq, sl):
        def body(p, c):
            for cp in page_copies(seq, sl, p):
                cp.wait()
            return c
        lax.fori_loop(0, n_pages, body, 0)

    @pl.when(b == 0)
    def _():
        for sl in range(2):
            kbuf[sl, past:, :] = jnp.zeros((blk, KV_WIDTH), F32)
            vbuf[sl, past:, :] = jnp.zeros((blk, KV_WIDTH), F32)
            kibuf[sl, past:, :] = jnp.zeros((blk, IDX_DIM), F32)
        start_all(0, 0)

    @pl.when(b + 1 < nb)
    def _():
        start_all(b + 1, 1 - slot)

    kbuf[slot, past:past + t_seq, :] = knew_ref[0]
    vbuf[slot, past:past + t_seq, :] = vnew_ref[0]
    kibuf[slot, past:past + t_seq, :] = kinew_ref[0]
    wait_all(b, slot)

    d = _dot_nt(qi_ref[0], kibuf[slot].astype(BF16))
    s = jnp.maximum(d[0:t_seq], 0.0) * wi_ref[0, 0:t_seq, 0:1]
    for h in range(1, N_IDX_HEADS):
        s = s + jnp.maximum(d[h * t_seq:(h + 1) * t_seq], 0.0) * wi_ref[0, h * t_seq:(h + 1) * t_seq, 0:1]
    kpos = lax.broadcasted_iota(I32, (t_seq, lp), 1)
    qpos = past + lax.broadcasted_iota(I32, (t_seq, lp), 0)
    valid = kpos <= qpos
    key = jnp.where(valid, _sortable(s * IDX_SCALE), INT_MIN)

    def count(pred):
        return jnp.sum(jnp.where(pred, 1, 0), axis=1, keepdims=True)

    thr = _radix_select(lambda cand: count(key >= cand), n_sel, (t_seq, 1))
    need = (n_sel - count(key > thr)).astype(F32)
    eq = key == thr
    eqf = jnp.where(eq, 1.0, 0.0)
    utri = (lax.broadcasted_iota(I32, (blk, blk), 0) <= lax.broadcasted_iota(I32, (blk, blk), 1)).astype(BF16)
    carry = jnp.zeros((t_seq, 1), F32)
    ranks = []
    for kb in range(nblk):
        e = eqf[:, kb * blk:(kb + 1) * blk]
        ranks.append(_dot(e.astype(BF16), utri) + carry)
        carry = carry + jnp.sum(e, axis=1, keepdims=True)
    prefix = jnp.concatenate(ranks, axis=1)
    sel = jnp.where(((key > thr) | (eq & (prefix <= need))) & valid, 1.0, 0.0)

    lg = _dot_nt(qpad_ref[0], kbuf[slot].astype(BF16)) * ATTN_SCALE
    far = lp - 2 * blk
    near_dist = (past + lax.broadcasted_iota(I32, (t_seq, 2 * blk), 0)
                 - (far + lax.broadcasted_iota(I32, (t_seq, 2 * blk), 1)))
    bias_rows = []
    for h in range(N_HEADS):
        far_bias = jnp.full((t_seq, far), relb_ref[N_BUCKETS - 1, h], F32)
        bias_rows.append(jnp.concatenate([far_bias, _t5_bias(near_dist, relb_ref, h)], axis=1))
    lg = lg + jnp.concatenate(bias_rows, axis=0)
    selh = jnp.concatenate([sel] * N_HEADS, axis=0) > 0.0
    m = jnp.max(jnp.where(selh, lg, NEG_BIG), axis=1, keepdims=True)
    p = jnp.where(selh, jnp.exp(lg - m), 0.0)
    l = jnp.sum(p, axis=1, keepdims=True)
    o = _dot(p.astype(BF16), vbuf[slot].astype(BF16))
    o_ref[0] = o / l


def _attn_sample(page_table, rel_bias, qpad, qi_r, wi_r, k_new, v_new, ki_new, cache_k, cache_v, cache_ki,
                 past):
    nb, t_seq, _ = k_new.shape
    n_sel = min(TOPK_MAX, (past + t_seq) // 4)
    lp = past + LANES
    rows = N_HEADS * t_seq
    per_b = lambda a: pl.BlockSpec((1,) + a.shape[1:], lambda i, pt: (i,) + (0,) * (a.ndim - 1))
    hbm = pl.BlockSpec(memory_space=pl.ANY)
    grid_spec = pltpu.PrefetchScalarGridSpec(
        num_scalar_prefetch=1,
        grid=(nb,),
        in_specs=[pl.BlockSpec(memory_space=pltpu.SMEM),
                  per_b(qpad), per_b(qi_r), per_b(wi_r), per_b(k_new), per_b(v_new), per_b(ki_new),
                  hbm, hbm, hbm],
        out_specs=pl.BlockSpec((1, rows, KV_WIDTH), lambda i, pt: (i, 0, 0)),
        scratch_shapes=[pltpu.VMEM((2, lp, KV_WIDTH), F32),
                        pltpu.VMEM((2, lp, KV_WIDTH), F32),
                        pltpu.VMEM((2, lp, IDX_DIM), F32),
                        pltpu.SemaphoreType.DMA((2, 3))],
    )
    return pl.pallas_call(
        functools.partial(_attn_sample_kernel, n_sel, past, t_seq),
        grid_spec=grid_spec,
        out_shape=jax.ShapeDtypeStruct((nb, rows, KV_WIDTH), F32),
        compiler_params=_cparams("arbitrary"),
        name="attn_sample",
    )(page_table, rel_bias, qpad, qi_r, wi_r, k_new, v_new, ki_new, cache_k, cache_v, cache_ki)


def _merge_kernel(alpha, x_ref, oa_ref, oc_ref, ga_ref, gb_ref, g1_ref, sc2_ref, sh2_ref,
                  woa_ref, woc_ref, wout_ref, wq_ref, lng_ref, lnb_ref,
                  x1_ref, h2T_ref, pqT_ref):
    ta = _dot(oa_ref[...], woa_ref[...])
    tc = _dot(oc_ref[...], woc_ref[...])
    merged = jax.nn.sigmoid(ga_ref[...]) * ta + jax.nn.sigmoid(gb_ref[...]) * tc
    out = _dot(merged.astype(BF16), wout_ref[...])
    x1 = _layer_norm(alpha * x_ref[...] + g1_ref[0] * out, lng_ref[...], lnb_ref[...])
    x1_ref[...] = x1
    h2 = x1 * (1.0 + sc2_ref[0]) + sh2_ref[0]
    h2b = h2.astype(BF16)
    h2T_ref[...] = h2.T.astype(BF16)
    pqT_ref[...] = _dot(h2b, wq_ref[...]).T.astype(BF16)


def _merge(alpha, x, oattn, oconv, ga, gb, g1, sc2, sh2, woa, woc, wout, wq, ln_g, ln_b, rows_per_vec):
    n, d = x.shape
    tm = min(512, rows_per_vec) if g1.shape[1] == 1 else g1.shape[1]
    tiles_per_vec = rows_per_vec // tm
    tile = lambda width: pl.BlockSpec((tm, width), lambda i: (i, 0))
    vec = pl.BlockSpec((1,) + g1.shape[1:], lambda i: (i // tiles_per_vec, 0, 0))
    full = lambda a: pl.BlockSpec(a.shape, lambda i: (0,) * a.ndim)
    pq_w = wq.shape[1]
    return pl.pallas_call(
        functools.partial(_merge_kernel, alpha),
        grid=(n // tm,),
        in_specs=[tile(d), tile(ATTN_WIDTH), tile(D_CONV), tile(d), tile(d), vec, vec, vec,
                  full(woa), full(woc), full(wout), full(wq), full(ln_g), full(ln_b)],
        out_specs=(tile(d), pl.BlockSpec((d, tm), lambda i: (0, i)), pl.BlockSpec((pq_w, tm), lambda i: (0, i))),
        out_shape=(jax.ShapeDtypeStruct((n, d), F32), jax.ShapeDtypeStruct((d, n), BF16),
                   jax.ShapeDtypeStruct((pq_w, n), BF16)),
        compiler_params=_cparams("arbitrary"),
        name="merge_ln1_peerq",
    )(x, oattn, oconv, ga, gb, g1, sc2, sh2, woa, woc, wout, wq, ln_g, ln_b)


def _top16(s, v_scr):
    iota = lax.broadcasted_iota(I32, s.shape, 0)

    def body(r, carry):
        x, rank = carry
        m = jnp.max(x, axis=0, keepdims=True)
        idx = jnp.min(jnp.where(x == m, iota, N_KEYS), axis=0, keepdims=True)
        hit = iota == idx
        v_scr[pl.ds(r, 1), :] = m
        return jnp.where(hit, -jnp.inf, x), jnp.where(hit, r, rank)

    _, rank = lax.fori_loop(0, PEER_TOPK, body, (s, jnp.full(s.shape, PEER_TOPK, I32)))
    return rank


def _peer_select_kernel(pqT_ref, k1_ref, k2_ref, cnt1_ref, w1_ref, rank2_ref, e2_ref, v1_scr, v2_scr):
    tn = pqT_ref.shape[1]
    k = PEER_TOPK
    r16 = lax.broadcasted_iota(I32, (k, tn), 0)
    r8 = lax.broadcasted_iota(I32, (8, tn), 0)
    flat = jnp.concatenate([r16] + [a * k + r8 for a in range(1, 8)] + [(8 + r8) * k], axis=0)

    def head(h, c):
        base = pl.multiple_of(h * 2 * PEER_HALF, 2 * PEER_HALF)
        s1 = _dot(k1_ref[...], pqT_ref[pl.ds(base, PEER_HALF), :])
        s2 = _dot(k2_ref[...], pqT_ref[pl.ds(base + PEER_HALF, PEER_HALF), :])
        rank1 = _top16(s1, v1_scr)
        rank2 = _top16(s2, v2_scr)
        v1 = v1_scr[...]
        v2 = v2_scr[...]
        cand = jnp.concatenate([v1[0:1] + v2] + [v1[a:a + 1] + v2[0:8] for a in range(1, 8)]
                               + [v1[8:16] + v2[0:1]], axis=0)
        top = v1[0:1] + v2[0:1]

        def pick(r, carry):
            x, cnt, z = carry
            m = jnp.max(x, axis=0, keepdims=True)
            idx = jnp.min(jnp.where(x == m, flat, k * k), axis=0, keepdims=True)
            cnt = cnt + jnp.where(r16 == (idx >> 4), 1.0, 0.0)
            return jnp.where(flat == idx, -jnp.inf, x), cnt, z + jnp.exp(m - top)

        _, cnt, z = lax.fori_loop(0, k, pick, (cand, jnp.zeros((k, tn), F32), jnp.zeros((1, tn), F32)))
        cnt1 = jnp.zeros(s1.shape, F32)
        for a in range(k):
            cnt1 = jnp.where(rank1 == a, cnt[a:a + 1], cnt1)
        cnt1_ref[h] = cnt1
        w1_ref[h] = jnp.exp(s1 - v1[0:1]) * (0.5 / z)
        tiles = (N_KEYS // BF16_ROWS, BF16_ROWS, tn)
        rank2_ref[h] = rank2.astype(F32).astype(BF16).reshape(tiles)
        e2_ref[h] = jnp.exp(s2 - v2[0:1]).astype(BF16).reshape(tiles)
        return c

    lax.fori_loop(0, PEER_HEADS, head, 0)


def _peer_select(pqT, k1, k2):
    _, n = pqT.shape
    tn = LANES
    tab = jax.ShapeDtypeStruct((PEER_HEADS, N_KEYS, n), F32)
    tab_spec = pl.BlockSpec((PEER_HEADS, N_KEYS, tn), lambda i: (0, 0, i))
    tabb = jax.ShapeDtypeStruct((PEER_HEADS, N_KEYS // BF16_ROWS, BF16_ROWS, n), BF16)
    tabb_spec = pl.BlockSpec((PEER_HEADS, N_KEYS // BF16_ROWS, BF16_ROWS, tn), lambda i: (0, 0, 0, i))
    return pl.pallas_call(
        _peer_select_kernel,
        grid=(n // tn,),
        in_specs=[pl.BlockSpec((pqT.shape[0], tn), lambda i: (0, i)),
                  pl.BlockSpec(k1.shape, lambda i: (0, 0)),
                  pl.BlockSpec(k2.shape, lambda i: (0, 0))],
        out_specs=(tab_spec, tab_spec, tabb_spec, tabb_spec),
        out_shape=(tab, tab, tabb, tabb),
        scratch_shapes=[pltpu.VMEM((PEER_TOPK, tn), F32), pltpu.VMEM((PEER_TOPK, tn), F32)],
        compiler_params=_cparams("arbitrary"),
        name="peer_select",
    )(pqT, k1, k2)


def _peer_dense_kernel(alpha, i1_per_step, h2T_ref, u_ref, vT_ref, cnt1_ref, w1_ref, rank2_ref, e2_ref,
                       x1_ref, g2_ref, lng_ref, lnb_ref, y_ref, acc_scr, aT_scr, act_scr):
    j = pl.program_id(1)

    @pl.when(j == 0)
    def _():
        acc_scr[...] = jnp.zeros(acc_scr.shape, F32)

    aT_scr[...] = _dot(u_ref[...], h2T_ref[...])
    tn = aT_scr.shape[1]
    tiles = (N_KEYS // BF16_ROWS, BF16_ROWS, tn)
    zero = jnp.zeros(tiles, BF16)
    def one_i1(ii, carry):
        i1 = j * i1_per_step + ii
        gate = zero
        for h in range(PEER_HEADS):
            c1 = jnp.broadcast_to(cnt1_ref[h, pl.ds(i1, 1), :], (BF16_ROWS, tn)).astype(BF16)
            w1 = jnp.broadcast_to(w1_ref[h, pl.ds(i1, 1), :], (BF16_ROWS, tn)).astype(BF16)
            gate = gate + jnp.where(rank2_ref[h] < c1[None], e2_ref[h], zero) * w1[None]
        rows = pl.ds(pl.multiple_of(ii * N_KEYS, N_KEYS), N_KEYS)
        a = aT_scr[rows, :].astype(BF16)
        t = jnp.tanh(a * (_GELU_C1 + _GELU_C2 * (a * a)))
        act_scr[rows, :] = ((a + a * t).reshape(tiles) * gate).reshape(N_KEYS, tn)
        return carry

    lax.fori_loop(0, i1_per_step, one_i1, 0)
    acc_scr[...] += _dot(vT_ref[...], act_scr[...])

    @pl.when(j == pl.num_programs(1) - 1)
    def _():
        peer = acc_scr[...].T
        y_ref[...] = _layer_norm(alpha * x1_ref[...] + g2_ref[0] * peer, lng_ref[...], lnb_ref[...])


def _peer_dense(alpha, h2T, u_b, vT_b, tabs, x1, g2, ln_g, ln_b, rows_per_vec):
    d, n = h2T.shape
    tn = min(512, rows_per_vec) if g2.shape[1] == 1 else g2.shape[1]
    tiles_per_vec = rows_per_vec // tn
    i1_per_step = 8
    te = i1_per_step * N_KEYS
    n_exp = u_b.shape[0]
    tab_spec = pl.BlockSpec((PEER_HEADS, N_KEYS, tn), lambda i, j: (0, 0, i))
    tabb_spec = pl.BlockSpec((PEER_HEADS, N_KEYS // BF16_ROWS, BF16_ROWS, tn), lambda i, j: (0, 0, 0, i))
    return pl.pallas_call(
        functools.partial(_peer_dense_kernel, alpha, i1_per_step),
        grid=(n // tn, n_exp // te),
        in_specs=[pl.BlockSpec((d, tn), lambda i, j: (0, i)),
                  pl.BlockSpec((te, d), lambda i, j: (j, 0)),
                  pl.BlockSpec((d, te), lambda i, j: (0, j)),
                  tab_spec, tab_spec, tabb_spec, tabb_spec,
                  pl.BlockSpec((tn, d), lambda i, j: (i, 0)),
                  pl.BlockSpec((1,) + g2.shape[1:], lambda i, j: (i // tiles_per_vec, 0, 0)),
                  pl.BlockSpec(ln_g.shape, lambda i, j: (0, 0)),
                  pl.BlockSpec(ln_b.shape, lambda i, j: (0, 0))],
        out_specs=pl.BlockSpec((tn, d), lambda i, j: (i, 0)),
        out_shape=jax.ShapeDtypeStruct((n, d), F32),
        scratch_shapes=[pltpu.VMEM((d, tn), F32), pltpu.VMEM((te, tn), F32), pltpu.VMEM((te, tn), BF16)],
        compiler_params=_cparams("arbitrary", "arbitrary"),
        name="peer_dense_ln2",
    )(h2T, u_b, vT_b, *tabs, x1, g2, ln_g, ln_b)


def _pack_w_in(w):
    widths = (ATTN_WIDTH, KV_WIDTH, KV_WIDTH, N_IDX_HEADS * IDX_DIM, N_IDX_HEADS, IDX_DIM,
              D_CONV, D_CONV, D_CONV, w.shape[0], w.shape[0])
    parts, start = [], 0
    for wd in widths:
        parts.append(w[:, start:start + wd])
        start += wd
    q, k, v, qi, wi, ki, bg, cg, xin, ga, gb = parts
    pad = jnp.zeros((w.shape[0], _KIWI - IDX_DIM - N_IDX_HEADS), w.dtype)
    return jnp.concatenate([q, k, v, qi, ki, wi, pad, bg, cg, xin, ga, gb], axis=1).astype(BF16)


def kernel(x_prompt, x_sample, c_prompt, c_sample, cache_k, cache_v, cache_kidx, state_conv, page_table,
           rel_bias, w_ada, b_ada, w_in, conv_w, conv_b, w_o_attn, w_o_conv, w_out, ln1_g, ln1_b,
           ln2_g, ln2_b, peer_wq, peer_k1, peer_k2, peer_u, peer_v):
    depth = w_in.shape[0]
    alpha = (2 * depth) ** 0.25
    bp, sp, d = x_prompt.shape
    bs, ts, _ = x_sample.shape
    n_pool = cache_k.shape[1]
    past = page_table.shape[1] * PAGE_SIZE
    ns = bs * ts

    xp, xs = x_prompt, x_sample.reshape(ns, d)
    outs = [[] for _ in range(8)]
    for l in range(depth):
        c_all = jnp.concatenate([c_prompt, c_sample], axis=0)
        c_all = jnp.pad(c_all, ((0, (-c_all.shape[0]) % 16), (0, 0)))
        mod = _ada(c_all, w_ada[l], b_ada[l])[:bp + bs]
        sh1, sc1, g1, sh2, sc2, g2 = jnp.split(mod, 6, axis=-1)
        as_p = lambda a: a[:bp].reshape(bp, 1, d)
        as_s = lambda a: jnp.repeat(a[bp:], ts, axis=0).reshape(1, ns, d)

        w_all = _pack_w_in(w_in[l])
        woa, woc, wout = w_o_attn[l].astype(BF16), w_o_conv[l].astype(BF16), w_out[l].astype(BF16)
        wq = peer_wq[l].astype(BF16)
        k1, k2 = peer_k1[l].astype(BF16), peer_k2[l].astype(BF16)
        u_b = peer_u[l].astype(BF16)
        vT_b = peer_v[l].T.astype(BF16)
        lg1, lb1 = ln1_g[l].reshape(1, d), ln1_b[l].reshape(1, d)
        lg2, lb2 = ln2_g[l].reshape(1, d), ln2_b[l].reshape(1, d)

        (qT, k_p, v_p, kb_p, vTb_p, qiT, ki_p, kib_p, wiT, oconv_p, ga_p, gb_p, ulast_p) = _inproj_prompt(
            xp, as_p(sc1), as_p(sh1), w_all, conv_w[l], conv_b[l])
        oattn_p = _attn_prompt(rel_bias, qT, qiT, wiT, kb_p, vTb_p, kib_p)
        x1_p, h2T_p, pqT_p = _merge(alpha, xp.reshape(bp * sp, d), oattn_p.reshape(bp * sp, ATTN_WIDTH),
                                    oconv_p.reshape(bp * sp, D_CONV), ga_p.reshape(bp * sp, d),
                                    gb_p.reshape(bp * sp, d), as_p(g1), as_p(sc2), as_p(sh2),
                                    woa, woc, wout, wq, lg1, lb1, sp)
        tabs_p = _peer_select(pqT_p, k1, k2)
        y_p = _peer_dense(alpha, h2T_p, u_b, vT_b, tabs_p, x1_p, as_p(g2), lg2, lb2, sp)
        xp = y_p.reshape(bp, sp, d)

        st = state_conv[l]
        st0 = jnp.repeat(st[:, 0], ts, axis=0)
        st1 = jnp.repeat(st[:, 1], ts, axis=0)
        (q_s, k_s, v_s, qi_s, kiwi_s, oconv_s, ga_s, gb_s, u_s) = _inproj_sample(
            xs, as_s(sc1)[0], as_s(sh1)[0], w_all, conv_w[l], conv_b[l], st0, st1, ts)
        ki_s = kiwi_s[:, :IDX_DIM]
        wi_s = kiwi_s[:, IDX_DIM:IDX_DIM + N_IDX_HEADS]
        q5 = q_s.reshape(bs, ts, N_KV_HEADS, GROUP, HEAD_DIM).transpose(0, 2, 3, 1, 4)
        qpad = jnp.zeros((bs, N_KV_HEADS, GROUP * ts, N_KV_HEADS, HEAD_DIM), F32)
        for g in range(N_KV_HEADS):
            qpad = qpad.at[:, g, :, g, :].set(q5[:, g].reshape(bs, GROUP * ts, HEAD_DIM))
        qpad = qpad.reshape(bs, N_HEADS * ts, KV_WIDTH).astype(BF16)
        qi_r = qi_s.reshape(bs, ts, N_IDX_HEADS, IDX_DIM).transpose(0, 2, 1, 3).reshape(
            bs, N_IDX_HEADS * ts, IDX_DIM).astype(BF16)
        wi_r = jnp.broadcast_to(wi_s.reshape(bs, ts, N_IDX_HEADS).transpose(0, 2, 1).reshape(
            bs, N_IDX_HEADS * ts, 1), (bs, N_IDX_HEADS * ts, LANES))
        o_s = _attn_sample(page_table, rel_bias, qpad, qi_r, wi_r,
                           k_s.reshape(bs, ts, KV_WIDTH), v_s.reshape(bs, ts, KV_WIDTH),
                           ki_s.reshape(bs, ts, IDX_DIM),
                           cache_k[l].reshape(n_pool, PAGE_SIZE, KV_WIDTH),
                           cache_v[l].reshape(n_pool, PAGE_SIZE, KV_WIDTH), cache_kidx[l], past)
        o5 = o_s.reshape(bs, N_KV_HEADS, GROUP, ts, N_KV_HEADS, HEAD_DIM)
        oattn_s = jnp.stack([o5[:, g, :, :, g, :] for g in range(N_KV_HEADS)], axis=1)
        oattn_s = oattn_s.transpose(0, 3, 1, 2, 4).reshape(ns, ATTN_WIDTH).astype(BF16)
        x1_s, h2T_s, pqT_s = _merge(alpha, xs, oattn_s, oconv_s, ga_s, gb_s, as_s(g1), as_s(sc2), as_s(sh2),
                                    woa, woc, wout, wq, lg1, lb1, ns)
        tabs_s = _peer_select(pqT_s, k1, k2)
        xs = _peer_dense(alpha, h2T_s, u_b, vT_b, tabs_s, x1_s, as_s(g2), lg2, lb2, ns)

        for lst, val in zip(outs, (
                k_p.reshape(bp, sp, N_KV_HEADS, HEAD_DIM), v_p.reshape(bp, sp, N_KV_HEADS, HEAD_DIM), ki_p,
                ulast_p[:, 8 - (CONV_WIDTH - 1):],
                k_s.reshape(bs, ts, N_KV_HEADS, HEAD_DIM), v_s.reshape(bs, ts, N_KV_HEADS, HEAD_DIM),
                ki_s.reshape(bs, ts, IDX_DIM),
                u_s.reshape(bs, ts, D_CONV)[:, ts - (CONV_WIDTH - 1):])):
            lst.append(val)

    return (xp, xs.reshape(bs, ts, d)) + tuple(jnp.stack(o) for o in outs)
```

```python
import functools
import math

import jax
import jax.numpy as jnp
from jax import lax
from jax.experimental import pallas as pl
from jax.experimental.pallas import tpu as pltpu

F32 = jnp.float32
BF16 = jnp.bfloat16
I32 = jnp.int32

N_HEADS = 8
N_KV_HEADS = 2
HEAD_DIM = 64
GROUP = N_HEADS // N_KV_HEADS
ATTN_WIDTH = N_HEADS * HEAD_DIM
KV_WIDTH = N_KV_HEADS * HEAD_DIM
ATTN_SCALE = HEAD_DIM ** -0.5
N_IDX_HEADS = 4
IDX_DIM = 64
IDX_SCALE = (IDX_DIM * N_IDX_HEADS) ** -0.5
TOPK_MAX = 256
N_BUCKETS = 32
MAX_DISTANCE = 128
D_CONV = 512
CONV_WIDTH = 3
N_KEYS = 128
PEER_HEADS = 8
PEER_HALF = 64
PEER_TOPK = 16
LN_EPS = 1e-5
PAGE_SIZE = 128

LANES = 128
BF16_ROWS = 16
VMEM_LIMIT = 56 * 1024 * 1024

_GELU_C1 = math.sqrt(2.0 / math.pi)
_GELU_C2 = _GELU_C1 * 0.044715

NEG_BIG = -1e30

_KIWI = LANES
_OFF_Q = 0
_OFF_K = _OFF_Q + ATTN_WIDTH
_OFF_V = _OFF_K + KV_WIDTH
_OFF_QI = _OFF_V + KV_WIDTH
_OFF_KIWI = _OFF_QI + N_IDX_HEADS * IDX_DIM
_OFF_BG = _OFF_KIWI + _KIWI
_OFF_CG = _OFF_BG + D_CONV
_OFF_XIN = _OFF_CG + D_CONV
_OFF_GA = _OFF_XIN + D_CONV


def _cparams(*sem):
    return pltpu.CompilerParams(dimension_semantics=sem, vmem_limit_bytes=VMEM_LIMIT)


def _dot(a, b):
    return jnp.dot(a, b, preferred_element_type=F32)


def _dot_nt(a, b):
    return lax.dot_general(a, b, (((1,), (1,)), ((), ())), preferred_element_type=F32)


def _kth_largest(stats, k, n_valid, v_min, v_max):
    enough = n_valid >= k
    lo0 = jnp.where(enough, v_min, -jnp.inf)
    hi0 = jnp.where(enough, v_max, -jnp.inf)

    def cond(state):
        lo, hi = state
        return jnp.max(jnp.where(lo < hi, 1.0, 0.0)) > 0.0

    def step(state):
        lo, hi = state
        mid = lo + (hi - lo) * 0.5
        mid = jnp.where(mid > lo, mid, hi)
        cnt, above, below = stats(mid)
        active = lo < hi
        return (jnp.where(active & (cnt >= k), above, lo), jnp.where(active & (cnt < k), below, hi))

    lo, _ = lax.while_loop(cond, lambda state: step(step(state)), (lo0, hi0))
    return lo


def _t5_bias(dist, relb_ref, h):
    n = jnp.maximum(dist, 0)
    max_exact = N_BUCKETS // 2
    nf = jnp.maximum(n, 1).astype(F32)
    large = max_exact + jnp.floor(jnp.log(nf / max_exact) / math.log(MAX_DISTANCE / max_exact)
                                  * (N_BUCKETS - max_exact)).astype(I32)
    large = jnp.minimum(large, N_BUCKETS - 1)
    bucket = jnp.where(n < max_exact, n, large)
    out = jnp.zeros(dist.shape, F32)
    for b in range(N_BUCKETS):
        out = jnp.where(bucket == b, relb_ref[b, h], out)
    return out


def _layer_norm(x, g, b):
    mu = jnp.mean(x, axis=-1, keepdims=True)
    var = jnp.mean(jnp.square(x - mu), axis=-1, keepdims=True)
    return (x - mu) * lax.rsqrt(var + LN_EPS) * g + b


def _ada_kernel(c_ref, w_ref, b_ref, o_ref):
    o_ref[...] = _dot(c_ref[...].astype(BF16), w_ref[...].astype(BF16)) + b_ref[...]


def _ada(c, w, b):
    m, d = c.shape
    n = w.shape[1]
    tn = n // 4
    return pl.pallas_call(
        _ada_kernel,
        grid=(n // tn,),
        in_specs=[pl.BlockSpec((m, d), lambda j: (0, 0)),
                  pl.BlockSpec((d, tn), lambda j: (0, j)),
                  pl.BlockSpec((1, tn), lambda j: (0, j))],
        out_specs=pl.BlockSpec((m, tn), lambda j: (0, j)),
        out_shape=jax.ShapeDtypeStruct((m, n), F32),
        compiler_params=_cparams("arbitrary"),
        name="ada_mod",
    )(c, w, b.reshape(1, n))


def _conv_out(bg, u, u1, u2, cw_ref, cb_ref):
    y = cb_ref[...] + u2 * cw_ref[0:1, :] + u1 * cw_ref[1:2, :] + u * cw_ref[2:3, :]
    return bg * y


def _inproj_prompt_kernel(x_ref, sc_ref, sh_ref, w_ref, cw_ref, cb_ref,
                          qT_ref, k_ref, v_ref, kb_ref, vTb_ref, qiT_ref, ki_ref, kib_ref, wiT_ref,
                          oconv_ref, ga_ref, gb_ref, ulast_ref, uprev_scr):
    s = pl.program_id(1)
    tm = x_ref.shape[1]
    h = (x_ref[0] * (1.0 + sc_ref[0]) + sh_ref[0]).astype(BF16)

    def proj(a, width):
        return _dot(h, w_ref[:, a:a + width])

    qT_ref[0] = proj(_OFF_Q, ATTN_WIDTH).T.astype(BF16)
    k = proj(_OFF_K, KV_WIDTH)
    k_ref[0] = k
    kb_ref[0] = k.astype(BF16)
    v = proj(_OFF_V, KV_WIDTH)
    v_ref[0] = v
    for i in range(tm // LANES):
        vTb_ref[0, i] = v[i * LANES:(i + 1) * LANES].T.astype(BF16)
    qiT_ref[0] = proj(_OFF_QI, N_IDX_HEADS * IDX_DIM).T.astype(BF16)
    kiwi = proj(_OFF_KIWI, _KIWI)
    ki_ref[0] = kiwi[:, :IDX_DIM]
    kib_ref[0] = kiwi[:, :IDX_DIM].astype(BF16)
    wiT_ref[0] = kiwi.T[IDX_DIM:IDX_DIM + 8]

    @pl.when(s == 0)
    def _():
        uprev_scr[...] = jnp.zeros(uprev_scr.shape, F32)

    bg = proj(_OFF_BG, D_CONV)
    u = proj(_OFF_CG, D_CONV) * proj(_OFF_XIN, D_CONV)
    row = lax.broadcasted_iota(I32, u.shape, 0)
    up = uprev_scr[...]
    u1 = jnp.where(row < 1, pltpu.roll(up, 1, 0), pltpu.roll(u, 1, 0))
    u2 = jnp.where(row < 2, pltpu.roll(up, 2, 0), pltpu.roll(u, 2, 0))
    oconv_ref[0] = _conv_out(bg, u, u1, u2, cw_ref, cb_ref).astype(BF16)
    uprev_scr[...] = u
    ulast_ref[0] = u[tm - 8:, :]

    ga_ref[0] = proj(_OFF_GA, w_ref.shape[0])
    gb_ref[0] = proj(_OFF_GA + w_ref.shape[0], w_ref.shape[0])


def _inproj_prompt(x, sc, sh, w_all, conv_w, conv_b):
    b, s, d = x.shape
    tm = min(256, s)
    nkb = tm // LANES
    row = lambda width, dt: jax.ShapeDtypeStruct((b, s, width), dt)
    out_shape = (
        jax.ShapeDtypeStruct((b, ATTN_WIDTH, s), BF16),
        row(KV_WIDTH, F32), row(KV_WIDTH, F32), row(KV_WIDTH, BF16),
        jax.ShapeDtypeStruct((b, s // LANES, KV_WIDTH, LANES), BF16),
        jax.ShapeDtypeStruct((b, N_IDX_HEADS * IDX_DIM, s), BF16),
        row(IDX_DIM, F32), row(IDX_DIM, BF16),
        jax.ShapeDtypeStruct((b, 8, s), F32),
        row(D_CONV, BF16),
        row(d, F32), row(d, F32),
        jax.ShapeDtypeStruct((b, 8, D_CONV), F32),
    )
    tile = lambda width: pl.BlockSpec((1, tm, width), lambda i, j: (i, j, 0))
    tileT = lambda rows: pl.BlockSpec((1, rows, tm), lambda i, j: (i, 0, j))
    out_specs = (
        tileT(ATTN_WIDTH), tile(KV_WIDTH), tile(KV_WIDTH), tile(KV_WIDTH),
        pl.BlockSpec((1, nkb, KV_WIDTH, LANES), lambda i, j: (i, j, 0, 0)),
        tileT(N_IDX_HEADS * IDX_DIM), tile(IDX_DIM), tile(IDX_DIM), tileT(8),
        tile(D_CONV), tile(d), tile(d),
        pl.BlockSpec((1, 8, D_CONV), lambda i, j: (i, 0, 0)),
    )
    vec = pl.BlockSpec((1, 1, d), lambda i, j: (i, 0, 0))
    return pl.pallas_call(
        _inproj_prompt_kernel,
        grid=(b, s // tm),
        in_specs=[tile(d), vec, vec,
                  pl.BlockSpec(w_all.shape, lambda i, j: (0, 0)),
                  pl.BlockSpec(conv_w.shape, lambda i, j: (0, 0)),
                  pl.BlockSpec((1, D_CONV), lambda i, j: (0, 0))],
        out_specs=out_specs,
        out_shape=out_shape,
        scratch_shapes=[pltpu.VMEM((tm, D_CONV), F32)],
        compiler_params=_cparams("arbitrary", "arbitrary"),
        name="inproj_prompt",
    )(x, sc, sh, w_all, conv_w, conv_b.reshape(1, D_CONV))


def _inproj_sample_kernel(t_seq, x_ref, sc_ref, sh_ref, w_ref, cw_ref, cb_ref, s0_ref, s1_ref,
                          q_ref, k_ref, v_ref, qi_ref, kiwi_ref, oconv_ref, ga_ref, gb_ref, u_ref):
    h = (x_ref[...] * (1.0 + sc_ref[...]) + sh_ref[...]).astype(BF16)

    def proj(a, width):
        return _dot(h, w_ref[:, a:a + width])

    q_ref[...] = proj(_OFF_Q, ATTN_WIDTH)
    k_ref[...] = proj(_OFF_K, KV_WIDTH)
    v_ref[...] = proj(_OFF_V, KV_WIDTH)
    qi_ref[...] = proj(_OFF_QI, N_IDX_HEADS * IDX_DIM)
    kiwi_ref[...] = proj(_OFF_KIWI, _KIWI)
    bg = proj(_OFF_BG, D_CONV)
    u = proj(_OFF_CG, D_CONV) * proj(_OFF_XIN, D_CONV)
    t = lax.broadcasted_iota(I32, u.shape, 0) % t_seq
    u1 = jnp.where(t == 0, s1_ref[...], pltpu.roll(u, 1, 0))
    u2 = jnp.where(t == 0, s0_ref[...], jnp.where(t == 1, s1_ref[...], pltpu.roll(u, 2, 0)))
    oconv_ref[...] = _conv_out(bg, u, u1, u2, cw_ref, cb_ref).astype(BF16)
    u_ref[...] = u
    ga_ref[...] = proj(_OFF_GA, w_ref.shape[0])
    gb_ref[...] = proj(_OFF_GA + w_ref.shape[0], w_ref.shape[0])


def _inproj_sample(x, sc_rows, sh_rows, w_all, conv_w, conv_b, st0_rows, st1_rows, t_seq):
    n, d = x.shape
    full = lambda a: pl.BlockSpec(a.shape, lambda i: (0,) * a.ndim)
    o = lambda width, dt: jax.ShapeDtypeStruct((n, width), dt)
    out_shape = (o(ATTN_WIDTH, F32), o(KV_WIDTH, F32), o(KV_WIDTH, F32), o(N_IDX_HEADS * IDX_DIM, F32),
                 o(_KIWI, F32), o(D_CONV, BF16), o(d, F32), o(d, F32), o(D_CONV, F32))
    args = (x, sc_rows, sh_rows, w_all, conv_w, conv_b.reshape(1, D_CONV), st0_rows, st1_rows)
    return pl.pallas_call(
        functools.partial(_inproj_sample_kernel, t_seq),
        grid=(1,),
        in_specs=[full(a) for a in args],
        out_specs=tuple(pl.BlockSpec(s.shape, lambda i: (0, 0)) for s in out_shape),
        out_shape=out_shape,
        compiler_params=_cparams("arbitrary"),
        name="inproj_sample",
    )(*args)


def _attn_prompt_kernel(n_sel, relb_ref, qT_ref, qiT_ref, wiT_ref, kb_ref, vTb_ref, kib_ref,
                        o_ref, bias_scr, key_scr, sel_scr):
    qb = pl.program_id(1)
    blk = LANES
    kblk = 2 * blk
    ntrip = qb // 2 + 1

    @pl.when((pl.program_id(0) == 0) & (qb == 0))
    def _():
        j = lax.broadcasted_iota(I32, (blk, blk), 0)
        i = lax.broadcasted_iota(I32, (blk, blk), 1)
        for delta in range(3):
            dist = i - j + blk * delta
            for h in range(N_HEADS):
                g, r = divmod(h, GROUP)
                bias_scr[delta, g, :, r * blk:(r + 1) * blk] = _t5_bias(dist, relb_ref, h)

    row = lax.broadcasted_iota(I32, (kblk, blk), 0)
    lane = lax.broadcasted_iota(I32, (kblk, blk), 1)

    def valid_mask(t):
        return (t * kblk + row) <= (qb * blk + lane)

    def key_rows(t):
        return pl.ds(pl.multiple_of(t * kblk, kblk), kblk)

    qiT = jnp.concatenate([qiT_ref[0, h * IDX_DIM:(h + 1) * IDX_DIM, :] for h in range(N_IDX_HEADS)], axis=1)
    wi = [wiT_ref[0, h:h + 1, :] for h in range(N_IDX_HEADS)]
    fold = lambda a: a.reshape(kblk // 8, 8, blk)
    tall = lambda a: jnp.sum(a, axis=0, keepdims=True)
    part0 = (jnp.zeros((8, blk), F32), jnp.full((8, blk), jnp.inf, F32), jnp.full((8, blk), -jnp.inf, F32))

    def score_blk(t, carry):
        cnt, lo, hi = carry
        d = _dot(kib_ref[0, key_rows(t), :], qiT)
        s = jnp.maximum(d[:, 0:blk], 0.0) * wi[0]
        for h in range(1, N_IDX_HEADS):
            s = s + jnp.maximum(d[:, h * blk:(h + 1) * blk], 0.0) * wi[h]
        s = s * IDX_SCALE
        valid = valid_mask(t)
        key_scr[key_rows(t), :] = jnp.where(valid, s, -jnp.inf)
        return (cnt + jnp.sum(fold(jnp.where(valid, 1.0, 0.0)), axis=0),
                jnp.minimum(lo, jnp.min(fold(jnp.where(valid, s, jnp.inf)), axis=0)),
                jnp.maximum(hi, jnp.max(fold(jnp.where(valid, s, -jnp.inf)), axis=0)))

    cnt, lo, hi = lax.fori_loop(0, ntrip, score_blk, part0)
    n_valid = tall(cnt)
    v_min = jnp.min(lo, axis=0, keepdims=True)
    v_max = jnp.max(hi, axis=0, keepdims=True)

    def stats(mid):
        def body(t, carry):
            cnt, above, below = carry
            x = key_scr[key_rows(t), :]
            ge = x >= mid
            return (cnt + jnp.sum(fold(jnp.where(ge, 1.0, 0.0)), axis=0),
                    jnp.minimum(above, jnp.min(fold(jnp.where(ge, x, jnp.inf)), axis=0)),
                    jnp.maximum(below, jnp.max(fold(jnp.where(ge, -jnp.inf, x)), axis=0)))
        cnt, above, below = lax.fori_loop(0, ntrip, body, part0)
        return tall(cnt), jnp.min(above, axis=0, keepdims=True), jnp.max(below, axis=0, keepdims=True)

    thr = _kth_largest(stats, float(n_sel), n_valid, v_min, v_max)

    def count_gt(t, acc):
        return acc + jnp.sum(fold(jnp.where(key_scr[key_rows(t), :] > thr, 1.0, 0.0)), axis=0)

    need = n_sel - tall(lax.fori_loop(0, ntrip, count_gt, jnp.zeros((8, blk), F32)))
    ltri = (lax.broadcasted_iota(I32, (kblk, kblk), 1) <= lax.broadcasted_iota(I32, (kblk, kblk), 0)).astype(BF16)

    def sel_blk(t, carry):
        kv = key_scr[key_rows(t), :]
        eq = kv == thr
        prefix = _dot(ltri, eq.astype(BF16)) + carry
        sel = ((kv > thr) | (eq & (prefix <= need))) & valid_mask(t)
        sel_scr[key_rows(t), :] = jnp.where(sel, 0.0, NEG_BIG)
        return prefix[kblk - 1:kblk, :]

    lax.fori_loop(0, ntrip, sel_blk, jnp.zeros((1, blk), F32))

    zero = jnp.zeros((HEAD_DIM, blk), BF16)
    scale = jnp.asarray(ATTN_SCALE, BF16)
    qTg = []
    for g in range(N_KV_HEADS):
        cols = []
        for r in range(GROUP):
            base = g * GROUP * HEAD_DIM + r * HEAD_DIM
            parts = [zero] * N_KV_HEADS
            parts[g] = qT_ref[0, base:base + HEAD_DIM, :] * scale
            cols.append(jnp.concatenate(parts, axis=0))
        qTg.append(jnp.concatenate(cols, axis=1))
    width = GROUP * blk

    def att_blk(t, carry):
        kmat = kb_ref[0, key_rows(t), :]
        mask = jnp.concatenate([sel_scr[key_rows(t), :]] * GROUP, axis=1)
        d0 = jnp.clip(qb - 2 * t, 0, 2)
        d1 = jnp.clip(qb - 2 * t - 1, 0, 2)
        new = []
        for g in range(N_KV_HEADS):
            m, l, acc = carry[g]
            bias = jnp.concatenate([bias_scr[d0, g], bias_scr[d1, g]], axis=0)
            lg = _dot(kmat, qTg[g]) + bias + mask
            m_new = jnp.maximum(m, jnp.max(lg, axis=0, keepdims=True))
            p = jnp.exp(lg - m_new)
            alpha = jnp.exp(m - m_new)
            l = alpha * l + jnp.sum(p, axis=0, keepdims=True)
            rows = slice(g * HEAD_DIM, (g + 1) * HEAD_DIM)
            vt = jnp.concatenate([vTb_ref[0, 2 * t, rows, :], vTb_ref[0, 2 * t + 1, rows, :]], axis=1)
            acc = alpha * acc + _dot(vt, p.astype(BF16))
            new.append((m_new, l, acc))
        return tuple(new)

    init = (jnp.full((1, width), NEG_BIG, F32), jnp.zeros((1, width), F32), jnp.zeros((HEAD_DIM, width), F32))
    res = lax.fori_loop(0, ntrip, att_blk, (init,) * N_KV_HEADS)
    outs = []
    for g in range(N_KV_HEADS):
        _, l, acc = res[g]
        og = acc / l
        outs += [og[:, r * blk:(r + 1) * blk] for r in range(GROUP)]
    o_ref[0] = jnp.concatenate(outs, axis=0).T.astype(BF16)


def _attn_prompt(rel_bias, qT, qiT, wiT, kb, vTb, kib):
    b, _, s = qT.shape
    n_sel = min(TOPK_MAX, s // 4)
    blk = LANES
    per_b = lambda a: pl.BlockSpec((1,) + a.shape[1:], lambda i, j: (i,) + (0,) * (a.ndim - 1))
    qtile = lambda rows: pl.BlockSpec((1, rows, blk), lambda i, j: (i, 0, j))
    return pl.pallas_call(
        functools.partial(_attn_prompt_kernel, n_sel),
        grid=(b, s // blk),
        in_specs=[pl.BlockSpec(memory_space=pltpu.SMEM),
                  qtile(ATTN_WIDTH), qtile(N_IDX_HEADS * IDX_DIM), qtile(8),
                  per_b(kb), per_b(vTb), per_b(kib)],
        out_specs=pl.BlockSpec((1, blk, ATTN_WIDTH), lambda i, j: (i, j, 0)),
        out_shape=jax.ShapeDtypeStruct((b, s, ATTN_WIDTH), BF16),
        scratch_shapes=[pltpu.VMEM((3, N_KV_HEADS, blk, GROUP * blk), F32),
                        pltpu.VMEM((s, blk), F32),
                        pltpu.VMEM((s, blk), F32)],
        compiler_params=_cparams("arbitrary", "arbitrary"),
        name="attn_prompt",
    )(rel_bias, qT, qiT, wiT, kb, vTb, kib)


def _attn_sample_kernel(n_sel, past, t_seq, pt_ref, relb_ref, qpad_ref, qi_ref, wi_ref,
                        knew_ref, vnew_ref, kinew_ref, ck_hbm, cv_hbm, cki_hbm,
                        o_ref, kbuf, vbuf, kibuf, sem):
    b = pl.program_id(0)
    nb = pl.num_programs(0)
    n_pages = past // PAGE_SIZE
    lp = kbuf.shape[1]
    blk = LANES
    nblk = lp // blk
    slot = b % 2

    def page_copies(seq, sl, p):
        phys = pt_ref[seq, p]
        rows = pl.ds(pl.multiple_of(p * PAGE_SIZE, PAGE_SIZE), PAGE_SIZE)
        return (pltpu.make_async_copy(ck_hbm.at[phys], kbuf.at[sl, rows, :], sem.at[sl, 0]),
                pltpu.make_async_copy(cv_hbm.at[phys], vbuf.at[sl, rows, :], sem.at[sl, 1]),
                pltpu.make_async_copy(cki_hbm.at[phys], kibuf.at[sl, rows, :], sem.at[sl, 2]))

    def start_all(seq, sl):
        def body(p, c):
            for cp in page_copies(seq, sl, p):
                cp.start()
            return c
        lax.fori_loop(0, n_pages, body, 0)

    def wait_all(seq, sl):
        def body(p, c):
            for cp in page_copies(seq, sl, p):
                cp.wait()
            return c
        lax.fori_loop(0, n_pages, body, 0)

    @pl.when(b == 0)
    def _():
        for sl in range(2):
            kbuf[sl, past:, :] = jnp.zeros((blk, KV_WIDTH), F32)
            vbuf[sl, past:, :] = jnp.zeros((blk, KV_WIDTH), F32)
            kibuf[sl, past:, :] = jnp.zeros((blk, IDX_DIM), F32)
        start_all(0, 0)

    @pl.when(b + 1 < nb)
    def _():
        start_all(b + 1, 1 - slot)

    kbuf[slot, past:past + t_seq, :] = knew_ref[0]
    vbuf[slot, past:past + t_seq, :] = vnew_ref[0]
    kibuf[slot, past:past + t_seq, :] = kinew_ref[0]
    wait_all(b, slot)

    d = _dot_nt(qi_ref[0], kibuf[slot].astype(BF16))
    s = jnp.maximum(d[0:t_seq], 0.0) * wi_ref[0, 0:t_seq, 0:1]
    for h in range(1, N_IDX_HEADS):
        s = s + jnp.maximum(d[h * t_seq:(h + 1) * t_seq], 0.0) * wi_ref[0, h * t_seq:(h + 1) * t_seq, 0:1]
    kpos = lax.broadcasted_iota(I32, (t_seq, lp), 1)
    qpos = past + lax.broadcasted_iota(I32, (t_seq, lp), 0)
    valid = kpos <= qpos
    s = s * IDX_SCALE
    key = jnp.where(valid, s, -jnp.inf)

    def count(pred):
        return jnp.sum(jnp.where(pred, 1.0, 0.0), axis=1, keepdims=True)

    def stats(mid):
        ge = key >= mid
        return (count(ge), jnp.min(jnp.where(ge, key, jnp.inf), axis=1, keepdims=True),
                jnp.max(jnp.where(ge, -jnp.inf, key), axis=1, keepdims=True))

    thr = _kth_largest(stats, float(n_sel), count(valid),
                       jnp.min(jnp.where(valid, s, jnp.inf), axis=1, keepdims=True),
                       jnp.max(key, axis=1, keepdims=True))
    need = n_sel - count(key > thr)
    eq = key == thr
    eqf = jnp.where(eq, 1.0, 0.0)
    utri = (lax.broadcasted_iota(I32, (blk, blk), 0) <= lax.broadcasted_iota(I32, (blk, blk), 1)).astype(BF16)
    carry = jnp.zeros((t_seq, 1), F32)
    ranks = []
    for kb in range(nblk):
        e = eqf[:, kb * blk:(kb + 1) * blk]
        ranks.append(_dot(e.astype(BF16), utri) + carry)
        carry = carry + jnp.sum(e, axis=1, keepdims=True)
    prefix = jnp.concatenate(ranks, axis=1)
    sel = jnp.where(((key > thr) | (eq & (prefix <= need))) & valid, 1.0, 0.0)

    lg = _dot_nt(qpad_ref[0], kbuf[slot].astype(BF16)) * ATTN_SCALE
    far = lp - 2 * blk
    near_dist = (past + lax.broadcasted_iota(I32, (t_seq, 2 * blk), 0)
                 - (far + lax.broadcasted_iota(I32, (t_seq, 2 * blk), 1)))
    bias_rows = []
    for h in range(N_HEADS):
        far_bias = jnp.full((t_seq, far), relb_ref[N_BUCKETS - 1, h], F32)
        bias_rows.append(jnp.concatenate([far_bias, _t5_bias(near_dist, relb_ref, h)], axis=1))
    lg = lg + jnp.concatenate(bias_rows, axis=0)
    selh = jnp.concatenate([sel] * N_HEADS, axis=0) > 0.0
    m = jnp.max(jnp.where(selh, lg, NEG_BIG), axis=1, keepdims=True)
    p = jnp.where(selh, jnp.exp(lg - m), 0.0)
    l = jnp.sum(p, axis=1, keepdims=True)
    o = _dot(p.astype(BF16), vbuf[slot].astype(BF16))
    o_ref[0] = o / l


def _attn_sample(page_table, rel_bias, qpad, qi_r, wi_r, k_new, v_new, ki_new, cache_k, cache_v, cache_ki,
                 past):
    nb, t_seq, _ = k_new.shape
    n_sel = min(TOPK_MAX, (past + t_seq) // 4)
    lp = past + LANES
    rows = N_HEADS * t_seq
    per_b = lambda a: pl.BlockSpec((1,) + a.shape[1:], lambda i, pt: (i,) + (0,) * (a.ndim - 1))
    hbm = pl.BlockSpec(memory_space=pl.ANY)
    grid_spec = pltpu.PrefetchScalarGridSpec(
        num_scalar_prefetch=1,
        grid=(nb,),
        in_specs=[pl.BlockSpec(memory_space=pltpu.SMEM),
                  per_b(qpad), per_b(qi_r), per_b(wi_r), per_b(k_new), per_b(v_new), per_b(ki_new),
                  hbm, hbm, hbm],
        out_specs=pl.BlockSpec((1, rows, KV_WIDTH), lambda i, pt: (i, 0, 0)),
        scratch_shapes=[pltpu.VMEM((2, lp, KV_WIDTH), F32),
                        pltpu.VMEM((2, lp, KV_WIDTH), F32),
                        pltpu.VMEM((2, lp, IDX_DIM), F32),
                        pltpu.SemaphoreType.DMA((2, 3))],
    )
    return pl.pallas_call(
        functools.partial(_attn_sample_kernel, n_sel, past, t_seq),
        grid_spec=grid_spec,
        out_shape=jax.ShapeDtypeStruct((nb, rows, KV_WIDTH), F32),
        compiler_params=_cparams("arbitrary"),
        name="attn_sample",
    )(page_table, rel_bias, qpad, qi_r, wi_r, k_new, v_new, ki_new, cache_k, cache_v, cache_ki)


def _merge_kernel(alpha, x_ref, oa_ref, oc_ref, ga_ref, gb_ref, g1_ref, sc2_ref, sh2_ref,
                  woa_ref, woc_ref, wout_ref, wq_ref, lng_ref, lnb_ref,
                  x1_ref, h2T_ref, pqT_ref):
    ta = _dot(oa_ref[...], woa_ref[...])
    tc = _dot(oc_ref[...], woc_ref[...])
    merged = jax.nn.sigmoid(ga_ref[...]) * ta + jax.nn.sigmoid(gb_ref[...]) * tc
    out = _dot(merged.astype(BF16), wout_ref[...])
    x1 = _layer_norm(alpha * x_ref[...] + g1_ref[0] * out, lng_ref[...], lnb_ref[...])
    x1_ref[...] = x1
    h2 = x1 * (1.0 + sc2_ref[0]) + sh2_ref[0]
    h2b = h2.astype(BF16)
    h2T_ref[...] = h2.T.astype(BF16)
    pqT_ref[...] = _dot(h2b, wq_ref[...]).T.astype(BF16)


def _merge(alpha, x, oattn, oconv, ga, gb, g1, sc2, sh2, woa, woc, wout, wq, ln_g, ln_b, rows_per_vec):
    n, d = x.shape
    tm = min(512, rows_per_vec) if g1.shape[1] == 1 else g1.shape[1]
    tiles_per_vec = rows_per_vec // tm
    tile = lambda width: pl.BlockSpec((tm, width), lambda i: (i, 0))
    vec = pl.BlockSpec((1,) + g1.shape[1:], lambda i: (i // tiles_per_vec, 0, 0))
    full = lambda a: pl.BlockSpec(a.shape, lambda i: (0,) * a.ndim)
    pq_w = wq.shape[1]
    return pl.pallas_call(
        functools.partial(_merge_kernel, alpha),
        grid=(n // tm,),
        in_specs=[tile(d), tile(ATTN_WIDTH), tile(D_CONV), tile(d), tile(d), vec, vec, vec,
                  full(woa), full(woc), full(wout), full(wq), full(ln_g), full(ln_b)],
        out_specs=(tile(d), pl.BlockSpec((d, tm), lambda i: (0, i)), pl.BlockSpec((pq_w, tm), lambda i: (0, i))),
        out_shape=(jax.ShapeDtypeStruct((n, d), F32), jax.ShapeDtypeStruct((d, n), BF16),
                   jax.ShapeDtypeStruct((pq_w, n), BF16)),
        compiler_params=_cparams("arbitrary"),
        name="merge_ln1_peerq",
    )(x, oattn, oconv, ga, gb, g1, sc2, sh2, woa, woc, wout, wq, ln_g, ln_b)


def _top16(s, v_scr):
    iota = lax.broadcasted_iota(I32, s.shape, 0)

    def body(r, carry):
        x, rank = carry
        m = jnp.max(x, axis=0, keepdims=True)
        idx = jnp.min(jnp.where(x == m, iota, N_KEYS), axis=0, keepdims=True)
        hit = iota == idx
        v_scr[pl.ds(r, 1), :] = m
        return jnp.where(hit, -jnp.inf, x), jnp.where(hit, lax.convert_element_type(r, F32), rank)

    _, rank = lax.fori_loop(0, PEER_TOPK, body, (s, jnp.full(s.shape, float(PEER_TOPK), F32)))
    return rank


def _top16_distinct(xs, v_scrs, rank_scrs):
    for rs in rank_scrs:
        rs[...] = jnp.full(rs.shape, float(PEER_TOPK), F32)

    def body(r, xs):
        out = []
        for x, vs, rs in zip(xs, v_scrs, rank_scrs):
            m = jnp.max(x, axis=0, keepdims=True)
            hit = x == m
            vs[pl.ds(r, 1), :] = m
            rs[...] = jnp.where(hit, lax.convert_element_type(r, F32), rs[...])
            out.append(jnp.where(hit, -jnp.inf, x))
        return tuple(out)

    lax.fori_loop(0, PEER_TOPK, body, tuple(xs))


def _pair_candidates(v1, v2):
    return jnp.concatenate([v1[0:1] + v2] + [v1[a:a + 1] + v2[0:8] for a in range(1, 8)]
                           + [v1[8:16] + v2[0:1]], axis=0)


def _peer_select_kernel(pqT_ref, k1_ref, k2_ref, cnt1_ref, w1_ref, rank2_ref, e2_ref,
                        v1_scr, v2_scr, r1_scr, r2_scr):
    tn = pqT_ref.shape[1]
    k = PEER_TOPK
    tiles = (N_KEYS // BF16_ROWS, BF16_ROWS, tn)

    def emit(h, s1, s2, rank1, rank2, cnt_rows, z):
        cnt1 = jnp.zeros(s1.shape, F32)
        for a in range(k):
            cnt1 = jnp.where(rank1 == float(a), cnt_rows[a], cnt1)
        cnt1_ref[h] = cnt1
        w1_ref[h] = jnp.exp(s1 - v1_scr[0:1, :]) * (0.5 / z)
        rank2_ref[h] = rank2.astype(BF16).reshape(tiles)
        e2_ref[h] = jnp.exp(s2 - v2_scr[0:1, :]).astype(BF16).reshape(tiles)

    def head(h, c):
        base = pl.multiple_of(h * 2 * PEER_HALF, 2 * PEER_HALF)
        s1 = _dot(k1_ref[...], pqT_ref[pl.ds(base, PEER_HALF), :])
        s2 = _dot(k2_ref[...], pqT_ref[pl.ds(base + PEER_HALF, PEER_HALF), :])

        _top16_distinct((s1, s2), (v1_scr, v2_scr), (r1_scr, r2_scr))
        v1, v2 = v1_scr[...], v2_scr[...]
        rank1, rank2 = r1_scr[...], r2_scr[...]
        top = v1[0:1] + v2[0:1]

        def pick(r, carry):
            x, z = carry
            m = jnp.max(x, axis=0, keepdims=True)
            return jnp.where(x == m, -jnp.inf, x), z + jnp.exp(m - top)

        x, z = lax.fori_loop(0, k, pick, (_pair_candidates(v1, v2), jnp.zeros((1, tn), F32)))
        taken = jnp.where(x == -jnp.inf, 1.0, 0.0)
        cnt_rows = ([jnp.sum(taken[0:k], axis=0, keepdims=True)]
                    + [jnp.sum(taken[k + 8 * (a - 1):k + 8 * a], axis=0, keepdims=True) for a in range(1, 8)]
                    + [taken[k + 56 + a:k + 57 + a] for a in range(8)])
        n_pairs = cnt_rows[0]
        for row in cnt_rows[1:]:
            n_pairs = n_pairs + row
        n1 = jnp.sum(jnp.where(rank1 < float(k), 1.0, 0.0), axis=0, keepdims=True)
        n2 = jnp.sum(jnp.where(rank2 < float(k), 1.0, 0.0), axis=0, keepdims=True)
        tied = (n1 != float(k)) | (n2 != float(k)) | (n_pairs != float(k))
        emit(h, s1, s2, rank1, rank2, cnt_rows, z)

        @pl.when(jnp.sum(jnp.where(tied, 1.0, 0.0)) > 0.0)
        def _():
            rank1 = _top16(s1, v1_scr)
            rank2 = _top16(s2, v2_scr)
            v1, v2 = v1_scr[...], v2_scr[...]
            top = v1[0:1] + v2[0:1]
            r16 = lax.broadcasted_iota(I32, (k, tn), 0)
            r8 = lax.broadcasted_iota(I32, (8, tn), 0)
            flat = jnp.concatenate([r16] + [a * k + r8 for a in range(1, 8)] + [(8 + r8) * k], axis=0)

            def pick_exact(r, carry):
                x, cnt, z = carry
                m = jnp.max(x, axis=0, keepdims=True)
                idx = jnp.min(jnp.where(x == m, flat, k * k), axis=0, keepdims=True)
                cnt = cnt + jnp.where(r16 == (idx >> 4), 1.0, 0.0)
                return jnp.where(flat == idx, -jnp.inf, x), cnt, z + jnp.exp(m - top)

            _, cnt, z = lax.fori_loop(0, k, pick_exact, (_pair_candidates(v1, v2), jnp.zeros((k, tn), F32),
                                                          jnp.zeros((1, tn), F32)))
            emit(h, s1, s2, rank1, rank2, [cnt[a:a + 1] for a in range(k)], z)

        return c

    lax.fori_loop(0, PEER_HEADS, head, 0)


def _peer_select(pqT, k1, k2):
    _, n = pqT.shape
    tn = LANES
    nt = n // tn
    tab = jax.ShapeDtypeStruct((PEER_HEADS, N_KEYS, n), F32)
    tab_spec = pl.BlockSpec((PEER_HEADS, N_KEYS, tn), lambda i: (0, 0, i))
    tabb = jax.ShapeDtypeStruct((PEER_HEADS, N_KEYS // BF16_ROWS, BF16_ROWS, n), BF16)
    tabb_spec = pl.BlockSpec((PEER_HEADS, N_KEYS // BF16_ROWS, BF16_ROWS, tn), lambda i: (0, 0, 0, i))
    list_scr = pltpu.VMEM((PEER_TOPK, tn), F32)
    rank_scr = pltpu.VMEM((N_KEYS, tn), F32)
    return pl.pallas_call(
        _peer_select_kernel,
        grid=(nt,),
        in_specs=[pl.BlockSpec((pqT.shape[0], tn), lambda i: (0, i)),
                  pl.BlockSpec(k1.shape, lambda i: (0, 0)),
                  pl.BlockSpec(k2.shape, lambda i: (0, 0))],
        out_specs=(tab_spec, tab_spec, tabb_spec, tabb_spec),
        out_shape=(tab, tab, tabb, tabb),
        scratch_shapes=[list_scr, list_scr, rank_scr, rank_scr],
        compiler_params=_cparams("arbitrary"),
        name="peer_select",
    )(pqT, k1, k2)


def _peer_dense_kernel(alpha, i1_per_step, h2T_ref, u_ref, vT_ref, cnt1_ref, w1_ref, rank2_ref, e2_ref,
                       x1_ref, g2_ref, lng_ref, lnb_ref, y_ref, acc_scr, aT_scr, act_scr):
    j = pl.program_id(1)

    @pl.when(j == 0)
    def _():
        acc_scr[...] = jnp.zeros(acc_scr.shape, F32)

    aT_scr[...] = _dot(u_ref[...], h2T_ref[...])
    tn = aT_scr.shape[1]
    tiles = (N_KEYS // BF16_ROWS, BF16_ROWS, tn)
    zero = jnp.zeros(tiles, BF16)
    for ii in range(i1_per_step):
        i1 = j * i1_per_step + ii
        gate = zero
        for h in range(PEER_HEADS):
            c1 = jnp.broadcast_to(cnt1_ref[h, pl.ds(i1, 1), :], (BF16_ROWS, tn)).astype(BF16)
            w1 = jnp.broadcast_to(w1_ref[h, pl.ds(i1, 1), :], (BF16_ROWS, tn)).astype(BF16)
            gate = gate + jnp.where(rank2_ref[h] < c1[None], e2_ref[h], zero) * w1[None]
        rows = slice(ii * N_KEYS, (ii + 1) * N_KEYS)
        a = aT_scr[rows, :].astype(BF16)
        t = jnp.tanh(a * (_GELU_C1 + _GELU_C2 * (a * a)))
        act_scr[rows, :] = ((a + a * t).reshape(tiles) * gate).reshape(N_KEYS, tn)
    acc_scr[...] += _dot(vT_ref[...], act_scr[...])

    @pl.when(j == pl.num_programs(1) - 1)
    def _():
        peer = acc_scr[...].T
        y_ref[...] = _layer_norm(alpha * x1_ref[...] + g2_ref[0] * peer, lng_ref[...], lnb_ref[...])


def _peer_dense(alpha, h2T, u_b, vT_b, tabs, x1, g2, ln_g, ln_b, rows_per_vec):
    d, n = h2T.shape
    tn = min(512, rows_per_vec) if g2.shape[1] == 1 else g2.shape[1]
    tiles_per_vec = rows_per_vec // tn
    i1_per_step = 8
    te = i1_per_step * N_KEYS
    n_exp = u_b.shape[0]
    tab_spec = pl.BlockSpec((PEER_HEADS, N_KEYS, tn), lambda i, j: (0, 0, i))
    tabb_spec = pl.BlockSpec((PEER_HEADS, N_KEYS // BF16_ROWS, BF16_ROWS, tn), lambda i, j: (0, 0, 0, i))
    return pl.pallas_call(
        functools.partial(_peer_dense_kernel, alpha, i1_per_step),
        grid=(n // tn, n_exp // te),
        in_specs=[pl.BlockSpec((d, tn), lambda i, j: (0, i)),
                  pl.BlockSpec((te, d), lambda i, j: (j, 0)),
                  pl.BlockSpec((d, te), lambda i, j: (0, j)),
                  tab_spec, tab_spec, tabb_spec, tabb_spec,
                  pl.BlockSpec((tn, d), lambda i, j: (i, 0)),
                  pl.BlockSpec((1,) + g2.shape[1:], lambda i, j: (i // tiles_per_vec, 0, 0)),
                  pl.BlockSpec(ln_g.shape, lambda i, j: (0, 0)),
                  pl.BlockSpec(ln_b.shape, lambda i, j: (0, 0))],
        out_specs=pl.BlockSpec((tn, d), lambda i, j: (i, 0)),
        out_shape=jax.ShapeDtypeStruct((n, d), F32),
        scratch_shapes=[pltpu.VMEM((d, tn), F32), pltpu.VMEM((te, tn), F32), pltpu.VMEM((te, tn), BF16)],
        compiler_params=_cparams("arbitrary", "arbitrary"),
        name="peer_dense_ln2",
    )(h2T, u_b, vT_b, *tabs, x1, g2, ln_g, ln_b)


def _pack_w_in(w):
    widths = (ATTN_WIDTH, KV_WIDTH, KV_WIDTH, N_IDX_HEADS * IDX_DIM, N_IDX_HEADS, IDX_DIM,
              D_CONV, D_CONV, D_CONV, w.shape[0], w.shape[0])
    parts, start = [], 0
    for wd in widths:
        parts.append(w[:, start:start + wd])
        start += wd
    q, k, v, qi, wi, ki, bg, cg, xin, ga, gb = parts
    pad = jnp.zeros((w.shape[0], _KIWI - IDX_DIM - N_IDX_HEADS), w.dtype)
    return jnp.concatenate([q, k, v, qi, ki, wi, pad, bg, cg, xin, ga, gb], axis=1).astype(BF16)


def kernel(x_prompt, x_sample, c_prompt, c_sample, cache_k, cache_v, cache_kidx, state_conv, page_table,
           rel_bias, w_ada, b_ada, w_in, conv_w, conv_b, w_o_attn, w_o_conv, w_out, ln1_g, ln1_b,
           ln2_g, ln2_b, peer_wq, peer_k1, peer_k2, peer_u, peer_v):
    depth = w_in.shape[0]
    alpha = (2 * depth) ** 0.25
    bp, sp, d = x_prompt.shape
    bs, ts, _ = x_sample.shape
    n_pool = cache_k.shape[1]
    past = page_table.shape[1] * PAGE_SIZE
    ns = bs * ts

    xp, xs = x_prompt, x_sample.reshape(ns, d)
    outs = [[] for _ in range(8)]
    for l in range(depth):
        c_all = jnp.concatenate([c_prompt, c_sample], axis=0)
        c_all = jnp.pad(c_all, ((0, (-c_all.shape[0]) % 16), (0, 0)))
        mod = _ada(c_all, w_ada[l], b_ada[l])[:bp + bs]
        sh1, sc1, g1, sh2, sc2, g2 = jnp.split(mod, 6, axis=-1)
        as_p = lambda a: a[:bp].reshape(bp, 1, d)
        as_s = lambda a: jnp.repeat(a[bp:], ts, axis=0).reshape(1, ns, d)

        w_all = _pack_w_in(w_in[l])
        woa, woc, wout = w_o_attn[l].astype(BF16), w_o_conv[l].astype(BF16), w_out[l].astype(BF16)
        wq = peer_wq[l].astype(BF16)
        k1, k2 = peer_k1[l].astype(BF16), peer_k2[l].astype(BF16)
        u_b = peer_u[l].astype(BF16)
        vT_b = peer_v[l].T.astype(BF16)
        lg1, lb1 = ln1_g[l].reshape(1, d), ln1_b[l].reshape(1, d)
        lg2, lb2 = ln2_g[l].reshape(1, d), ln2_b[l].reshape(1, d)

        (qT, k_p, v_p, kb_p, vTb_p, qiT, ki_p, kib_p, wiT, oconv_p, ga_p, gb_p, ulast_p) = _inproj_prompt(
            xp, as_p(sc1), as_p(sh1), w_all, conv_w[l], conv_b[l])
        oattn_p = _attn_prompt(rel_bias, qT, qiT, wiT, kb_p, vTb_p, kib_p)
        x1_p, h2T_p, pqT_p = _merge(alpha, xp.reshape(bp * sp, d), oattn_p.reshape(bp * sp, ATTN_WIDTH),
                                    oconv_p.reshape(bp * sp, D_CONV), ga_p.reshape(bp * sp, d),
                                    gb_p.reshape(bp * sp, d), as_p(g1), as_p(sc2), as_p(sh2),
                                    woa, woc, wout, wq, lg1, lb1, sp)
        tabs_p = _peer_select(pqT_p, k1, k2)
        y_p = _peer_dense(alpha, h2T_p, u_b, vT_b, tabs_p, x1_p, as_p(g2), lg2, lb2, sp)
        xp = y_p.reshape(bp, sp, d)

        st = state_conv[l]
        st0 = jnp.repeat(st[:, 0], ts, axis=0)
        st1 = jnp.repeat(st[:, 1], ts, axis=0)
        (q_s, k_s, v_s, qi_s, kiwi_s, oconv_s, ga_s, gb_s, u_s) = _inproj_sample(
            xs, as_s(sc1)[0], as_s(sh1)[0], w_all, conv_w[l], conv_b[l], st0, st1, ts)
        ki_s = kiwi_s[:, :IDX_DIM]
        wi_s = kiwi_s[:, IDX_DIM:IDX_DIM + N_IDX_HEADS]
        q5 = q_s.reshape(bs, ts, N_KV_HEADS, GROUP, HEAD_DIM).transpose(0, 2, 3, 1, 4)
        qpad = jnp.zeros((bs, N_KV_HEADS, GROUP * ts, N_KV_HEADS, HEAD_DIM), F32)
        for g in range(N_KV_HEADS):
            qpad = qpad.at[:, g, :, g, :].set(q5[:, g].reshape(bs, GROUP * ts, HEAD_DIM))
        qpad = qpad.reshape(bs, N_HEADS * ts, KV_WIDTH).astype(BF16)
        qi_r = qi_s.reshape(bs, ts, N_IDX_HEADS, IDX_DIM).transpose(0, 2, 1, 3).reshape(
            bs, N_IDX_HEADS * ts, IDX_DIM).astype(BF16)
        wi_r = jnp.broadcast_to(wi_s.reshape(bs, ts, N_IDX_HEADS).transpose(0, 2, 1).reshape(
            bs, N_IDX_HEADS * ts, 1), (bs, N_IDX_HEADS * ts, LANES))
        o_s = _attn_sample(page_table, rel_bias, qpad, qi_r, wi_r,
                           k_s.reshape(bs, ts, KV_WIDTH), v_s.reshape(bs, ts, KV_WIDTH),
                           ki_s.reshape(bs, ts, IDX_DIM),
                           cache_k[l].reshape(n_pool, PAGE_SIZE, KV_WIDTH),
                           cache_v[l].reshape(n_pool, PAGE_SIZE, KV_WIDTH), cache_kidx[l], past)
        o5 = o_s.reshape(bs, N_KV_HEADS, GROUP, ts, N_KV_HEADS, HEAD_DIM)
        oattn_s = jnp.stack([o5[:, g, :, :, g, :] for g in range(N_KV_HEADS)], axis=1)
        oattn_s = oattn_s.transpose(0, 3, 1, 2, 4).reshape(ns, ATTN_WIDTH).astype(BF16)
        x1_s, h2T_s, pqT_s = _merge(alpha, xs, oattn_s, oconv_s, ga_s, gb_s, as_s(g1), as_s(sc2), as_s(sh2),
                                    woa, woc, wout, wq, lg1, lb1, ns)
        tabs_s = _peer_select(pqT_s, k1, k2)
        xs = _peer_dense(alpha, h2T_s, u_b, vT_b, tabs_s, x1_s, as_s(g2), lg2, lb2, ns)

        for lst, val in zip(outs, (
                k_p.reshape(bp, sp, N_KV_HEADS, HEAD_DIM), v_p.reshape(bp, sp, N_KV_HEADS, HEAD_DIM), ki_p,
                ulast_p[:, 8 - (CONV_WIDTH - 1):],
                k_s.reshape(bs, ts, N_KV_HEADS, HEAD_DIM), v_s.reshape(bs, ts, N_KV_HEADS, HEAD_DIM),
                ki_s.reshape(bs, ts, IDX_DIM),
                u_s.reshape(bs, ts, D_CONV)[:, ts - (CONV_WIDTH - 1):])):
            lst.append(val)

    return (xp, xs.reshape(bs, ts, d)) + tuple(jnp.stack(o) for o in outs)
```

```python
import functools
import math

import jax
import jax.numpy as jnp
from jax import lax
from jax.experimental import pallas as pl
from jax.experimental.pallas import tpu as pltpu

F32 = jnp.float32
BF16 = jnp.bfloat16
I32 = jnp.int32

N_HEADS = 8
N_KV_HEADS = 2
HEAD_DIM = 64
GROUP = N_HEADS // N_KV_HEADS
ATTN_WIDTH = N_HEADS * HEAD_DIM
KV_WIDTH = N_KV_HEADS * HEAD_DIM
ATTN_SCALE = HEAD_DIM ** -0.5
N_IDX_HEADS = 4
IDX_DIM = 64
IDX_SCALE = (IDX_DIM * N_IDX_HEADS) ** -0.5
TOPK_MAX = 256
N_BUCKETS = 32
MAX_DISTANCE = 128
D_CONV = 512
CONV_WIDTH = 3
N_KEYS = 128
PEER_HEADS = 8
PEER_HALF = 64
PEER_TOPK = 16
LN_EPS = 1e-5
PAGE_SIZE = 128

LANES = 128
BF16_ROWS = 16
VMEM_LIMIT = 56 * 1024 * 1024

_GELU_C1 = math.sqrt(2.0 / math.pi)
_GELU_C2 = _GELU_C1 * 0.044715

NEG_BIG = -1e30

_KIWI = LANES
_OFF_Q = 0
_OFF_K = _OFF_Q + ATTN_WIDTH
_OFF_V = _OFF_K + KV_WIDTH
_OFF_QI = _OFF_V + KV_WIDTH
_OFF_KIWI = _OFF_QI + N_IDX_HEADS * IDX_DIM
_OFF_BG = _OFF_KIWI + _KIWI
_OFF_CG = _OFF_BG + D_CONV
_OFF_XIN = _OFF_CG + D_CONV
_OFF_GA = _OFF_XIN + D_CONV


def _cparams(*sem):
    return pltpu.CompilerParams(dimension_semantics=sem, vmem_limit_bytes=VMEM_LIMIT)


def _dot(a, b):
    return jnp.dot(a, b, preferred_element_type=F32)


def _dot_nt(a, b):
    return lax.dot_general(a, b, (((1,), (1,)), ((), ())), preferred_element_type=F32)


_COARSE_STEPS = 8


def _kth_largest(stats, count_ge, k, n_valid, v_min, v_max):
    enough = n_valid >= k
    lo0 = jnp.where(enough, v_min, -jnp.inf)
    hi0 = jnp.where(enough, v_max, -jnp.inf)

    def cond(state):
        lo, hi = state
        return jnp.max(jnp.where(lo < hi, 1.0, 0.0)) > 0.0

    def midpoint(lo, hi):
        mid = lo + (hi - lo) * 0.5
        return jnp.where(mid > lo, mid, hi)

    def coarse_step(_, state):
        lo, hi = state
        mid = midpoint(lo, hi)
        cnt = count_ge(mid)
        active = lo < hi
        return (jnp.where(active & (cnt >= k), mid, lo), jnp.where(active & (cnt < k), mid, hi))

    def step(state):
        lo, hi = state
        cnt, above, below = stats(midpoint(lo, hi))
        active = lo < hi
        return (jnp.where(active & (cnt >= k), above, lo), jnp.where(active & (cnt < k), below, hi))

    state = lax.fori_loop(0, _COARSE_STEPS, coarse_step, (lo0, hi0))
    lo, _ = lax.while_loop(cond, lambda state: step(step(state)), state)
    return lo


def _t5_bias(dist, relb_ref, h):
    n = jnp.maximum(dist, 0)
    max_exact = N_BUCKETS // 2
    nf = jnp.maximum(n, 1).astype(F32)
    large = max_exact + jnp.floor(jnp.log(nf / max_exact) / math.log(MAX_DISTANCE / max_exact)
                                  * (N_BUCKETS - max_exact)).astype(I32)
    large = jnp.minimum(large, N_BUCKETS - 1)
    bucket = jnp.where(n < max_exact, n, large)
    out = jnp.zeros(dist.shape, F32)
    for b in range(N_BUCKETS):
        out = jnp.where(bucket == b, relb_ref[b, h], out)
    return out


def _layer_norm(x, g, b):
    mu = jnp.mean(x, axis=-1, keepdims=True)
    var = jnp.mean(jnp.square(x - mu), axis=-1, keepdims=True)
    return (x - mu) * lax.rsqrt(var + LN_EPS) * g + b


def _ada_kernel(c_ref, w_ref, b_ref, o_ref):
    o_ref[...] = _dot(c_ref[...].astype(BF16), w_ref[...].astype(BF16)) + b_ref[...]


def _ada(c, w, b):
    m, d = c.shape
    n = w.shape[1]
    tn = n // 4
    return pl.pallas_call(
        _ada_kernel,
        grid=(n // tn,),
        in_specs=[pl.BlockSpec((m, d), lambda j: (0, 0)),
                  pl.BlockSpec((d, tn), lambda j: (0, j)),
                  pl.BlockSpec((1, tn), lambda j: (0, j))],
        out_specs=pl.BlockSpec((m, tn), lambda j: (0, j)),
        out_shape=jax.ShapeDtypeStruct((m, n), F32),
        compiler_params=_cparams("arbitrary"),
        name="ada_mod",
    )(c, w, b.reshape(1, n))


def _conv_out(bg, u, u1, u2, cw_ref, cb_ref):
    y = cb_ref[...] + u2 * cw_ref[0:1, :] + u1 * cw_ref[1:2, :] + u * cw_ref[2:3, :]
    return bg * y


def _inproj_prompt_kernel(x_ref, sc_ref, sh_ref, w_ref, cw_ref, cb_ref,
                          qT_ref, k_ref, v_ref, kb_ref, vTb_ref, qiT_ref, ki_ref, kib_ref, wiT_ref,
                          oconv_ref, ga_ref, gb_ref, ulast_ref, uprev_scr):
    s = pl.program_id(1)
    tm = x_ref.shape[1]
    h = (x_ref[0] * (1.0 + sc_ref[0]) + sh_ref[0]).astype(BF16)

    def proj(a, width):
        return _dot(h, w_ref[:, a:a + width])

    qT_ref[0] = proj(_OFF_Q, ATTN_WIDTH).T.astype(BF16)
    k = proj(_OFF_K, KV_WIDTH)
    k_ref[0] = k
    kb_ref[0] = k.astype(BF16)
    v = proj(_OFF_V, KV_WIDTH)
    v_ref[0] = v
    for i in range(tm // LANES):
        vTb_ref[0, i] = v[i * LANES:(i + 1) * LANES].T.astype(BF16)
    qiT_ref[0] = proj(_OFF_QI, N_IDX_HEADS * IDX_DIM).T.astype(BF16)
    kiwi = proj(_OFF_KIWI, _KIWI)
    ki_ref[0] = kiwi[:, :IDX_DIM]
    kib_ref[0] = kiwi[:, :IDX_DIM].astype(BF16)
    wiT_ref[0] = kiwi.T[IDX_DIM:IDX_DIM + 8]

    @pl.when(s == 0)
    def _():
        uprev_scr[...] = jnp.zeros(uprev_scr.shape, F32)

    bg = proj(_OFF_BG, D_CONV)
    u = proj(_OFF_CG, D_CONV) * proj(_OFF_XIN, D_CONV)
    row = lax.broadcasted_iota(I32, u.shape, 0)
    up = uprev_scr[...]
    u1 = jnp.where(row < 1, pltpu.roll(up, 1, 0), pltpu.roll(u, 1, 0))
    u2 = jnp.where(row < 2, pltpu.roll(up, 2, 0), pltpu.roll(u, 2, 0))
    oconv_ref[0] = _conv_out(bg, u, u1, u2, cw_ref, cb_ref).astype(BF16)
    uprev_scr[...] = u
    ulast_ref[0] = u[tm - 8:, :]

    ga_ref[0] = proj(_OFF_GA, w_ref.shape[0])
    gb_ref[0] = proj(_OFF_GA + w_ref.shape[0], w_ref.shape[0])


def _inproj_prompt(x, sc, sh, w_all, conv_w, conv_b):
    b, s, d = x.shape
    tm = min(256, s)
    nkb = tm // LANES
    row = lambda width, dt: jax.ShapeDtypeStruct((b, s, width), dt)
    out_shape = (
        jax.ShapeDtypeStruct((b, ATTN_WIDTH, s), BF16),
        row(KV_WIDTH, F32), row(KV_WIDTH, F32), row(KV_WIDTH, BF16),
        jax.ShapeDtypeStruct((b, s // LANES, KV_WIDTH, LANES), BF16),
        jax.ShapeDtypeStruct((b, N_IDX_HEADS * IDX_DIM, s), BF16),
        row(IDX_DIM, F32), row(IDX_DIM, BF16),
        jax.ShapeDtypeStruct((b, 8, s), F32),
        row(D_CONV, BF16),
        row(d, F32), row(d, F32),
        jax.ShapeDtypeStruct((b, 8, D_CONV), F32),
    )
    tile = lambda width: pl.BlockSpec((1, tm, width), lambda i, j: (i, j, 0))
    tileT = lambda rows: pl.BlockSpec((1, rows, tm), lambda i, j: (i, 0, j))
    out_specs = (
        tileT(ATTN_WIDTH), tile(KV_WIDTH), tile(KV_WIDTH), tile(KV_WIDTH),
        pl.BlockSpec((1, nkb, KV_WIDTH, LANES), lambda i, j: (i, j, 0, 0)),
        tileT(N_IDX_HEADS * IDX_DIM), tile(IDX_DIM), tile(IDX_DIM), tileT(8),
        tile(D_CONV), tile(d), tile(d),
        pl.BlockSpec((1, 8, D_CONV), lambda i, j: (i, 0, 0)),
    )
    vec = pl.BlockSpec((1, 1, d), lambda i, j: (i, 0, 0))
    return pl.pallas_call(
        _inproj_prompt_kernel,
        grid=(b, s // tm),
        in_specs=[tile(d), vec, vec,
                  pl.BlockSpec(w_all.shape, lambda i, j: (0, 0)),
                  pl.BlockSpec(conv_w.shape, lambda i, j: (0, 0)),
                  pl.BlockSpec((1, D_CONV), lambda i, j: (0, 0))],
        out_specs=out_specs,
        out_shape=out_shape,
        scratch_shapes=[pltpu.VMEM((tm, D_CONV), F32)],
        compiler_params=_cparams("arbitrary", "arbitrary"),
        name="inproj_prompt",
    )(x, sc, sh, w_all, conv_w, conv_b.reshape(1, D_CONV))


def _inproj_sample_kernel(t_seq, x_ref, sc_ref, sh_ref, w_ref, cw_ref, cb_ref, s0_ref, s1_ref,
                          q_ref, k_ref, v_ref, qi_ref, kiwi_ref, oconv_ref, ga_ref, gb_ref, u_ref):
    h = (x_ref[...] * (1.0 + sc_ref[...]) + sh_ref[...]).astype(BF16)

    def proj(a, width):
        return _dot(h, w_ref[:, a:a + width])

    q_ref[...] = proj(_OFF_Q, ATTN_WIDTH)
    k_ref[...] = proj(_OFF_K, KV_WIDTH)
    v_ref[...] = proj(_OFF_V, KV_WIDTH)
    qi_ref[...] = proj(_OFF_QI, N_IDX_HEADS * IDX_DIM)
    kiwi_ref[...] = proj(_OFF_KIWI, _KIWI)
    bg = proj(_OFF_BG, D_CONV)
    u = proj(_OFF_CG, D_CONV) * proj(_OFF_XIN, D_CONV)
    t = lax.broadcasted_iota(I32, u.shape, 0) % t_seq
    u1 = jnp.where(t == 0, s1_ref[...], pltpu.roll(u, 1, 0))
    u2 = jnp.where(t == 0, s0_ref[...], jnp.where(t == 1, s1_ref[...], pltpu.roll(u, 2, 0)))
    oconv_ref[...] = _conv_out(bg, u, u1, u2, cw_ref, cb_ref).astype(BF16)
    u_ref[...] = u
    ga_ref[...] = proj(_OFF_GA, w_ref.shape[0])
    gb_ref[...] = proj(_OFF_GA + w_ref.shape[0], w_ref.shape[0])


def _inproj_sample(x, sc_rows, sh_rows, w_all, conv_w, conv_b, st0_rows, st1_rows, t_seq):
    n, d = x.shape
    full = lambda a: pl.BlockSpec(a.shape, lambda i: (0,) * a.ndim)
    o = lambda width, dt: jax.ShapeDtypeStruct((n, width), dt)
    out_shape = (o(ATTN_WIDTH, F32), o(KV_WIDTH, F32), o(KV_WIDTH, F32), o(N_IDX_HEADS * IDX_DIM, F32),
                 o(_KIWI, F32), o(D_CONV, BF16), o(d, F32), o(d, F32), o(D_CONV, F32))
    args = (x, sc_rows, sh_rows, w_all, conv_w, conv_b.reshape(1, D_CONV), st0_rows, st1_rows)
    return pl.pallas_call(
        functools.partial(_inproj_sample_kernel, t_seq),
        grid=(1,),
        in_specs=[full(a) for a in args],
        out_specs=tuple(pl.BlockSpec(s.shape, lambda i: (0, 0)) for s in out_shape),
        out_shape=out_shape,
        compiler_params=_cparams("arbitrary"),
        name="inproj_sample",
    )(*args)


def _attn_prompt_kernel(n_sel, relb_ref, qT_ref, qiT_ref, wiT_ref, kb_ref, vTb_ref, kib_ref,
                        o_ref, bias_scr, key_scr, sel_scr):
    qb = pl.program_id(1)
    blk = LANES
    kblk = 2 * blk
    ntrip = qb // 2 + 1

    @pl.when((pl.program_id(0) == 0) & (qb == 0))
    def _():
        j = lax.broadcasted_iota(I32, (blk, blk), 0)
        i = lax.broadcasted_iota(I32, (blk, blk), 1)
        for delta in range(3):
            dist = i - j + blk * delta
            for h in range(N_HEADS):
                g, r = divmod(h, GROUP)
                bias_scr[delta, g, :, r * blk:(r + 1) * blk] = _t5_bias(dist, relb_ref, h)

    row = lax.broadcasted_iota(I32, (kblk, blk), 0)
    lane = lax.broadcasted_iota(I32, (kblk, blk), 1)

    def valid_mask(t):
        return (t * kblk + row) <= (qb * blk + lane)

    def key_rows(t):
        return pl.ds(pl.multiple_of(t * kblk, kblk), kblk)

    qiT = jnp.concatenate([qiT_ref[0, h * IDX_DIM:(h + 1) * IDX_DIM, :] for h in range(N_IDX_HEADS)], axis=1)
    wi = [wiT_ref[0, h:h + 1, :] for h in range(N_IDX_HEADS)]
    fold = lambda a: a.reshape(kblk // 8, 8, blk)
    tall = lambda a: jnp.sum(a, axis=0, keepdims=True)
    part0 = (jnp.zeros((8, blk), F32), jnp.full((8, blk), jnp.inf, F32), jnp.full((8, blk), -jnp.inf, F32))

    def score_blk(t, carry):
        cnt, lo, hi = carry
        d = _dot(kib_ref[0, key_rows(t), :], qiT)
        s = jnp.maximum(d[:, 0:blk], 0.0) * wi[0]
        for h in range(1, N_IDX_HEADS):
            s = s + jnp.maximum(d[:, h * blk:(h + 1) * blk], 0.0) * wi[h]
        s = s * IDX_SCALE
        valid = valid_mask(t)
        key_scr[key_rows(t), :] = jnp.where(valid, s, -jnp.inf)
        return (cnt + jnp.sum(fold(jnp.where(valid, 1.0, 0.0)), axis=0),
                jnp.minimum(lo, jnp.min(fold(jnp.where(valid, s, jnp.inf)), axis=0)),
                jnp.maximum(hi, jnp.max(fold(jnp.where(valid, s, -jnp.inf)), axis=0)))

    cnt, lo, hi = lax.fori_loop(0, ntrip, score_blk, part0)
    n_valid = tall(cnt)
    v_min = jnp.min(lo, axis=0, keepdims=True)
    v_max = jnp.max(hi, axis=0, keepdims=True)

    def stats(mid):
        def body(t, carry):
            cnt, above, below = carry
            x = key_scr[key_rows(t), :]
            ge = x >= mid
            return (cnt + jnp.sum(fold(jnp.where(ge, 1.0, 0.0)), axis=0),
                    jnp.minimum(above, jnp.min(fold(jnp.where(ge, x, jnp.inf)), axis=0)),
                    jnp.maximum(below, jnp.max(fold(jnp.where(ge, -jnp.inf, x)), axis=0)))
        cnt, above, below = lax.fori_loop(0, ntrip, body, part0)
        return tall(cnt), jnp.min(above, axis=0, keepdims=True), jnp.max(below, axis=0, keepdims=True)

    def count(pred):
        def body(t, acc):
            return acc + jnp.sum(fold(jnp.where(pred(key_scr[key_rows(t), :]), 1.0, 0.0)), axis=0)
        return tall(lax.fori_loop(0, ntrip, body, jnp.zeros((8, blk), F32)))

    thr = _kth_largest(stats, lambda mid: count(lambda x: x >= mid), float(n_sel), n_valid, v_min, v_max)

    need = n_sel - count(lambda x: x > thr)
    ltri = (lax.broadcasted_iota(I32, (kblk, kblk), 1) <= lax.broadcasted_iota(I32, (kblk, kblk), 0)).astype(BF16)

    def sel_blk(t, carry):
        kv = key_scr[key_rows(t), :]
        eq = kv == thr
        prefix = _dot(ltri, eq.astype(BF16)) + carry
        sel = ((kv > thr) | (eq & (prefix <= need))) & valid_mask(t)
        sel_scr[key_rows(t), :] = jnp.where(sel, 0.0, NEG_BIG)
        return prefix[kblk - 1:kblk, :]

    lax.fori_loop(0, ntrip, sel_blk, jnp.zeros((1, blk), F32))

    zero = jnp.zeros((HEAD_DIM, blk), BF16)
    scale = jnp.asarray(ATTN_SCALE, BF16)
    qTg = []
    for g in range(N_KV_HEADS):
        cols = []
        for r in range(GROUP):
            base = g * GROUP * HEAD_DIM + r * HEAD_DIM
            parts = [zero] * N_KV_HEADS
            parts[g] = qT_ref[0, base:base + HEAD_DIM, :] * scale
            cols.append(jnp.concatenate(parts, axis=0))
        qTg.append(jnp.concatenate(cols, axis=1))
    width = GROUP * blk

    def att_blk(t, carry):
        kmat = kb_ref[0, key_rows(t), :]
        mask = jnp.concatenate([sel_scr[key_rows(t), :]] * GROUP, axis=1)
        d0 = jnp.clip(qb - 2 * t, 0, 2)
        d1 = jnp.clip(qb - 2 * t - 1, 0, 2)
        new = []
        for g in range(N_KV_HEADS):
            m, l, acc = carry[g]
            bias = jnp.concatenate([bias_scr[d0, g], bias_scr[d1, g]], axis=0)
            lg = _dot(kmat, qTg[g]) + bias + mask
            m_new = jnp.maximum(m, jnp.max(lg, axis=0, keepdims=True))
            p = jnp.exp(lg - m_new)
            alpha = jnp.exp(m - m_new)
            l = alpha * l + jnp.sum(p, axis=0, keepdims=True)
            rows = slice(g * HEAD_DIM, (g + 1) * HEAD_DIM)
            vt = jnp.concatenate([vTb_ref[0, 2 * t, rows, :], vTb_ref[0, 2 * t + 1, rows, :]], axis=1)
            acc = alpha * acc + _dot(vt, p.astype(BF16))
            new.append((m_new, l, acc))
        return tuple(new)

    init = (jnp.full((1, width), NEG_BIG, F32), jnp.zeros((1, width), F32), jnp.zeros((HEAD_DIM, width), F32))
    res = lax.fori_loop(0, ntrip, att_blk, (init,) * N_KV_HEADS)
    outs = []
    for g in range(N_KV_HEADS):
        _, l, acc = res[g]
        og = acc / l
        outs += [og[:, r * blk:(r + 1) * blk] for r in range(GROUP)]
    o_ref[0] = jnp.concatenate(outs, axis=0).T.astype(BF16)


def _attn_prompt(rel_bias, qT, qiT, wiT, kb, vTb, kib):
    b, _, s = qT.shape
    n_sel = min(TOPK_MAX, s // 4)
    blk = LANES
    per_b = lambda a: pl.BlockSpec((1,) + a.shape[1:], lambda i, j: (i,) + (0,) * (a.ndim - 1))
    qtile = lambda rows: pl.BlockSpec((1, rows, blk), lambda i, j: (i, 0, j))
    return pl.pallas_call(
        functools.partial(_attn_prompt_kernel, n_sel),
        grid=(b, s // blk),
        in_specs=[pl.BlockSpec(memory_space=pltpu.SMEM),
                  qtile(ATTN_WIDTH), qtile(N_IDX_HEADS * IDX_DIM), qtile(8),
                  per_b(kb), per_b(vTb), per_b(kib)],
        out_specs=pl.BlockSpec((1, blk, ATTN_WIDTH), lambda i, j: (i, j, 0)),
        out_shape=jax.ShapeDtypeStruct((b, s, ATTN_WIDTH), BF16),
        scratch_shapes=[pltpu.VMEM((3, N_KV_HEADS, blk, GROUP * blk), F32),
                        pltpu.VMEM((s, blk), F32),
                        pltpu.VMEM((s, blk), F32)],
        compiler_params=_cparams("arbitrary", "arbitrary"),
        name="attn_prompt",
    )(rel_bias, qT, qiT, wiT, kb, vTb, kib)


def _attn_sample_kernel(n_sel, past, t_seq, pt_ref, relb_ref, qpad_ref, qi_ref, wi_ref,
                        knew_ref, vnew_ref, kinew_ref, ck_hbm, cv_hbm, cki_hbm,
                        o_ref, kbuf, vbuf, kibuf, sem):
    b = pl.program_id(0)
    nb = pl.num_programs(0)
    n_pages = past // PAGE_SIZE
    lp = kbuf.shape[1]
    blk = LANES
    nblk = lp // blk
    slot = b % 2

    def page_copies(seq, sl, p):
        phys = pt_ref[seq, p]
        rows = pl.ds(pl.multiple_of(p * PAGE_SIZE, PAGE_SIZE), PAGE_SIZE)
        return (pltpu.make_async_copy(ck_hbm.at[phys], kbuf.at[sl, rows, :], sem.at[sl, 0]),
                pltpu.make_async_copy(cv_hbm.at[phys], vbuf.at[sl, rows, :], sem.at[sl, 1]),
                pltpu.make_async_copy(cki_hbm.at[phys], kibuf.at[sl, rows, :], sem.at[sl, 2]))

    def start_all(seq, sl):
        def body(p, c):
            for cp in page_copies(seq, sl, p):
                cp.start()
            return c
        lax.fori_loop(0, n_pages, body, 0)

    def wait_all(seq, sl):
        def body(p, c):
            for cp in page_copies(seq, sl, p):
                cp.wait()
            return c
        lax.fori_loop(0, n_pages, body, 0)

    @pl.when(b == 0)
    def _():
        for sl in range(2):
            kbuf[sl, past:, :] = jnp.zeros((blk, KV_WIDTH), F32)
            vbuf[sl, past:, :] = jnp.zeros((blk, KV_WIDTH), F32)
            kibuf[sl, past:, :] = jnp.zeros((blk, IDX_DIM), F32)
        start_all(0, 0)

    @pl.when(b + 1 < nb)
    def _():
        start_all(b + 1, 1 - slot)

    kbuf[slot, past:past + t_seq, :] = knew_ref[0]
    vbuf[slot, past:past + t_seq, :] = vnew_ref[0]
    kibuf[slot, past:past + t_seq, :] = kinew_ref[0]
    wait_all(b, slot)

    d = _dot_nt(qi_ref[0], kibuf[slot].astype(BF16))
    s = jnp.maximum(d[0:t_seq], 0.0) * wi_ref[0, 0:t_seq, 0:1]
    for h in range(1, N_IDX_HEADS):
        s = s + jnp.maximum(d[h * t_seq:(h + 1) * t_seq], 0.0) * wi_ref[0, h * t_seq:(h + 1) * t_seq, 0:1]
    kpos = lax.broadcasted_iota(I32, (t_seq, lp), 1)
    qpos = past + lax.broadcasted_iota(I32, (t_seq, lp), 0)
    valid = kpos <= qpos
    s = s * IDX_SCALE
    key = jnp.where(valid, s, -jnp.inf)

    def count(pred):
        return jnp.sum(jnp.where(pred, 1.0, 0.0), axis=1, keepdims=True)

    def stats(mid):
        ge = key >= mid
        return (count(ge), jnp.min(jnp.where(ge, key, jnp.inf), axis=1, keepdims=True),
                jnp.max(jnp.where(ge, -jnp.inf, key), axis=1, keepdims=True))

    thr = _kth_largest(stats, lambda mid: count(key >= mid), float(n_sel), count(valid),
                       jnp.min(jnp.where(valid, s, jnp.inf), axis=1, keepdims=True),
                       jnp.max(key, axis=1, keepdims=True))
    need = n_sel - count(key > thr)
    eq = key == thr
    eqf = jnp.where(eq, 1.0, 0.0)
    utri = (lax.broadcasted_iota(I32, (blk, blk), 0) <= lax.broadcasted_iota(I32, (blk, blk), 1)).astype(BF16)
    carry = jnp.zeros((t_seq, 1), F32)
    ranks = []
    for kb in range(nblk):
        e = eqf[:, kb * blk:(kb + 1) * blk]
        ranks.append(_dot(e.astype(BF16), utri) + carry)
        carry = carry + jnp.sum(e, axis=1, keepdims=True)
    prefix = jnp.concatenate(ranks, axis=1)
    sel = jnp.where(((key > thr) | (eq & (prefix <= need))) & valid, 1.0, 0.0)

    lg = _dot_nt(qpad_ref[0], kbuf[slot].astype(BF16)) * ATTN_SCALE
    far = lp - 2 * blk
    near_dist = (past + lax.broadcasted_iota(I32, (t_seq, 2 * blk), 0)
                 - (far + lax.broadcasted_iota(I32, (t_seq, 2 * blk), 1)))
    bias_rows = []
    for h in range(N_HEADS):
        far_bias = jnp.full((t_seq, far), relb_ref[N_BUCKETS - 1, h], F32)
        bias_rows.append(jnp.concatenate([far_bias, _t5_bias(near_dist, relb_ref, h)], axis=1))
    lg = lg + jnp.concatenate(bias_rows, axis=0)
    selh = jnp.concatenate([sel] * N_HEADS, axis=0) > 0.0
    m = jnp.max(jnp.where(selh, lg, NEG_BIG), axis=1, keepdims=True)
    p = jnp.where(selh, jnp.exp(lg - m), 0.0)
    l = jnp.sum(p, axis=1, keepdims=True)
    o = _dot(p.astype(BF16), vbuf[slot].astype(BF16))
    o_ref[0] = o / l


def _attn_sample(page_table, rel_bias, qpad, qi_r, wi_r, k_new, v_new, ki_new, cache_k, cache_v, cache_ki,
                 past):
    nb, t_seq, _ = k_new.shape
    n_sel = min(TOPK_MAX, (past + t_seq) // 4)
    lp = past + LANES
    rows = N_HEADS * t_seq
    per_b = lambda a: pl.BlockSpec((1,) + a.shape[1:], lambda i, pt: (i,) + (0,) * (a.ndim - 1))
    hbm = pl.BlockSpec(memory_space=pl.ANY)
    grid_spec = pltpu.PrefetchScalarGridSpec(
        num_scalar_prefetch=1,
        grid=(nb,),
        in_specs=[pl.BlockSpec(memory_space=pltpu.SMEM),
                  per_b(qpad), per_b(qi_r), per_b(wi_r), per_b(k_new), per_b(v_new), per_b(ki_new),
                  hbm, hbm, hbm],
        out_specs=pl.BlockSpec((1, rows, KV_WIDTH), lambda i, pt: (i, 0, 0)),
        scratch_shapes=[pltpu.VMEM((2, lp, KV_WIDTH), F32),
                        pltpu.VMEM((2, lp, KV_WIDTH), F32),
                        pltpu.VMEM((2, lp, IDX_DIM), F32),
                        pltpu.SemaphoreType.DMA((2, 3))],
    )
    return pl.pallas_call(
        functools.partial(_attn_sample_kernel, n_sel, past, t_seq),
        grid_spec=grid_spec,
        out_shape=jax.ShapeDtypeStruct((nb, rows, KV_WIDTH), F32),
        compiler_params=_cparams("arbitrary"),
        name="attn_sample",
    )(page_table, rel_bias, qpad, qi_r, wi_r, k_new, v_new, ki_new, cache_k, cache_v, cache_ki)


def _merge_kernel(alpha, x_ref, oa_ref, oc_ref, ga_ref, gb_ref, g1_ref, sc2_ref, sh2_ref,
                  woa_ref, woc_ref, wout_ref, wq_ref, lng_ref, lnb_ref,
                  x1_ref, h2T_ref, pqT_ref):
    ta = _dot(oa_ref[...], woa_ref[...])
    tc = _dot(oc_ref[...], woc_ref[...])
    merged = jax.nn.sigmoid(ga_ref[...]) * ta + jax.nn.sigmoid(gb_ref[...]) * tc
    out = _dot(merged.astype(BF16), wout_ref[...])
    x1 = _layer_norm(alpha * x_ref[...] + g1_ref[0] * out, lng_ref[...], lnb_ref[...])
    x1_ref[...] = x1
    h2 = x1 * (1.0 + sc2_ref[0]) + sh2_ref[0]
    h2b = h2.astype(BF16)
    h2T_ref[...] = h2.T.astype(BF16)
    pqT_ref[...] = _dot(h2b, wq_ref[...]).T.astype(BF16)


def _merge(alpha, x, oattn, oconv, ga, gb, g1, sc2, sh2, woa, woc, wout, wq, ln_g, ln_b, rows_per_vec):
    n, d = x.shape
    tm = min(512, rows_per_vec) if g1.shape[1] == 1 else g1.shape[1]
    tiles_per_vec = rows_per_vec // tm
    tile = lambda width: pl.BlockSpec((tm, width), lambda i: (i, 0))
    vec = pl.BlockSpec((1,) + g1.shape[1:], lambda i: (i // tiles_per_vec, 0, 0))
    full = lambda a: pl.BlockSpec(a.shape, lambda i: (0,) * a.ndim)
    pq_w = wq.shape[1]
    return pl.pallas_call(
        functools.partial(_merge_kernel, alpha),
        grid=(n // tm,),
        in_specs=[tile(d), tile(ATTN_WIDTH), tile(D_CONV), tile(d), tile(d), vec, vec, vec,
                  full(woa), full(woc), full(wout), full(wq), full(ln_g), full(ln_b)],
        out_specs=(tile(d), pl.BlockSpec((d, tm), lambda i: (0, i)), pl.BlockSpec((pq_w, tm), lambda i: (0, i))),
        out_shape=(jax.ShapeDtypeStruct((n, d), F32), jax.ShapeDtypeStruct((d, n), BF16),
                   jax.ShapeDtypeStruct((pq_w, n), BF16)),
        compiler_params=_cparams("arbitrary"),
        name="merge_ln1_peerq",
    )(x, oattn, oconv, ga, gb, g1, sc2, sh2, woa, woc, wout, wq, ln_g, ln_b)


def _top16(s, v_scr):
    iota = lax.broadcasted_iota(I32, s.shape, 0)

    def body(r, carry):
        x, rank = carry
        m = jnp.max(x, axis=0, keepdims=True)
        idx = jnp.min(jnp.where(x == m, iota, N_KEYS), axis=0, keepdims=True)
        hit = iota == idx
        v_scr[pl.ds(r, 1), :] = m
        return jnp.where(hit, -jnp.inf, x), jnp.where(hit, lax.convert_element_type(r, F32), rank)

    _, rank = lax.fori_loop(0, PEER_TOPK, body, (s, jnp.full(s.shape, float(PEER_TOPK), F32)))
    return rank


def _pair_candidates(v1, v2):
    return jnp.concatenate([v1[0:1] + v2] + [v1[a:a + 1] + v2[0:8] for a in range(1, 8)]
                           + [v1[8:16] + v2[0:1]], axis=0)


def _peer_select_kernel(pqT_ref, k1_ref, k2_ref, cnt1_ref, w1_ref, rank2_ref, e2_ref,
                        v_scr, r_scr, vx1_scr, vx2_scr):
    tn = pqT_ref.shape[1]
    k = PEER_TOPK
    tiles = (N_KEYS // BF16_ROWS, BF16_ROWS, tn)

    def scores(h):
        base = h * 2 * PEER_HALF
        return (_dot(k1_ref[...], pqT_ref[base:base + PEER_HALF, :]),
                _dot(k2_ref[...], pqT_ref[base + PEER_HALF:base + 2 * PEER_HALF, :]))

    def emit(h, s1, s2, rank1, rank2, cnt_rows, z, top1, top2):
        cnt1 = jnp.zeros(s1.shape, F32)
        for a in range(k):
            cnt1 = jnp.where(rank1 == float(a), cnt_rows[a], cnt1)
        cnt1_ref[h] = cnt1
        w1_ref[h] = jnp.exp(s1 - top1) * (0.5 / z)
        rank2_ref[h] = rank2.astype(BF16).reshape(tiles)
        e2_ref[h] = jnp.exp(s2 - top2).astype(BF16).reshape(tiles)

    def finish(h, par, x, z):
        s1, s2 = scores(h)
        rank1, rank2 = r_scr[par, 0], r_scr[par, 1]
        taken = jnp.where(x == -jnp.inf, 1.0, 0.0)
        cnt_rows = ([jnp.sum(taken[0:k], axis=0, keepdims=True)]
                    + [jnp.sum(taken[k + 8 * (a - 1):k + 8 * a], axis=0, keepdims=True) for a in range(1, 8)]
                    + [taken[k + 56 + a:k + 57 + a] for a in range(8)])
        n_pairs = cnt_rows[0]
        for row in cnt_rows[1:]:
            n_pairs = n_pairs + row
        n1 = jnp.sum(jnp.where(rank1 < float(k), 1.0, 0.0), axis=0, keepdims=True)
        n2 = jnp.sum(jnp.where(rank2 < float(k), 1.0, 0.0), axis=0, keepdims=True)
        tied = (n1 != float(k)) | (n2 != float(k)) | (n_pairs != float(k))
        emit(h, s1, s2, rank1, rank2, cnt_rows, z, v_scr[par, 0, 0:1, :], v_scr[par, 1, 0:1, :])

        @pl.when(jnp.sum(jnp.where(tied, 1.0, 0.0)) > 0.0)
        def _():
            rank1 = _top16(s1, vx1_scr)
            rank2 = _top16(s2, vx2_scr)
            v1, v2 = vx1_scr[...], vx2_scr[...]
            top = v1[0:1] + v2[0:1]
            r16 = lax.broadcasted_iota(I32, (k, tn), 0)
            r8 = lax.broadcasted_iota(I32, (8, tn), 0)
            flat = jnp.concatenate([r16] + [a * k + r8 for a in range(1, 8)] + [(8 + r8) * k], axis=0)

            def pick_exact(r, carry):
                x, cnt, z = carry
                m = jnp.max(x, axis=0, keepdims=True)
                idx = jnp.min(jnp.where(x == m, flat, k * k), axis=0, keepdims=True)
                cnt = cnt + jnp.where(r16 == (idx >> 4), 1.0, 0.0)
                return jnp.where(flat == idx, -jnp.inf, x), cnt, z + jnp.exp(m - top)

            _, cnt, z = lax.fori_loop(0, k, pick_exact, (_pair_candidates(v1, v2), jnp.zeros((k, tn), F32),
                                                          jnp.zeros((1, tn), F32)))
            emit(h, s1, s2, rank1, rank2, [cnt[a:a + 1] for a in range(k)], z, v1[0:1], v2[0:1])

    pending = None
    for h in range(PEER_HEADS + 1):
        par = h % 2
        lists = h < PEER_HEADS
        init = []
        if lists:
            init += list(scores(h))
            for half in range(2):
                r_scr[par, half] = jnp.full((N_KEYS, tn), float(k), F32)
        if pending is not None:
            init += [pending[0], jnp.zeros((1, tn), F32)]
            top = pending[1]

        def body(r, carry, lists=lists, pairs=pending is not None, par=par):
            out = []
            if lists:
                for half in range(2):
                    x = carry[half]
                    m = jnp.max(x, axis=0, keepdims=True)
                    hit = x == m
                    v_scr[par, half, pl.ds(r, 1), :] = m
                    r_scr[par, half] = jnp.where(hit, lax.convert_element_type(r, F32), r_scr[par, half])
                    out.append(jnp.where(hit, -jnp.inf, x))
            if pairs:
                x, z = carry[-2], carry[-1]
                m = jnp.max(x, axis=0, keepdims=True)
                out += [jnp.where(x == m, -jnp.inf, x), z + jnp.exp(m - top)]
            return tuple(out)

        res = lax.fori_loop(0, k, body, tuple(init))
        if pending is not None:
            finish(h - 1, 1 - par, res[-2], res[-1])
        if lists:
            v1, v2 = v_scr[par, 0], v_scr[par, 1]
            pending = (_pair_candidates(v1, v2), v1[0:1] + v2[0:1])
        else:
            pending = None


def _peer_select(pqT, k1, k2):
    _, n = pqT.shape
    tn = LANES
    nt = n // tn
    tab = jax.ShapeDtypeStruct((PEER_HEADS, N_KEYS, n), F32)
    tab_spec = pl.BlockSpec((PEER_HEADS, N_KEYS, tn), lambda i: (0, 0, i))
    tabb = jax.ShapeDtypeStruct((PEER_HEADS, N_KEYS // BF16_ROWS, BF16_ROWS, n), BF16)
    tabb_spec = pl.BlockSpec((PEER_HEADS, N_KEYS // BF16_ROWS, BF16_ROWS, tn), lambda i: (0, 0, 0, i))
    list_scr = pltpu.VMEM((PEER_TOPK, tn), F32)
    return pl.pallas_call(
        _peer_select_kernel,
        grid=(nt,),
        in_specs=[pl.BlockSpec((pqT.shape[0], tn), lambda i: (0, i)),
                  pl.BlockSpec(k1.shape, lambda i: (0, 0)),
                  pl.BlockSpec(k2.shape, lambda i: (0, 0))],
        out_specs=(tab_spec, tab_spec, tabb_spec, tabb_spec),
        out_shape=(tab, tab, tabb, tabb),
        scratch_shapes=[pltpu.VMEM((2, 2, PEER_TOPK, tn), F32), pltpu.VMEM((2, 2, N_KEYS, tn), F32),
                        list_scr, list_scr],
        compiler_params=_cparams("arbitrary"),
        name="peer_select",
    )(pqT, k1, k2)


def _peer_dense_kernel(alpha, i1_per_step, h2T_ref, u_ref, vT_ref, cnt1_ref, w1_ref, rank2_ref, e2_ref,
                       x1_ref, g2_ref, lng_ref, lnb_ref, y_ref, acc_scr, aT_scr, act_scr):
    j = pl.program_id(1)

    @pl.when(j == 0)
    def _():
        acc_scr[...] = jnp.zeros(acc_scr.shape, F32)

    tn = aT_scr.shape[1]
    tiles = (N_KEYS // BF16_ROWS, BF16_ROWS, tn)
    zero = jnp.zeros(tiles, BF16)
    for ii in range(i1_per_step):
        i1 = j * i1_per_step + ii
        gate = zero
        for h in range(PEER_HEADS):
            c1 = jnp.broadcast_to(cnt1_ref[h, pl.ds(i1, 1), :], (BF16_ROWS, tn)).astype(BF16)
            w1 = jnp.broadcast_to(w1_ref[h, pl.ds(i1, 1), :], (BF16_ROWS, tn)).astype(BF16)
            gate = gate + jnp.where(rank2_ref[h] < c1[None], e2_ref[h], zero) * w1[None]
        act_scr[ii * N_KEYS:(ii + 1) * N_KEYS, :] = gate.reshape(N_KEYS, tn)
    aT_scr[...] = _dot(u_ref[...], h2T_ref[...])
    for ii in range(i1_per_step):
        rows = slice(ii * N_KEYS, (ii + 1) * N_KEYS)
        a = aT_scr[rows, :].astype(BF16)
        t = jnp.tanh(a * (_GELU_C1 + _GELU_C2 * (a * a)))
        act_scr[rows, :] = (a + a * t) * act_scr[rows, :]
    acc_scr[...] += _dot(vT_ref[...], act_scr[...])

    @pl.when(j == pl.num_programs(1) - 1)
    def _():
        peer = acc_scr[...].T
        y_ref[...] = _layer_norm(alpha * x1_ref[...] + g2_ref[0] * peer, lng_ref[...], lnb_ref[...])


def _peer_dense(alpha, h2T, u_b, vT_b, tabs, x1, g2, ln_g, ln_b, rows_per_vec):
    d, n = h2T.shape
    tn = min(512, rows_per_vec) if g2.shape[1] == 1 else g2.shape[1]
    tiles_per_vec = rows_per_vec // tn
    i1_per_step = 8
    te = i1_per_step * N_KEYS
    n_exp = u_b.shape[0]
    tab_spec = pl.BlockSpec((PEER_HEADS, N_KEYS, tn), lambda i, j: (0, 0, i))
    tabb_spec = pl.BlockSpec((PEER_HEADS, N_KEYS // BF16_ROWS, BF16_ROWS, tn), lambda i, j: (0, 0, 0, i))
    return pl.pallas_call(
        functools.partial(_peer_dense_kernel, alpha, i1_per_step),
        grid=(n // tn, n_exp // te),
        in_specs=[pl.BlockSpec((d, tn), lambda i, j: (0, i)),
                  pl.BlockSpec((te, d), lambda i, j: (j, 0)),
                  pl.BlockSpec((d, te), lambda i, j: (0, j)),
                  tab_spec, tab_spec, tabb_spec, tabb_spec,
                  pl.BlockSpec((tn, d), lambda i, j: (i, 0)),
                  pl.BlockSpec((1,) + g2.shape[1:], lambda i, j: (i // tiles_per_vec, 0, 0)),
                  pl.BlockSpec(ln_g.shape, lambda i, j: (0, 0)),
                  pl.BlockSpec(ln_b.shape, lambda i, j: (0, 0))],
        out_specs=pl.BlockSpec((tn, d), lambda i, j: (i, 0)),
        out_shape=jax.ShapeDtypeStruct((n, d), F32),
        scratch_shapes=[pltpu.VMEM((d, tn), F32), pltpu.VMEM((te, tn), F32), pltpu.VMEM((te, tn), BF16)],
        compiler_params=_cparams("arbitrary", "arbitrary"),
        name="peer_dense_ln2",
    )(h2T, u_b, vT_b, *tabs, x1, g2, ln_g, ln_b)


def _pack_w_in(w):
    widths = (ATTN_WIDTH, KV_WIDTH, KV_WIDTH, N_IDX_HEADS * IDX_DIM, N_IDX_HEADS, IDX_DIM,
              D_CONV, D_CONV, D_CONV, w.shape[0], w.shape[0])
    parts, start = [], 0
    for wd in widths:
        parts.append(w[:, start:start + wd])
        start += wd
    q, k, v, qi, wi, ki, bg, cg, xin, ga, gb = parts
    pad = jnp.zeros((w.shape[0], _KIWI - IDX_DIM - N_IDX_HEADS), w.dtype)
    return jnp.concatenate([q, k, v, qi, ki, wi, pad, bg, cg, xin, ga, gb], axis=1).astype(BF16)


def kernel(x_prompt, x_sample, c_prompt, c_sample, cache_k, cache_v, cache_kidx, state_conv, page_table,
           rel_bias, w_ada, b_ada, w_in, conv_w, conv_b, w_o_attn, w_o_conv, w_out, ln1_g, ln1_b,
           ln2_g, ln2_b, peer_wq, peer_k1, peer_k2, peer_u, peer_v):
    depth = w_in.shape[0]
    alpha = (2 * depth) ** 0.25
    bp, sp, d = x_prompt.shape
    bs, ts, _ = x_sample.shape
    n_pool = cache_k.shape[1]
    past = page_table.shape[1] * PAGE_SIZE
    ns = bs * ts

    xp, xs = x_prompt, x_sample.reshape(ns, d)
    outs = [[] for _ in range(8)]
    for l in range(depth):
        c_all = jnp.concatenate([c_prompt, c_sample], axis=0)
        c_all = jnp.pad(c_all, ((0, (-c_all.shape[0]) % 16), (0, 0)))
        mod = _ada(c_all, w_ada[l], b_ada[l])[:bp + bs]
        sh1, sc1, g1, sh2, sc2, g2 = jnp.split(mod, 6, axis=-1)
        as_p = lambda a: a[:bp].reshape(bp, 1, d)
        as_s = lambda a: jnp.repeat(a[bp:], ts, axis=0).reshape(1, ns, d)

        w_all = _pack_w_in(w_in[l])
        woa, woc, wout = w_o_attn[l].astype(BF16), w_o_conv[l].astype(BF16), w_out[l].astype(BF16)
        wq = peer_wq[l].astype(BF16)
        k1, k2 = peer_k1[l].astype(BF16), peer_k2[l].astype(BF16)
        u_b = peer_u[l].astype(BF16)
        vT_b = peer_v[l].T.astype(BF16)
        lg1, lb1 = ln1_g[l].reshape(1, d), ln1_b[l].reshape(1, d)
        lg2, lb2 = ln2_g[l].reshape(1, d), ln2_b[l].reshape(1, d)

        (qT, k_p, v_p, kb_p, vTb_p, qiT, ki_p, kib_p, wiT, oconv_p, ga_p, gb_p, ulast_p) = _inproj_prompt(
            xp, as_p(sc1), as_p(sh1), w_all, conv_w[l], conv_b[l])
        oattn_p = _attn_prompt(rel_bias, qT, qiT, wiT, kb_p, vTb_p, kib_p)
        x1_p, h2T_p, pqT_p = _merge(alpha, xp.reshape(bp * sp, d), oattn_p.reshape(bp * sp, ATTN_WIDTH),
                                    oconv_p.reshape(bp * sp, D_CONV), ga_p.reshape(bp * sp, d),
                                    gb_p.reshape(bp * sp, d), as_p(g1), as_p(sc2), as_p(sh2),
                                    woa, woc, wout, wq, lg1, lb1, sp)
        tabs_p = _peer_select(pqT_p, k1, k2)
        y_p = _peer_dense(alpha, h2T_p, u_b, vT_b, tabs_p, x1_p, as_p(g2), lg2, lb2, sp)
        xp = y_p.reshape(bp, sp, d)

        st = state_conv[l]
        st0 = jnp.repeat(st[:, 0], ts, axis=0)
        st1 = jnp.repeat(st[:, 1], ts, axis=0)
        (q_s, k_s, v_s, qi_s, kiwi_s, oconv_s, ga_s, gb_s, u_s) = _inproj_sample(
            xs, as_s(sc1)[0], as_s(sh1)[0], w_all, conv_w[l], conv_b[l], st0, st1, ts)
        ki_s = kiwi_s[:, :IDX_DIM]
        wi_s = kiwi_s[:, IDX_DIM:IDX_DIM + N_IDX_HEADS]
        q5 = q_s.reshape(bs, ts, N_KV_HEADS, GROUP, HEAD_DIM).transpose(0, 2, 3, 1, 4)
        qpad = jnp.zeros((bs, N_KV_HEADS, GROUP * ts, N_KV_HEADS, HEAD_DIM), F32)
        for g in range(N_KV_HEADS):
            qpad = qpad.at[:, g, :, g, :].set(q5[:, g].reshape(bs, GROUP * ts, HEAD_DIM))
        qpad = qpad.reshape(bs, N_HEADS * ts, KV_WIDTH).astype(BF16)
        qi_r = qi_s.reshape(bs, ts, N_IDX_HEADS, IDX_DIM).transpose(0, 2, 1, 3).reshape(
            bs, N_IDX_HEADS * ts, IDX_DIM).astype(BF16)
        wi_r = jnp.broadcast_to(wi_s.reshape(bs, ts, N_IDX_HEADS).transpose(0, 2, 1).reshape(
            bs, N_IDX_HEADS * ts, 1), (bs, N_IDX_HEADS * ts, LANES))
        o_s = _attn_sample(page_table, rel_bias, qpad, qi_r, wi_r,
                           k_s.reshape(bs, ts, KV_WIDTH), v_s.reshape(bs, ts, KV_WIDTH),
                           ki_s.reshape(bs, ts, IDX_DIM),
                           cache_k[l].reshape(n_pool, PAGE_SIZE, KV_WIDTH),
                           cache_v[l].reshape(n_pool, PAGE_SIZE, KV_WIDTH), cache_kidx[l], past)
        o5 = o_s.reshape(bs, N_KV_HEADS, GROUP, ts, N_KV_HEADS, HEAD_DIM)
        oattn_s = jnp.stack([o5[:, g, :, :, g, :] for g in range(N_KV_HEADS)], axis=1)
        oattn_s = oattn_s.transpose(0, 3, 1, 2, 4).reshape(ns, ATTN_WIDTH).astype(BF16)
        x1_s, h2T_s, pqT_s = _merge(alpha, xs, oattn_s, oconv_s, ga_s, gb_s, as_s(g1), as_s(sc2), as_s(sh2),
                                    woa, woc, wout, wq, lg1, lb1, ns)
        tabs_s = _peer_select(pqT_s, k1, k2)
        xs = _peer_dense(alpha, h2T_s, u_b, vT_b, tabs_s, x1_s, as_s(g2), lg2, lb2, ns)

        for lst, val in zip(outs, (
                k_p.reshape(bp, sp, N_KV_HEADS, HEAD_DIM), v_p.reshape(bp, sp, N_KV_HEADS, HEAD_DIM), ki_p,
                ulast_p[:, 8 - (CONV_WIDTH - 1):],
                k_s.reshape(bs, ts, N_KV_HEADS, HEAD_DIM), v_s.reshape(bs, ts, N_KV_HEADS, HEAD_DIM),
                ki_s.reshape(bs, ts, IDX_DIM),
                u_s.reshape(bs, ts, D_CONV)[:, ts - (CONV_WIDTH - 1):])):
            lst.append(val)

    return (xp, xs.reshape(bs, ts, d)) + tuple(jnp.stack(o) for o in outs)
```

```python
import functools
import math

import jax
import jax.numpy as jnp
from jax import lax
from jax.experimental import pallas as pl
from jax.experimental.pallas import tpu as pltpu

F32 = jnp.float32
BF16 = jnp.bfloat16
I32 = jnp.int32

N_HEADS = 8
N_KV_HEADS = 2
HEAD_DIM = 64
GROUP = N_HEADS // N_KV_HEADS
ATTN_WIDTH = N_HEADS * HEAD_DIM
KV_WIDTH = N_KV_HEADS * HEAD_DIM
ATTN_SCALE = HEAD_DIM ** -0.5
N_IDX_HEADS = 4
IDX_DIM = 64
IDX_SCALE = (IDX_DIM * N_IDX_HEADS) ** -0.5
TOPK_MAX = 256
N_BUCKETS = 32
MAX_DISTANCE = 128
D_CONV = 512
CONV_WIDTH = 3
N_KEYS = 128
PEER_HEADS = 8
PEER_HALF = 64
PEER_TOPK = 16
LN_EPS = 1e-5
PAGE_SIZE = 128

LANES = 128
BF16_ROWS = 16
VMEM_LIMIT = 56 * 1024 * 1024

_GELU_C1 = math.sqrt(2.0 / math.pi)
_GELU_C2 = _GELU_C1 * 0.044715

NEG_BIG = -1e30

_KIWI = LANES
_OFF_Q = 0
_OFF_K = _OFF_Q + ATTN_WIDTH
_OFF_V = _OFF_K + KV_WIDTH
_OFF_QI = _OFF_V + KV_WIDTH
_OFF_KIWI = _OFF_QI + N_IDX_HEADS * IDX_DIM
_OFF_BG = _OFF_KIWI + _KIWI
_OFF_CG = _OFF_BG + D_CONV
_OFF_XIN = _OFF_CG + D_CONV
_OFF_GA = _OFF_XIN + D_CONV


def _cparams(*sem):
    return pltpu.CompilerParams(dimension_semantics=sem, vmem_limit_bytes=VMEM_LIMIT)


def _dot(a, b):
    return jnp.dot(a, b, preferred_element_type=F32)


def _dot_nt(a, b):
    return lax.dot_general(a, b, (((1,), (1,)), ((), ())), preferred_element_type=F32)


_COARSE_STEPS = 8


def _kth_largest(stats, count_ge, k, n_valid, v_min, v_max):
    enough = n_valid >= k
    lo0 = jnp.where(enough, v_min, -jnp.inf)
    hi0 = jnp.where(enough, v_max, -jnp.inf)

    def cond(state):
        lo, hi = state
        return jnp.max(jnp.where(lo < hi, 1.0, 0.0)) > 0.0

    def midpoint(lo, hi):
        mid = lo + (hi - lo) * 0.5
        return jnp.where(mid > lo, mid, hi)

    def coarse_step(_, state):
        lo, hi = state
        mid = midpoint(lo, hi)
        cnt = count_ge(mid)
        active = lo < hi
        return (jnp.where(active & (cnt >= k), mid, lo), jnp.where(active & (cnt < k), mid, hi))

    def step(state):
        lo, hi = state
        cnt, above, below = stats(midpoint(lo, hi))
        active = lo < hi
        return (jnp.where(active & (cnt >= k), above, lo), jnp.where(active & (cnt < k), below, hi))

    state = lax.fori_loop(0, _COARSE_STEPS, coarse_step, (lo0, hi0))
    lo, _ = lax.while_loop(cond, lambda state: step(step(state)), state)
    return lo


def _t5_bias(dist, relb_ref, h):
    n = jnp.maximum(dist, 0)
    max_exact = N_BUCKETS // 2
    nf = jnp.maximum(n, 1).astype(F32)
    large = max_exact + jnp.floor(jnp.log(nf / max_exact) / math.log(MAX_DISTANCE / max_exact)
                                  * (N_BUCKETS - max_exact)).astype(I32)
    large = jnp.minimum(large, N_BUCKETS - 1)
    bucket = jnp.where(n < max_exact, n, large)
    out = jnp.zeros(dist.shape, F32)
    for b in range(N_BUCKETS):
        out = jnp.where(bucket == b, relb_ref[b, h], out)
    return out


def _layer_norm(x, g, b):
    mu = jnp.mean(x, axis=-1, keepdims=True)
    var = jnp.mean(jnp.square(x - mu), axis=-1, keepdims=True)
    return (x - mu) * lax.rsqrt(var + LN_EPS) * g + b


def _ada_kernel(c_ref, w_ref, b_ref, o_ref):
    o_ref[...] = _dot(c_ref[...].astype(BF16), w_ref[...].astype(BF16)) + b_ref[...]


def _ada(c, w, b):
    m, d = c.shape
    n = w.shape[1]
    tn = n // 4
    return pl.pallas_call(
        _ada_kernel,
        grid=(n // tn,),
        in_specs=[pl.BlockSpec((m, d), lambda j: (0, 0)),
                  pl.BlockSpec((d, tn), lambda j: (0, j)),
                  pl.BlockSpec((1, tn), lambda j: (0, j))],
        out_specs=pl.BlockSpec((m, tn), lambda j: (0, j)),
        out_shape=jax.ShapeDtypeStruct((m, n), F32),
        compiler_params=_cparams("arbitrary"),
        name="ada_mod",
    )(c, w, b.reshape(1, n))


def _conv_out(bg, u, u1, u2, cw_ref, cb_ref):
    y = cb_ref[...] + u2 * cw_ref[0:1, :] + u1 * cw_ref[1:2, :] + u * cw_ref[2:3, :]
    return bg * y


def _inproj_prompt_kernel(x_ref, sc_ref, sh_ref, w_ref, cw_ref, cb_ref,
                          qT_ref, k_ref, v_ref, kb_ref, vTb_ref, qiT_ref, ki_ref, kib_ref, wiT_ref,
                          oconv_ref, ga_ref, gb_ref, ulast_ref, uprev_scr):
    s = pl.program_id(1)
    tm = x_ref.shape[1]
    h = (x_ref[0] * (1.0 + sc_ref[0]) + sh_ref[0]).astype(BF16)

    def proj(a, width):
        return _dot(h, w_ref[:, a:a + width])

    qT_ref[0] = proj(_OFF_Q, ATTN_WIDTH).T.astype(BF16)
    k = proj(_OFF_K, KV_WIDTH)
    k_ref[0] = k
    kb_ref[0] = k.astype(BF16)
    v = proj(_OFF_V, KV_WIDTH)
    v_ref[0] = v
    for i in range(tm // LANES):
        vTb_ref[0, i] = v[i * LANES:(i + 1) * LANES].T.astype(BF16)
    qiT_ref[0] = proj(_OFF_QI, N_IDX_HEADS * IDX_DIM).T.astype(BF16)
    kiwi = proj(_OFF_KIWI, _KIWI)
    ki_ref[0] = kiwi[:, :IDX_DIM]
    kib_ref[0] = kiwi[:, :IDX_DIM].astype(BF16)
    wiT_ref[0] = kiwi.T[IDX_DIM:IDX_DIM + 8]

    @pl.when(s == 0)
    def _():
        uprev_scr[...] = jnp.zeros(uprev_scr.shape, F32)

    bg = proj(_OFF_BG, D_CONV)
    u = proj(_OFF_CG, D_CONV) * proj(_OFF_XIN, D_CONV)
    row = lax.broadcasted_iota(I32, u.shape, 0)
    up = uprev_scr[...]
    u1 = jnp.where(row < 1, pltpu.roll(up, 1, 0), pltpu.roll(u, 1, 0))
    u2 = jnp.where(row < 2, pltpu.roll(up, 2, 0), pltpu.roll(u, 2, 0))
    oconv_ref[0] = _conv_out(bg, u, u1, u2, cw_ref, cb_ref).astype(BF16)
    uprev_scr[...] = u
    ulast_ref[0] = u[tm - 8:, :]

    ga_ref[0] = proj(_OFF_GA, w_ref.shape[0])
    gb_ref[0] = proj(_OFF_GA + w_ref.shape[0], w_ref.shape[0])


def _inproj_prompt(x, sc, sh, w_all, conv_w, conv_b):
    b, s, d = x.shape
    tm = min(256, s)
    nkb = tm // LANES
    row = lambda width, dt: jax.ShapeDtypeStruct((b, s, width), dt)
    out_shape = (
        jax.ShapeDtypeStruct((b, ATTN_WIDTH, s), BF16),
        row(KV_WIDTH, F32), row(KV_WIDTH, F32), row(KV_WIDTH, BF16),
        jax.ShapeDtypeStruct((b, s // LANES, KV_WIDTH, LANES), BF16),
        jax.ShapeDtypeStruct((b, N_IDX_HEADS * IDX_DIM, s), BF16),
        row(IDX_DIM, F32), row(IDX_DIM, BF16),
        jax.ShapeDtypeStruct((b, 8, s), F32),
        row(D_CONV, BF16),
        row(d, F32), row(d, F32),
        jax.ShapeDtypeStruct((b, 8, D_CONV), F32),
    )
    tile = lambda width: pl.BlockSpec((1, tm, width), lambda i, j: (i, j, 0))
    tileT = lambda rows: pl.BlockSpec((1, rows, tm), lambda i, j: (i, 0, j))
    out_specs = (
        tileT(ATTN_WIDTH), tile(KV_WIDTH), tile(KV_WIDTH), tile(KV_WIDTH),
        pl.BlockSpec((1, nkb, KV_WIDTH, LANES), lambda i, j: (i, j, 0, 0)),
        tileT(N_IDX_HEADS * IDX_DIM), tile(IDX_DIM), tile(IDX_DIM), tileT(8),
        tile(D_CONV), tile(d), tile(d),
        pl.BlockSpec((1, 8, D_CONV), lambda i, j: (i, 0, 0)),
    )
    vec = pl.BlockSpec((1, 1, d), lambda i, j: (i, 0, 0))
    return pl.pallas_call(
        _inproj_prompt_kernel,
        grid=(b, s // tm),
        in_specs=[tile(d), vec, vec,
                  pl.BlockSpec(w_all.shape, lambda i, j: (0, 0)),
                  pl.BlockSpec(conv_w.shape, lambda i, j: (0, 0)),
                  pl.BlockSpec((1, D_CONV), lambda i, j: (0, 0))],
        out_specs=out_specs,
        out_shape=out_shape,
        scratch_shapes=[pltpu.VMEM((tm, D_CONV), F32)],
        compiler_params=_cparams("arbitrary", "arbitrary"),
        name="inproj_prompt",
    )(x, sc, sh, w_all, conv_w, conv_b.reshape(1, D_CONV))


def _inproj_sample_kernel(t_seq, x_ref, sc_ref, sh_ref, w_ref, cw_ref, cb_ref, s0_ref, s1_ref,
                          q_ref, k_ref, v_ref, qi_ref, kiwi_ref, oconv_ref, ga_ref, gb_ref, u_ref):
    h = (x_ref[...] * (1.0 + sc_ref[...]) + sh_ref[...]).astype(BF16)

    def proj(a, width):
        return _dot(h, w_ref[:, a:a + width])

    q_ref[...] = proj(_OFF_Q, ATTN_WIDTH)
    k_ref[...] = proj(_OFF_K, KV_WIDTH)
    v_ref[...] = proj(_OFF_V, KV_WIDTH)
    qi_ref[...] = proj(_OFF_QI, N_IDX_HEADS * IDX_DIM)
    kiwi_ref[...] = proj(_OFF_KIWI, _KIWI)
    bg = proj(_OFF_BG, D_CONV)
    u = proj(_OFF_CG, D_CONV) * proj(_OFF_XIN, D_CONV)
    t = lax.broadcasted_iota(I32, u.shape, 0) % t_seq
    u1 = jnp.where(t == 0, s1_ref[...], pltpu.roll(u, 1, 0))
    u2 = jnp.where(t == 0, s0_ref[...], jnp.where(t == 1, s1_ref[...], pltpu.roll(u, 2, 0)))
    oconv_ref[...] = _conv_out(bg, u, u1, u2, cw_ref, cb_ref).astype(BF16)
    u_ref[...] = u
    ga_ref[...] = proj(_OFF_GA, w_ref.shape[0])
    gb_ref[...] = proj(_OFF_GA + w_ref.shape[0], w_ref.shape[0])


def _inproj_sample(x, sc_rows, sh_rows, w_all, conv_w, conv_b, st0_rows, st1_rows, t_seq):
    n, d = x.shape
    full = lambda a: pl.BlockSpec(a.shape, lambda i: (0,) * a.ndim)
    o = lambda width, dt: jax.ShapeDtypeStruct((n, width), dt)
    out_shape = (o(ATTN_WIDTH, F32), o(KV_WIDTH, F32), o(KV_WIDTH, F32), o(N_IDX_HEADS * IDX_DIM, F32),
                 o(_KIWI, F32), o(D_CONV, BF16), o(d, F32), o(d, F32), o(D_CONV, F32))
    args = (x, sc_rows, sh_rows, w_all, conv_w, conv_b.reshape(1, D_CONV), st0_rows, st1_rows)
    return pl.pallas_call(
        functools.partial(_inproj_sample_kernel, t_seq),
        grid=(1,),
        in_specs=[full(a) for a in args],
        out_specs=tuple(pl.BlockSpec(s.shape, lambda i: (0, 0)) for s in out_shape),
        out_shape=out_shape,
        compiler_params=_cparams("arbitrary"),
        name="inproj_sample",
    )(*args)


def _attn_prompt_kernel(n_sel, relb_ref, qT_ref, qiT_ref, wiT_ref, kb_ref, vTb_ref, kib_ref,
                        o_ref, bias_scr, key_scr):
    qb = pl.program_id(1)
    blk = LANES
    kblk = 2 * blk
    ntrip = qb // 2 + 1

    @pl.when((pl.program_id(0) == 0) & (qb == 0))
    def _():
        j = lax.broadcasted_iota(I32, (blk, blk), 0)
        i = lax.broadcasted_iota(I32, (blk, blk), 1)
        for delta in range(3):
            dist = i - j + blk * delta
            for h in range(N_HEADS):
                g, r = divmod(h, GROUP)
                bias_scr[delta, g, :, r * blk:(r + 1) * blk] = _t5_bias(dist, relb_ref, h)

    row = lax.broadcasted_iota(I32, (kblk, blk), 0)
    lane = lax.broadcasted_iota(I32, (kblk, blk), 1)

    def valid_mask(t):
        return (t * kblk + row) <= (qb * blk + lane)

    def key_rows(t):
        return pl.ds(pl.multiple_of(t * kblk, kblk), kblk)

    qiT = jnp.concatenate([qiT_ref[0, h * IDX_DIM:(h + 1) * IDX_DIM, :] for h in range(N_IDX_HEADS)], axis=1)
    wi = [wiT_ref[0, h:h + 1, :] for h in range(N_IDX_HEADS)]
    fold = lambda a: a.reshape(kblk // 8, 8, blk)
    tall = lambda a: jnp.sum(a, axis=0, keepdims=True)
    part0 = (jnp.zeros((8, blk), F32), jnp.full((8, blk), jnp.inf, F32), jnp.full((8, blk), -jnp.inf, F32))

    def score_blk(t, carry):
        cnt, lo, hi = carry
        d = _dot(kib_ref[0, key_rows(t), :], qiT)
        s = jnp.maximum(d[:, 0:blk], 0.0) * wi[0]
        for h in range(1, N_IDX_HEADS):
            s = s + jnp.maximum(d[:, h * blk:(h + 1) * blk], 0.0) * wi[h]
        s = s * IDX_SCALE
        valid = valid_mask(t)
        key_scr[key_rows(t), :] = jnp.where(valid, s, -jnp.inf)
        return (cnt + jnp.sum(fold(jnp.where(valid, 1.0, 0.0)), axis=0),
                jnp.minimum(lo, jnp.min(fold(jnp.where(valid, s, jnp.inf)), axis=0)),
                jnp.maximum(hi, jnp.max(fold(jnp.where(valid, s, -jnp.inf)), axis=0)))

    cnt, lo, hi = lax.fori_loop(0, ntrip, score_blk, part0)
    n_valid = tall(cnt)
    v_min = jnp.min(lo, axis=0, keepdims=True)
    v_max = jnp.max(hi, axis=0, keepdims=True)

    def stats(mid):
        def body(t, carry):
            cnt, above, below = carry
            x = key_scr[key_rows(t), :]
            ge = x >= mid
            return (cnt + jnp.sum(fold(jnp.where(ge, 1.0, 0.0)), axis=0),
                    jnp.minimum(above, jnp.min(fold(jnp.where(ge, x, jnp.inf)), axis=0)),
                    jnp.maximum(below, jnp.max(fold(jnp.where(ge, -jnp.inf, x)), axis=0)))
        cnt, above, below = lax.fori_loop(0, ntrip, body, part0)
        return tall(cnt), jnp.min(above, axis=0, keepdims=True), jnp.max(below, axis=0, keepdims=True)

    def count(pred):
        def body(t, acc):
            return acc + jnp.sum(fold(jnp.where(pred(key_scr[key_rows(t), :]), 1.0, 0.0)), axis=0)
        return tall(lax.fori_loop(0, ntrip, body, jnp.zeros((8, blk), F32)))

    thr = _kth_largest(stats, lambda mid: count(lambda x: x >= mid), float(n_sel), n_valid, v_min, v_max)

    need = n_sel - count(lambda x: x > thr)
    ltri = (lax.broadcasted_iota(I32, (kblk, kblk), 1) <= lax.broadcasted_iota(I32, (kblk, kblk), 0)).astype(BF16)

    def select_mask(t, ties_before):
        kv = key_scr[key_rows(t), :]
        eq = kv == thr
        prefix = _dot(ltri, eq.astype(BF16)) + ties_before
        sel = ((kv > thr) | (eq & (prefix <= need))) & valid_mask(t)
        return jnp.where(sel, 0.0, NEG_BIG), prefix[kblk - 1:kblk, :]

    zero = jnp.zeros((HEAD_DIM, blk), BF16)
    scale = jnp.asarray(ATTN_SCALE, BF16)
    qTg = []
    for g in range(N_KV_HEADS):
        cols = []
        for r in range(GROUP):
            base = g * GROUP * HEAD_DIM + r * HEAD_DIM
            parts = [zero] * N_KV_HEADS
            parts[g] = qT_ref[0, base:base + HEAD_DIM, :] * scale
            cols.append(jnp.concatenate(parts, axis=0))
        qTg.append(jnp.concatenate(cols, axis=1))
    width = GROUP * blk

    def att_blk(t, carry):
        kmat = kb_ref[0, key_rows(t), :]
        mask1, ties = select_mask(t, carry[N_KV_HEADS])
        mask = jnp.concatenate([mask1] * GROUP, axis=1)
        d0 = jnp.clip(qb - 2 * t, 0, 2)
        d1 = jnp.clip(qb - 2 * t - 1, 0, 2)
        new = []
        for g in range(N_KV_HEADS):
            m, l, acc = carry[g]
            bias = jnp.concatenate([bias_scr[d0, g], bias_scr[d1, g]], axis=0)
            lg = _dot(kmat, qTg[g]) + bias + mask
            m_new = jnp.maximum(m, jnp.max(lg, axis=0, keepdims=True))
            p = jnp.exp(lg - m_new)
            alpha = jnp.exp(m - m_new)
            l = alpha * l + jnp.sum(p, axis=0, keepdims=True)
            rows = slice(g * HEAD_DIM, (g + 1) * HEAD_DIM)
            vt = jnp.concatenate([vTb_ref[0, 2 * t, rows, :], vTb_ref[0, 2 * t + 1, rows, :]], axis=1)
            acc = alpha * acc + _dot(vt, p.astype(BF16))
            new.append((m_new, l, acc))
        return tuple(new) + (ties,)

    init = (jnp.full((1, width), NEG_BIG, F32), jnp.zeros((1, width), F32), jnp.zeros((HEAD_DIM, width), F32))
    res = lax.fori_loop(0, ntrip, att_blk, (init,) * N_KV_HEADS + (jnp.zeros((1, blk), F32),))
    outs = []
    for g in range(N_KV_HEADS):
        _, l, acc = res[g]
        og = acc / l
        outs += [og[:, r * blk:(r + 1) * blk] for r in range(GROUP)]
    o_ref[0] = jnp.concatenate(outs, axis=0).T.astype(BF16)


def _attn_prompt(rel_bias, qT, qiT, wiT, kb, vTb, kib):
    b, _, s = qT.shape
    n_sel = min(TOPK_MAX, s // 4)
    blk = LANES
    per_b = lambda a: pl.BlockSpec((1,) + a.shape[1:], lambda i, j: (i,) + (0,) * (a.ndim - 1))
    qtile = lambda rows: pl.BlockSpec((1, rows, blk), lambda i, j: (i, 0, j))
    return pl.pallas_call(
        functools.partial(_attn_prompt_kernel, n_sel),
        grid=(b, s // blk),
        in_specs=[pl.BlockSpec(memory_space=pltpu.SMEM),
                  qtile(ATTN_WIDTH), qtile(N_IDX_HEADS * IDX_DIM), qtile(8),
                  per_b(kb), per_b(vTb), per_b(kib)],
        out_specs=pl.BlockSpec((1, blk, ATTN_WIDTH), lambda i, j: (i, j, 0)),
        out_shape=jax.ShapeDtypeStruct((b, s, ATTN_WIDTH), BF16),
        scratch_shapes=[pltpu.VMEM((3, N_KV_HEADS, blk, GROUP * blk), F32),
                        pltpu.VMEM((s, blk), F32)],
        compiler_params=_cparams("arbitrary", "arbitrary"),
        name="attn_prompt",
    )(rel_bias, qT, qiT, wiT, kb, vTb, kib)


def _attn_sample_kernel(n_sel, past, t_seq, pt_ref, relb_ref, qpad_ref, qi_ref, wi_ref,
                        knew_ref, vnew_ref, kinew_ref, ck_hbm, cv_hbm, cki_hbm,
                        o_ref, kbuf, vbuf, kibuf, sem):
    b = pl.program_id(0)
    nb = pl.num_programs(0)
    n_pages = past // PAGE_SIZE
    lp = kbuf.shape[1]
    blk = LANES
    nblk = lp // blk
    slot = b % 2

    def page_copies(seq, sl, p):
        phys = pt_ref[seq, p]
        rows = pl.ds(pl.multiple_of(p * PAGE_SIZE, PAGE_SIZE), PAGE_SIZE)
        return (pltpu.make_async_copy(ck_hbm.at[phys], kbuf.at[sl, rows, :], sem.at[sl, 0]),
                pltpu.make_async_copy(cv_hbm.at[phys], vbuf.at[sl, rows, :], sem.at[sl, 1]),
                pltpu.make_async_copy(cki_hbm.at[phys], kibuf.at[sl, rows, :], sem.at[sl, 2]))

    def start_all(seq, sl):
        def body(p, c):
            for cp in page_copies(seq, sl, p):
                cp.start()
            return c
        lax.fori_loop(0, n_pages, body, 0)

    def wait_all(seq, sl):
        def body(p, c):
            for cp in page_copies(seq, sl, p):
                cp.wait()
            return c
        lax.fori_loop(0, n_pages, body, 0)

    @pl.when(b == 0)
    def _():
        for sl in range(2):
            kbuf[sl, past:, :] = jnp.zeros((blk, KV_WIDTH), F32)
            vbuf[sl, past:, :] = jnp.zeros((blk, KV_WIDTH), F32)
            kibuf[sl, past:, :] = jnp.zeros((blk, IDX_DIM), F32)
        start_all(0, 0)

    @pl.when(b + 1 < nb)
    def _():
        start_all(b + 1, 1 - slot)

    kbuf[slot, past:past + t_seq, :] = knew_ref[0]
    vbuf[slot, past:past + t_seq, :] = vnew_ref[0]
    kibuf[slot, past:past + t_seq, :] = kinew_ref[0]
    wait_all(b, slot)

    d = _dot_nt(qi_ref[0], kibuf[slot].astype(BF16))
    s = jnp.maximum(d[0:t_seq], 0.0) * wi_ref[0, 0:t_seq, 0:1]
    for h in range(1, N_IDX_HEADS):
        s = s + jnp.maximum(d[h * t_seq:(h + 1) * t_seq], 0.0) * wi_ref[0, h * t_seq:(h + 1) * t_seq, 0:1]
    kpos = lax.broadcasted_iota(I32, (t_seq, lp), 1)
    qpos = past + lax.broadcasted_iota(I32, (t_seq, lp), 0)
    valid = kpos <= qpos
    s = s * IDX_SCALE
    key = jnp.where(valid, s, -jnp.inf)

    def count(pred):
        return jnp.sum(jnp.where(pred, 1.0, 0.0), axis=1, keepdims=True)

    def stats(mid):
        ge = key >= mid
        return (count(ge), jnp.min(jnp.where(ge, key, jnp.inf), axis=1, keepdims=True),
                jnp.max(jnp.where(ge, -jnp.inf, key), axis=1, keepdims=True))

    thr = _kth_largest(stats, lambda mid: count(key >= mid), float(n_sel), count(valid),
                       jnp.min(jnp.where(valid, s, jnp.inf), axis=1, keepdims=True),
                       jnp.max(key, axis=1, keepdims=True))
    need = n_sel - count(key > thr)
    eq = key == thr
    eqf = jnp.where(eq, 1.0, 0.0)
    utri = (lax.broadcasted_iota(I32, (blk, blk), 0) <= lax.broadcasted_iota(I32, (blk, blk), 1)).astype(BF16)
    carry = jnp.zeros((t_seq, 1), F32)
    ranks = []
    for kb in range(nblk):
        e = eqf[:, kb * blk:(kb + 1) * blk]
        ranks.append(_dot(e.astype(BF16), utri) + carry)
        carry = carry + jnp.sum(e, axis=1, keepdims=True)
    prefix = jnp.concatenate(ranks, axis=1)
    sel = jnp.where(((key > thr) | (eq & (prefix <= need))) & valid, 1.0, 0.0)

    lg = _dot_nt(qpad_ref[0], kbuf[slot].astype(BF16)) * ATTN_SCALE
    far = lp - 2 * blk
    near_dist = (past + lax.broadcasted_iota(I32, (t_seq, 2 * blk), 0)
                 - (far + lax.broadcasted_iota(I32, (t_seq, 2 * blk), 1)))
    bias_rows = []
    for h in range(N_HEADS):
        far_bias = jnp.full((t_seq, far), relb_ref[N_BUCKETS - 1, h], F32)
        bias_rows.append(jnp.concatenate([far_bias, _t5_bias(near_dist, relb_ref, h)], axis=1))
    lg = lg + jnp.concatenate(bias_rows, axis=0)
    selh = jnp.concatenate([sel] * N_HEADS, axis=0) > 0.0
    m = jnp.max(jnp.where(selh, lg, NEG_BIG), axis=1, keepdims=True)
    p = jnp.where(selh, jnp.exp(lg - m), 0.0)
    l = jnp.sum(p, axis=1, keepdims=True)
    o = _dot(p.astype(BF16), vbuf[slot].astype(BF16))
    o_ref[0] = o / l


def _attn_sample(page_table, rel_bias, qpad, qi_r, wi_r, k_new, v_new, ki_new, cache_k, cache_v, cache_ki,
                 past):
    nb, t_seq, _ = k_new.shape
    n_sel = min(TOPK_MAX, (past + t_seq) // 4)
    lp = past + LANES
    rows = N_HEADS * t_seq
    per_b = lambda a: pl.BlockSpec((1,) + a.shape[1:], lambda i, pt: (i,) + (0,) * (a.ndim - 1))
    hbm = pl.BlockSpec(memory_space=pl.ANY)
    grid_spec = pltpu.PrefetchScalarGridSpec(
        num_scalar_prefetch=1,
        grid=(nb,),
        in_specs=[pl.BlockSpec(memory_space=pltpu.SMEM),
                  per_b(qpad), per_b(qi_r), per_b(wi_r), per_b(k_new), per_b(v_new), per_b(ki_new),
                  hbm, hbm, hbm],
        out_specs=pl.BlockSpec((1, rows, KV_WIDTH), lambda i, pt: (i, 0, 0)),
        scratch_shapes=[pltpu.VMEM((2, lp, KV_WIDTH), F32),
                        pltpu.VMEM((2, lp, KV_WIDTH), F32),
                        pltpu.VMEM((2, lp, IDX_DIM), F32),
                        pltpu.SemaphoreType.DMA((2, 3))],
    )
    return pl.pallas_call(
        functools.partial(_attn_sample_kernel, n_sel, past, t_seq),
        grid_spec=grid_spec,
        out_shape=jax.ShapeDtypeStruct((nb, rows, KV_WIDTH), F32),
        compiler_params=_cparams("arbitrary"),
        name="attn_sample",
    )(page_table, rel_bias, qpad, qi_r, wi_r, k_new, v_new, ki_new, cache_k, cache_v, cache_ki)


def _merge_kernel(alpha, x_ref, oa_ref, oc_ref, ga_ref, gb_ref, g1_ref, sc2_ref, sh2_ref,
                  woa_ref, woc_ref, wout_ref, wq_ref, lng_ref, lnb_ref,
                  x1_ref, h2T_ref, pqT_ref):
    ta = _dot(oa_ref[...], woa_ref[...])
    tc = _dot(oc_ref[...], woc_ref[...])
    merged = jax.nn.sigmoid(ga_ref[...]) * ta + jax.nn.sigmoid(gb_ref[...]) * tc
    out = _dot(merged.astype(BF16), wout_ref[...])
    x1 = _layer_norm(alpha * x_ref[...] + g1_ref[0] * out, lng_ref[...], lnb_ref[...])
    x1_ref[...] = x1
    h2 = x1 * (1.0 + sc2_ref[0]) + sh2_ref[0]
    h2b = h2.astype(BF16)
    h2T_ref[...] = h2.T.astype(BF16)
    pqT_ref[...] = _dot(h2b, wq_ref[...]).T.astype(BF16)


def _merge(alpha, x, oattn, oconv, ga, gb, g1, sc2, sh2, woa, woc, wout, wq, ln_g, ln_b, rows_per_vec):
    n, d = x.shape
    tm = min(512, rows_per_vec) if g1.shape[1] == 1 else g1.shape[1]
    tiles_per_vec = rows_per_vec // tm
    tile = lambda width: pl.BlockSpec((tm, width), lambda i: (i, 0))
    vec = pl.BlockSpec((1,) + g1.shape[1:], lambda i: (i // tiles_per_vec, 0, 0))
    full = lambda a: pl.BlockSpec(a.shape, lambda i: (0,) * a.ndim)
    pq_w = wq.shape[1]
    return pl.pallas_call(
        functools.partial(_merge_kernel, alpha),
        grid=(n // tm,),
        in_specs=[tile(d), tile(ATTN_WIDTH), tile(D_CONV), tile(d), tile(d), vec, vec, vec,
                  full(woa), full(woc), full(wout), full(wq), full(ln_g), full(ln_b)],
        out_specs=(tile(d), pl.BlockSpec((d, tm), lambda i: (0, i)), pl.BlockSpec((pq_w, tm), lambda i: (0, i))),
        out_shape=(jax.ShapeDtypeStruct((n, d), F32), jax.ShapeDtypeStruct((d, n), BF16),
                   jax.ShapeDtypeStruct((pq_w, n), BF16)),
        compiler_params=_cparams("arbitrary"),
        name="merge_ln1_peerq",
    )(x, oattn, oconv, ga, gb, g1, sc2, sh2, woa, woc, wout, wq, ln_g, ln_b)


def _top16(s, v_scr):
    iota = lax.broadcasted_iota(I32, s.shape, 0)

    def body(r, carry):
        x, rank = carry
        m = jnp.max(x, axis=0, keepdims=True)
        idx = jnp.min(jnp.where(x == m, iota, N_KEYS), axis=0, keepdims=True)
        hit = iota == idx
        v_scr[pl.ds(r, 1), :] = m
        return jnp.where(hit, -jnp.inf, x), jnp.where(hit, lax.convert_element_type(r, F32), rank)

    _, rank = lax.fori_loop(0, PEER_TOPK, body, (s, jnp.full(s.shape, float(PEER_TOPK), F32)))
    return rank


def _pair_candidates(v1, v2):
    return jnp.concatenate([v1[0:1] + v2] + [v1[a:a + 1] + v2[0:8] for a in range(1, 8)]
                           + [v1[8:16] + v2[0:1]], axis=0)


def _peer_select_kernel(pqT_ref, k1_ref, k2_ref, cnt1_ref, w1_ref, rank2_ref, e2_ref,
                        v_scr, r_scr, vx1_scr, vx2_scr):
    tn = pqT_ref.shape[1]
    k = PEER_TOPK
    tiles = (N_KEYS // BF16_ROWS, BF16_ROWS, tn)

    def scores(h):
        base = h * 2 * PEER_HALF
        return (_dot(k1_ref[...], pqT_ref[base:base + PEER_HALF, :]),
                _dot(k2_ref[...], pqT_ref[base + PEER_HALF:base + 2 * PEER_HALF, :]))

    def emit(h, s1, s2, rank1, rank2, cnt_rows, z, top1, top2):
        cnt1 = jnp.zeros(s1.shape, F32)
        for a in range(k):
            cnt1 = jnp.where(rank1 == float(a), cnt_rows[a], cnt1)
        cnt1_ref[h] = cnt1
        w1_ref[h] = jnp.exp(s1 - top1) * (0.5 / z)
        rank2_ref[h] = rank2.astype(BF16).reshape(tiles)
        e2_ref[h] = jnp.exp(s2 - top2).astype(BF16).reshape(tiles)

    def finish(h, par, x, z):
        s1, s2 = scores(h)
        rank1, rank2 = r_scr[par, 0], r_scr[par, 1]
        taken = jnp.where(x == -jnp.inf, 1.0, 0.0)
        cnt_rows = ([jnp.sum(taken[0:k], axis=0, keepdims=True)]
                    + [jnp.sum(taken[k + 8 * (a - 1):k + 8 * a], axis=0, keepdims=True) for a in range(1, 8)]
                    + [taken[k + 56 + a:k + 57 + a] for a in range(8)])
        n_pairs = cnt_rows[0]
        for row in cnt_rows[1:]:
            n_pairs = n_pairs + row
        n1 = jnp.sum(jnp.where(rank1 < float(k), 1.0, 0.0), axis=0, keepdims=True)
        n2 = jnp.sum(jnp.where(rank2 < float(k), 1.0, 0.0), axis=0, keepdims=True)
        tied = (n1 != float(k)) | (n2 != float(k)) | (n_pairs != float(k))
        emit(h, s1, s2, rank1, rank2, cnt_rows, z, v_scr[par, 0, 0:1, :], v_scr[par, 1, 0:1, :])

        @pl.when(jnp.sum(jnp.where(tied, 1.0, 0.0)) > 0.0)
        def _():
            rank1 = _top16(s1, vx1_scr)
            rank2 = _top16(s2, vx2_scr)
            v1, v2 = vx1_scr[...], vx2_scr[...]
            top = v1[0:1] + v2[0:1]
            r16 = lax.broadcasted_iota(I32, (k, tn), 0)
            r8 = lax.broadcasted_iota(I32, (8, tn), 0)
            flat = jnp.concatenate([r16] + [a * k + r8 for a in range(1, 8)] + [(8 + r8) * k], axis=0)

            def pick_exact(r, carry):
                x, cnt, z = carry
                m = jnp.max(x, axis=0, keepdims=True)
                idx = jnp.min(jnp.where(x == m, flat, k * k), axis=0, keepdims=True)
                cnt = cnt + jnp.where(r16 == (idx >> 4), 1.0, 0.0)
                return jnp.where(flat == idx, -jnp.inf, x), cnt, z + jnp.exp(m - top)

            _, cnt, z = lax.fori_loop(0, k, pick_exact, (_pair_candidates(v1, v2), jnp.zeros((k, tn), F32),
                                                          jnp.zeros((1, tn), F32)))
            emit(h, s1, s2, rank1, rank2, [cnt[a:a + 1] for a in range(k)], z, v1[0:1], v2[0:1])

    pending = None
    for h in range(PEER_HEADS + 1):
        par = h % 2
        lists = h < PEER_HEADS
        init = []
        if lists:
            init += list(scores(h))
            for half in range(2):
                r_scr[par, half] = jnp.full((N_KEYS, tn), float(k), F32)
        if pending is not None:
            init += [pending[0], jnp.zeros((1, tn), F32)]
            top = pending[1]

        def body(r, carry, lists=lists, pairs=pending is not None, par=par):
            out = []
            if lists:
                for half in range(2):
                    x = carry[half]
                    m = jnp.max(x, axis=0, keepdims=True)
                    hit = x == m
                    v_scr[par, half, pl.ds(r, 1), :] = m
                    r_scr[par, half] = jnp.where(hit, lax.convert_element_type(r, F32), r_scr[par, half])
                    out.append(jnp.where(hit, -jnp.inf, x))
            if pairs:
                x, z = carry[-2], carry[-1]
                m = jnp.max(x, axis=0, keepdims=True)
                out += [jnp.where(x == m, -jnp.inf, x), z + jnp.exp(m - top)]
            return tuple(out)

        res = lax.fori_loop(0, k, body, tuple(init))
        if pending is not None:
            finish(h - 1, 1 - par, res[-2], res[-1])
        if lists:
            v1, v2 = v_scr[par, 0], v_scr[par, 1]
            pending = (_pair_candidates(v1, v2), v1[0:1] + v2[0:1])
        else:
            pending = None


def _peer_select(pqT, k1, k2):
    _, n = pqT.shape
    tn = LANES
    nt = n // tn
    tab = jax.ShapeDtypeStruct((PEER_HEADS, N_KEYS, n), F32)
    tab_spec = pl.BlockSpec((PEER_HEADS, N_KEYS, tn), lambda i: (0, 0, i))
    tabb = jax.ShapeDtypeStruct((PEER_HEADS, N_KEYS // BF16_ROWS, BF16_ROWS, n), BF16)
    tabb_spec = pl.BlockSpec((PEER_HEADS, N_KEYS // BF16_ROWS, BF16_ROWS, tn), lambda i: (0, 0, 0, i))
    list_scr = pltpu.VMEM((PEER_TOPK, tn), F32)
    return pl.pallas_call(
        _peer_select_kernel,
        grid=(nt,),
        in_specs=[pl.BlockSpec((pqT.shape[0], tn), lambda i: (0, i)),
                  pl.BlockSpec(k1.shape, lambda i: (0, 0)),
                  pl.BlockSpec(k2.shape, lambda i: (0, 0))],
        out_specs=(tab_spec, tab_spec, tabb_spec, tabb_spec),
        out_shape=(tab, tab, tabb, tabb),
        scratch_shapes=[pltpu.VMEM((2, 2, PEER_TOPK, tn), F32), pltpu.VMEM((2, 2, N_KEYS, tn), F32),
                        list_scr, list_scr],
        compiler_params=_cparams("arbitrary"),
        name="peer_select",
    )(pqT, k1, k2)


def _peer_dense_kernel(alpha, i1_per_step, h2T_ref, u_ref, vT_ref, cnt1_ref, w1_ref, rank2_ref, e2_ref,
                       x1_ref, g2_ref, lng_ref, lnb_ref, y_ref, acc_scr, aT_scr, act_scr):
    j = pl.program_id(1)

    @pl.when(j == 0)
    def _():
        acc_scr[...] = jnp.zeros(acc_scr.shape, F32)

    tn = aT_scr.shape[1]
    tiles = (N_KEYS // BF16_ROWS, BF16_ROWS, tn)
    zero = jnp.zeros(tiles, BF16)
    for ii in range(i1_per_step):
        i1 = j * i1_per_step + ii
        gate = zero
        for h in range(PEER_HEADS):
            c1 = jnp.broadcast_to(cnt1_ref[h, pl.ds(i1, 1), :], (BF16_ROWS, tn)).astype(BF16)
            w1 = jnp.broadcast_to(w1_ref[h, pl.ds(i1, 1), :], (BF16_ROWS, tn)).astype(BF16)
            gate = gate + jnp.where(rank2_ref[h] < c1[None], e2_ref[h], zero) * w1[None]
        act_scr[ii * N_KEYS:(ii + 1) * N_KEYS, :] = gate.reshape(N_KEYS, tn)
    aT_scr[...] = _dot(u_ref[...], h2T_ref[...])
    for ii in range(i1_per_step):
        rows = slice(ii * N_KEYS, (ii + 1) * N_KEYS)
        a = aT_scr[rows, :].astype(BF16)
        t = jnp.tanh(a * (_GELU_C1 + _GELU_C2 * (a * a)))
        act_scr[rows, :] = (a + a * t) * act_scr[rows, :]
    acc_scr[...] += _dot(vT_ref[...], act_scr[...])

    @pl.when(j == pl.num_programs(1) - 1)
    def _():
        peer = acc_scr[...].T
        y_ref[...] = _layer_norm(alpha * x1_ref[...] + g2_ref[0] * peer, lng_ref[...], lnb_ref[...])


def _peer_dense(alpha, h2T, u_b, vT_b, tabs, x1, g2, ln_g, ln_b, rows_per_vec):
    d, n = h2T.shape
    tn = min(512, rows_per_vec) if g2.shape[1] == 1 else g2.shape[1]
    tiles_per_vec = rows_per_vec // tn
    i1_per_step = 8
    te = i1_per_step * N_KEYS
    n_exp = u_b.shape[0]
    tab_spec = pl.BlockSpec((PEER_HEADS, N_KEYS, tn), lambda i, j: (0, 0, i))
    tabb_spec = pl.BlockSpec((PEER_HEADS, N_KEYS // BF16_ROWS, BF16_ROWS, tn), lambda i, j: (0, 0, 0, i))
    return pl.pallas_call(
        functools.partial(_peer_dense_kernel, alpha, i1_per_step),
        grid=(n // tn, n_exp // te),
        in_specs=[pl.BlockSpec((d, tn), lambda i, j: (0, i)),
                  pl.BlockSpec((te, d), lambda i, j: (j, 0)),
                  pl.BlockSpec((d, te), lambda i, j: (0, j)),
                  tab_spec, tab_spec, tabb_spec, tabb_spec,
                  pl.BlockSpec((tn, d), lambda i, j: (i, 0)),
                  pl.BlockSpec((1,) + g2.shape[1:], lambda i, j: (i // tiles_per_vec, 0, 0)),
                  pl.BlockSpec(ln_g.shape, lambda i, j: (0, 0)),
                  pl.BlockSpec(ln_b.shape, lambda i, j: (0, 0))],
        out_specs=pl.BlockSpec((tn, d), lambda i, j: (i, 0)),
        out_shape=jax.ShapeDtypeStruct((n, d), F32),
        scratch_shapes=[pltpu.VMEM((d, tn), F32), pltpu.VMEM((te, tn), F32), pltpu.VMEM((te, tn), BF16)],
        compiler_params=_cparams("arbitrary", "arbitrary"),
        name="peer_dense_ln2",
    )(h2T, u_b, vT_b, *tabs, x1, g2, ln_g, ln_b)


def _transpose_cast_kernel(x_ref, o_ref):
    o_ref[...] = x_ref[...].T.astype(BF16)


def _transpose_cast(x):
    r, c = x.shape
    tr = min(512, r)
    return pl.pallas_call(
        _transpose_cast_kernel,
        grid=(r // tr,),
        in_specs=[pl.BlockSpec((tr, c), lambda i: (i, 0))],
        out_specs=pl.BlockSpec((c, tr), lambda i: (0, i)),
        out_shape=jax.ShapeDtypeStruct((c, r), BF16),
        compiler_params=_cparams("arbitrary"),
        name="transpose_cast",
    )(x)


def _pack_w_in(w):
    widths = (ATTN_WIDTH, KV_WIDTH, KV_WIDTH, N_IDX_HEADS * IDX_DIM, N_IDX_HEADS, IDX_DIM,
              D_CONV, D_CONV, D_CONV, w.shape[0], w.shape[0])
    parts, start = [], 0
    for wd in widths:
        parts.append(w[:, start:start + wd])
        start += wd
    q, k, v, qi, wi, ki, bg, cg, xin, ga, gb = parts
    pad = jnp.zeros((w.shape[0], _KIWI - IDX_DIM - N_IDX_HEADS), w.dtype)
    return jnp.concatenate([q, k, v, qi, ki, wi, pad, bg, cg, xin, ga, gb], axis=1).astype(BF16)


def kernel(x_prompt, x_sample, c_prompt, c_sample, cache_k, cache_v, cache_kidx, state_conv, page_table,
           rel_bias, w_ada, b_ada, w_in, conv_w, conv_b, w_o_attn, w_o_conv, w_out, ln1_g, ln1_b,
           ln2_g, ln2_b, peer_wq, peer_k1, peer_k2, peer_u, peer_v):
    depth = w_in.shape[0]
    alpha = (2 * depth) ** 0.25
    bp, sp, d = x_prompt.shape
    bs, ts, _ = x_sample.shape
    n_pool = cache_k.shape[1]
    past = page_table.shape[1] * PAGE_SIZE
    ns = bs * ts

    xp, xs = x_prompt, x_sample.reshape(ns, d)
    outs = [[] for _ in range(8)]
    for l in range(depth):
        c_all = jnp.concatenate([c_prompt, c_sample], axis=0)
        c_all = jnp.pad(c_all, ((0, (-c_all.shape[0]) % 16), (0, 0)))
        mod = _ada(c_all, w_ada[l], b_ada[l])[:bp + bs]
        sh1, sc1, g1, sh2, sc2, g2 = jnp.split(mod, 6, axis=-1)
        as_p = lambda a: a[:bp].reshape(bp, 1, d)
        as_s = lambda a: jnp.repeat(a[bp:], ts, axis=0).reshape(1, ns, d)

        w_all = _pack_w_in(w_in[l])
        woa, woc, wout = w_o_attn[l].astype(BF16), w_o_conv[l].astype(BF16), w_out[l].astype(BF16)
        wq = peer_wq[l].astype(BF16)
        k1, k2 = peer_k1[l].astype(BF16), peer_k2[l].astype(BF16)
        u_b = peer_u[l].astype(BF16)
        vT_b = _transpose_cast(peer_v[l])
        lg1, lb1 = ln1_g[l].reshape(1, d), ln1_b[l].reshape(1, d)
        lg2, lb2 = ln2_g[l].reshape(1, d), ln2_b[l].reshape(1, d)

        (qT, k_p, v_p, kb_p, vTb_p, qiT, ki_p, kib_p, wiT, oconv_p, ga_p, gb_p, ulast_p) = _inproj_prompt(
            xp, as_p(sc1), as_p(sh1), w_all, conv_w[l], conv_b[l])
        oattn_p = _attn_prompt(rel_bias, qT, qiT, wiT, kb_p, vTb_p, kib_p)
        x1_p, h2T_p, pqT_p = _merge(alpha, xp.reshape(bp * sp, d), oattn_p.reshape(bp * sp, ATTN_WIDTH),
                                    oconv_p.reshape(bp * sp, D_CONV), ga_p.reshape(bp * sp, d),
                                    gb_p.reshape(bp * sp, d), as_p(g1), as_p(sc2), as_p(sh2),
                                    woa, woc, wout, wq, lg1, lb1, sp)
        tabs_p = _peer_select(pqT_p, k1, k2)
        y_p = _peer_dense(alpha, h2T_p, u_b, vT_b, tabs_p, x1_p, as_p(g2), lg2, lb2, sp)
        xp = y_p.reshape(bp, sp, d)

        st = state_conv[l]
        st0 = jnp.repeat(st[:, 0], ts, axis=0)
        st1 = jnp.repeat(st[:, 1], ts, axis=0)
        (q_s, k_s, v_s, qi_s, kiwi_s, oconv_s, ga_s, gb_s, u_s) = _inproj_sample(
            xs, as_s(sc1)[0], as_s(sh1)[0], w_all, conv_w[l], conv_b[l], st0, st1, ts)
        ki_s = kiwi_s[:, :IDX_DIM]
        wi_s = kiwi_s[:, IDX_DIM:IDX_DIM + N_IDX_HEADS]
        q5 = q_s.reshape(bs, ts, N_KV_HEADS, GROUP, HEAD_DIM).transpose(0, 2, 3, 1, 4)
        qpad = jnp.zeros((bs, N_KV_HEADS, GROUP * ts, N_KV_HEADS, HEAD_DIM), F32)
        for g in range(N_KV_HEADS):
            qpad = qpad.at[:, g, :, g, :].set(q5[:, g].reshape(bs, GROUP * ts, HEAD_DIM))
        qpad = qpad.reshape(bs, N_HEADS * ts, KV_WIDTH).astype(BF16)
        qi_r = qi_s.reshape(bs, ts, N_IDX_HEADS, IDX_DIM).transpose(0, 2, 1, 3).reshape(
            bs, N_IDX_HEADS * ts, IDX_DIM).astype(BF16)
        wi_r = jnp.broadcast_to(wi_s.reshape(bs, ts, N_IDX_HEADS).transpose(0, 2, 1).reshape(
            bs, N_IDX_HEADS * ts, 1), (bs, N_IDX_HEADS * ts, LANES))
        o_s = _attn_sample(page_table, rel_bias, qpad, qi_r, wi_r,
                           k_s.reshape(bs, ts, KV_WIDTH), v_s.reshape(bs, ts, KV_WIDTH),
                           ki_s.reshape(bs, ts, IDX_DIM),
                           cache_k[l].reshape(n_pool, PAGE_SIZE, KV_WIDTH),
                           cache_v[l].reshape(n_pool, PAGE_SIZE, KV_WIDTH), cache_kidx[l], past)
        o5 = o_s.reshape(bs, N_KV_HEADS, GROUP, ts, N_KV_HEADS, HEAD_DIM)
        oattn_s = jnp.stack([o5[:, g, :, :, g, :] for g in range(N_KV_HEADS)], axis=1)
        oattn_s = oattn_s.transpose(0, 3, 1, 2, 4).reshape(ns, ATTN_WIDTH).astype(BF16)
        x1_s, h2T_s, pqT_s = _merge(alpha, xs, oattn_s, oconv_s, ga_s, gb_s, as_s(g1), as_s(sc2), as_s(sh2),
                                    woa, woc, wout, wq, lg1, lb1, ns)
        tabs_s = _peer_select(pqT_s, k1, k2)
        xs = _peer_dense(alpha, h2T_s, u_b, vT_b, tabs_s, x1_s, as_s(g2), lg2, lb2, ns)

        for lst, val in zip(outs, (
                k_p.reshape(bp, sp, N_KV_HEADS, HEAD_DIM), v_p.reshape(bp, sp, N_KV_HEADS, HEAD_DIM), ki_p,
                ulast_p[:, 8 - (CONV_WIDTH - 1):],
                k_s.reshape(bs, ts, N_KV_HEADS, HEAD_DIM), v_s.reshape(bs, ts, N_KV_HEADS, HEAD_DIM),
                ki_s.reshape(bs, ts, IDX_DIM),
                u_s.reshape(bs, ts, D_CONV)[:, ts - (CONV_WIDTH - 1):])):
            lst.append(val)

    return (xp, xs.reshape(bs, ts, d)) + tuple(jnp.stack(o) for o in outs)
```

```python
import functools
import math

import jax
import jax.numpy as jnp
from jax import lax
from jax.experimental import pallas as pl
from jax.experimental.pallas import tpu as pltpu

F32 = jnp.float32
BF16 = jnp.bfloat16
I32 = jnp.int32

N_HEADS = 8
N_KV_HEADS = 2
HEAD_DIM = 64
GROUP = N_HEADS // N_KV_HEADS
ATTN_WIDTH = N_HEADS * HEAD_DIM
KV_WIDTH = N_KV_HEADS * HEAD_DIM
ATTN_SCALE = HEAD_DIM ** -0.5
N_IDX_HEADS = 4
IDX_DIM = 64
IDX_SCALE = (IDX_DIM * N_IDX_HEADS) ** -0.5
TOPK_MAX = 256
N_BUCKETS = 32
MAX_DISTANCE = 128
D_CONV = 512
CONV_WIDTH = 3
N_KEYS = 128
PEER_HEADS = 8
PEER_HALF = 64
PEER_TOPK = 16
LN_EPS = 1e-5
PAGE_SIZE = 128

LANES = 128
BF16_ROWS = 16
VMEM_LIMIT = 56 * 1024 * 1024

_GELU_C1 = math.sqrt(2.0 / math.pi)
_GELU_C2 = _GELU_C1 * 0.044715

NEG_BIG = -1e30

_KIWI = LANES
_OFF_Q = 0
_OFF_K = _OFF_Q + ATTN_WIDTH
_OFF_V = _OFF_K + KV_WIDTH
_OFF_QI = _OFF_V + KV_WIDTH
_OFF_KIWI = _OFF_QI + N_IDX_HEADS * IDX_DIM
_OFF_BG = _OFF_KIWI + _KIWI
_OFF_CG = _OFF_BG + D_CONV
_OFF_XIN = _OFF_CG + D_CONV
_OFF_GA = _OFF_XIN + D_CONV


def _cparams(*sem):
    return pltpu.CompilerParams(dimension_semantics=sem, vmem_limit_bytes=VMEM_LIMIT)


def _dot(a, b):
    return jnp.dot(a, b, preferred_element_type=F32)


def _dot_nt(a, b):
    return lax.dot_general(a, b, (((1,), (1,)), ((), ())), preferred_element_type=F32)


_COARSE_STEPS = 8


def _kth_largest(stats, count_ge, k, n_valid, v_min, v_max):
    enough = n_valid >= k
    lo0 = jnp.where(enough, v_min, -jnp.inf)
    hi0 = jnp.where(enough, v_max, -jnp.inf)

    def cond(state):
        lo, hi = state
        return jnp.max(jnp.where(lo < hi, 1.0, 0.0)) > 0.0

    def midpoint(lo, hi):
        mid = lo + (hi - lo) * 0.5
        return jnp.where(mid > lo, mid, hi)

    def coarse_step(_, state):
        lo, hi = state
        mid = midpoint(lo, hi)
        cnt = count_ge(mid)
        active = lo < hi
        return (jnp.where(active & (cnt >= k), mid, lo), jnp.where(active & (cnt < k), mid, hi))

    def step(state):
        lo, hi = state
        cnt, above, below = stats(midpoint(lo, hi))
        active = lo < hi
        return (jnp.where(active & (cnt >= k), above, lo), jnp.where(active & (cnt < k), below, hi))

    state = lax.fori_loop(0, _COARSE_STEPS, coarse_step, (lo0, hi0))
    lo, _ = lax.while_loop(cond, lambda state: step(step(state)), state)
    return lo


def _t5_bias(dist, relb_ref, h):
    n = jnp.maximum(dist, 0)
    max_exact = N_BUCKETS // 2
    nf = jnp.maximum(n, 1).astype(F32)
    large = max_exact + jnp.floor(jnp.log(nf / max_exact) / math.log(MAX_DISTANCE / max_exact)
                                  * (N_BUCKETS - max_exact)).astype(I32)
    large = jnp.minimum(large, N_BUCKETS - 1)
    bucket = jnp.where(n < max_exact, n, large)
    out = jnp.zeros(dist.shape, F32)
    for b in range(N_BUCKETS):
        out = jnp.where(bucket == b, relb_ref[b, h], out)
    return out


def _layer_norm(x, g, b):
    mu = jnp.mean(x, axis=-1, keepdims=True)
    var = jnp.mean(jnp.square(x - mu), axis=-1, keepdims=True)
    return (x - mu) * lax.rsqrt(var + LN_EPS) * g + b


def _ada_kernel(c_ref, w_ref, b_ref, o_ref):
    o_ref[...] = _dot(c_ref[...].astype(BF16), w_ref[...].astype(BF16)) + b_ref[...]


def _ada(c, w, b):
    m, d = c.shape
    n = w.shape[1]
    tn = n // 4
    return pl.pallas_call(
        _ada_kernel,
        grid=(n // tn,),
        in_specs=[pl.BlockSpec((m, d), lambda j: (0, 0)),
                  pl.BlockSpec((d, tn), lambda j: (0, j)),
                  pl.BlockSpec((1, tn), lambda j: (0, j))],
        out_specs=pl.BlockSpec((m, tn), lambda j: (0, j)),
        out_shape=jax.ShapeDtypeStruct((m, n), F32),
        compiler_params=_cparams("arbitrary"),
        name="ada_mod",
    )(c, w, b.reshape(1, n))


def _conv_out(bg, u, u1, u2, cw_ref, cb_ref):
    y = cb_ref[...] + u2 * cw_ref[0:1, :] + u1 * cw_ref[1:2, :] + u * cw_ref[2:3, :]
    return bg * y


def _inproj_prompt_kernel(x_ref, sc_ref, sh_ref, w_ref, cw_ref, cb_ref,
                          qT_ref, k_ref, v_ref, kb_ref, vTb_ref, qiT_ref, ki_ref, kib_ref, wiT_ref,
                          oconv_ref, ga_ref, gb_ref, ulast_ref, uprev_scr):
    s = pl.program_id(1)
    tm = x_ref.shape[1]
    h = (x_ref[0] * (1.0 + sc_ref[0]) + sh_ref[0]).astype(BF16)

    def proj(a, width):
        return _dot(h, w_ref[:, a:a + width])

    qT_ref[0] = proj(_OFF_Q, ATTN_WIDTH).T.astype(BF16)
    k = proj(_OFF_K, KV_WIDTH)
    k_ref[0] = k
    kb_ref[0] = k.astype(BF16)
    v = proj(_OFF_V, KV_WIDTH)
    v_ref[0] = v
    for i in range(tm // LANES):
        vTb_ref[0, i] = v[i * LANES:(i + 1) * LANES].T.astype(BF16)
    qiT_ref[0] = proj(_OFF_QI, N_IDX_HEADS * IDX_DIM).T.astype(BF16)
    kiwi = proj(_OFF_KIWI, _KIWI)
    ki_ref[0] = kiwi[:, :IDX_DIM]
    kib_ref[0] = kiwi[:, :IDX_DIM].astype(BF16)
    wiT_ref[0] = kiwi.T[IDX_DIM:IDX_DIM + 8]

    @pl.when(s == 0)
    def _():
        uprev_scr[...] = jnp.zeros(uprev_scr.shape, F32)

    bg = proj(_OFF_BG, D_CONV)
    u = proj(_OFF_CG, D_CONV) * proj(_OFF_XIN, D_CONV)
    row = lax.broadcasted_iota(I32, u.shape, 0)
    up = uprev_scr[...]
    u1 = jnp.where(row < 1, pltpu.roll(up, 1, 0), pltpu.roll(u, 1, 0))
    u2 = jnp.where(row < 2, pltpu.roll(up, 2, 0), pltpu.roll(u, 2, 0))
    oconv_ref[0] = _conv_out(bg, u, u1, u2, cw_ref, cb_ref).astype(BF16)
    uprev_scr[...] = u
    ulast_ref[0] = u[tm - 8:, :]

    ga_ref[0] = proj(_OFF_GA, w_ref.shape[0])
    gb_ref[0] = proj(_OFF_GA + w_ref.shape[0], w_ref.shape[0])


def _inproj_prompt(x, sc, sh, w_all, conv_w, conv_b):
    b, s, d = x.shape
    tm = min(256, s)
    nkb = tm // LANES
    row = lambda width, dt: jax.ShapeDtypeStruct((b, s, width), dt)
    out_shape = (
        jax.ShapeDtypeStruct((b, ATTN_WIDTH, s), BF16),
        row(KV_WIDTH, F32), row(KV_WIDTH, F32), row(KV_WIDTH, BF16),
        jax.ShapeDtypeStruct((b, s // LANES, KV_WIDTH, LANES), BF16),
        jax.ShapeDtypeStruct((b, N_IDX_HEADS * IDX_DIM, s), BF16),
        row(IDX_DIM, F32), row(IDX_DIM, BF16),
        jax.ShapeDtypeStruct((b, 8, s), F32),
        row(D_CONV, BF16),
        row(d, F32), row(d, F32),
        jax.ShapeDtypeStruct((b, 8, D_CONV), F32),
    )
    tile = lambda width: pl.BlockSpec((1, tm, width), lambda i, j: (i, j, 0))
    tileT = lambda rows: pl.BlockSpec((1, rows, tm), lambda i, j: (i, 0, j))
    out_specs = (
        tileT(ATTN_WIDTH), tile(KV_WIDTH), tile(KV_WIDTH), tile(KV_WIDTH),
        pl.BlockSpec((1, nkb, KV_WIDTH, LANES), lambda i, j: (i, j, 0, 0)),
        tileT(N_IDX_HEADS * IDX_DIM), tile(IDX_DIM), tile(IDX_DIM), tileT(8),
        tile(D_CONV), tile(d), tile(d),
        pl.BlockSpec((1, 8, D_CONV), lambda i, j: (i, 0, 0)),
    )
    vec = pl.BlockSpec((1, 1, d), lambda i, j: (i, 0, 0))
    return pl.pallas_call(
        _inproj_prompt_kernel,
        grid=(b, s // tm),
        in_specs=[tile(d), vec, vec,
                  pl.BlockSpec(w_all.shape, lambda i, j: (0, 0)),
                  pl.BlockSpec(conv_w.shape, lambda i, j: (0, 0)),
                  pl.BlockSpec((1, D_CONV), lambda i, j: (0, 0))],
        out_specs=out_specs,
        out_shape=out_shape,
        scratch_shapes=[pltpu.VMEM((tm, D_CONV), F32)],
        compiler_params=_cparams("arbitrary", "arbitrary"),
        name="inproj_prompt",
    )(x, sc, sh, w_all, conv_w, conv_b.reshape(1, D_CONV))


def _inproj_sample_kernel(t_seq, x_ref, sc_ref, sh_ref, w_ref, cw_ref, cb_ref, s0_ref, s1_ref,
                          q_ref, k_ref, v_ref, qi_ref, kiwi_ref, oconv_ref, ga_ref, gb_ref, u_ref):
    h = (x_ref[...] * (1.0 + sc_ref[...]) + sh_ref[...]).astype(BF16)

    def proj(a, width):
        return _dot(h, w_ref[:, a:a + width])

    q_ref[...] = proj(_OFF_Q, ATTN_WIDTH)
    k_ref[...] = proj(_OFF_K, KV_WIDTH)
    v_ref[...] = proj(_OFF_V, KV_WIDTH)
    qi_ref[...] = proj(_OFF_QI, N_IDX_HEADS * IDX_DIM)
    kiwi_ref[...] = proj(_OFF_KIWI, _KIWI)
    bg = proj(_OFF_BG, D_CONV)
    u = proj(_OFF_CG, D_CONV) * proj(_OFF_XIN, D_CONV)
    t = lax.broadcasted_iota(I32, u.shape, 0) % t_seq
    u1 = jnp.where(t == 0, s1_ref[...], pltpu.roll(u, 1, 0))
    u2 = jnp.where(t == 0, s0_ref[...], jnp.where(t == 1, s1_ref[...], pltpu.roll(u, 2, 0)))
    oconv_ref[...] = _conv_out(bg, u, u1, u2, cw_ref, cb_ref).astype(BF16)
    u_ref[...] = u
    ga_ref[...] = proj(_OFF_GA, w_ref.shape[0])
    gb_ref[...] = proj(_OFF_GA + w_ref.shape[0], w_ref.shape[0])


def _inproj_sample(x, sc_rows, sh_rows, w_all, conv_w, conv_b, st0_rows, st1_rows, t_seq):
    n, d = x.shape
    full = lambda a: pl.BlockSpec(a.shape, lambda i: (0,) * a.ndim)
    o = lambda width, dt: jax.ShapeDtypeStruct((n, width), dt)
    out_shape = (o(ATTN_WIDTH, F32), o(KV_WIDTH, F32), o(KV_WIDTH, F32), o(N_IDX_HEADS * IDX_DIM, F32),
                 o(_KIWI, F32), o(D_CONV, BF16), o(d, F32), o(d, F32), o(D_CONV, F32))
    args = (x, sc_rows, sh_rows, w_all, conv_w, conv_b.reshape(1, D_CONV), st0_rows, st1_rows)
    return pl.pallas_call(
        functools.partial(_inproj_sample_kernel, t_seq),
        grid=(1,),
        in_specs=[full(a) for a in args],
        out_specs=tuple(pl.BlockSpec(s.shape, lambda i: (0, 0)) for s in out_shape),
        out_shape=out_shape,
        compiler_params=_cparams("arbitrary"),
        name="inproj_sample",
    )(*args)


def _attn_prompt_kernel(n_sel, relb_ref, qT_ref, qiT_ref, wiT_ref, kb_ref, vTb_ref, kib_ref,
                        o_ref, bias_scr, key_scr):
    qb = pl.program_id(1)
    blk = LANES
    kblk = 2 * blk
    ntrip = qb // 2 + 1

    @pl.when((pl.program_id(0) == 0) & (qb == 0))
    def _():
        j = lax.broadcasted_iota(I32, (blk, blk), 0)
        i = lax.broadcasted_iota(I32, (blk, blk), 1)
        for delta in range(3):
            dist = i - j + blk * delta
            for h in range(N_HEADS):
                g, r = divmod(h, GROUP)
                bias_scr[delta, g, :, r * blk:(r + 1) * blk] = _t5_bias(dist, relb_ref, h)

    row = lax.broadcasted_iota(I32, (kblk, blk), 0)
    lane = lax.broadcasted_iota(I32, (kblk, blk), 1)

    def valid_mask(t):
        return (t * kblk + row) <= (qb * blk + lane)

    def key_rows(t):
        return pl.ds(pl.multiple_of(t * kblk, kblk), kblk)

    qiT = jnp.concatenate([qiT_ref[0, h * IDX_DIM:(h + 1) * IDX_DIM, :] for h in range(N_IDX_HEADS)], axis=1)
    wi = [wiT_ref[0, h:h + 1, :] for h in range(N_IDX_HEADS)]
    fold = lambda a: a.reshape(kblk // 8, 8, blk)
    tall = lambda a: jnp.sum(a, axis=0, keepdims=True)
    part0 = (jnp.zeros((8, blk), F32), jnp.full((8, blk), jnp.inf, F32), jnp.full((8, blk), -jnp.inf, F32))

    def score_blk(t, carry):
        cnt, lo, hi = carry
        d = _dot(kib_ref[0, key_rows(t), :], qiT)
        s = jnp.maximum(d[:, 0:blk], 0.0) * wi[0]
        for h in range(1, N_IDX_HEADS):
            s = s + jnp.maximum(d[:, h * blk:(h + 1) * blk], 0.0) * wi[h]
        s = s * IDX_SCALE
        valid = valid_mask(t)
        key_scr[key_rows(t), :] = jnp.where(valid, s, -jnp.inf)
        return (cnt + jnp.sum(fold(jnp.where(valid, 1.0, 0.0)), axis=0),
                jnp.minimum(lo, jnp.min(fold(jnp.where(valid, s, jnp.inf)), axis=0)),
                jnp.maximum(hi, jnp.max(fold(jnp.where(valid, s, -jnp.inf)), axis=0)))

    cnt, lo, hi = lax.fori_loop(0, ntrip, score_blk, part0)
    n_valid = tall(cnt)
    v_min = jnp.min(lo, axis=0, keepdims=True)
    v_max = jnp.max(hi, axis=0, keepdims=True)

    def stats(mid):
        def body(t, carry):
            cnt, above, below = carry
            x = key_scr[key_rows(t), :]
            ge = x >= mid
            return (cnt + jnp.sum(fold(jnp.where(ge, 1.0, 0.0)), axis=0),
                    jnp.minimum(above, jnp.min(fold(jnp.where(ge, x, jnp.inf)), axis=0)),
                    jnp.maximum(below, jnp.max(fold(jnp.where(ge, -jnp.inf, x)), axis=0)))
        cnt, above, below = lax.fori_loop(0, ntrip, body, part0)
        return tall(cnt), jnp.min(above, axis=0, keepdims=True), jnp.max(below, axis=0, keepdims=True)

    def count(pred):
        def body(t, acc):
            return acc + jnp.sum(fold(jnp.where(pred(key_scr[key_rows(t), :]), 1.0, 0.0)), axis=0)
        return tall(lax.fori_loop(0, ntrip, body, jnp.zeros((8, blk), F32)))

    thr = _kth_largest(stats, lambda mid: count(lambda x: x >= mid), float(n_sel), n_valid, v_min, v_max)

    need = n_sel - count(lambda x: x > thr)
    ltri = (lax.broadcasted_iota(I32, (kblk, kblk), 1) <= lax.broadcasted_iota(I32, (kblk, kblk), 0)).astype(BF16)

    def select_mask(t, ties_before):
        kv = key_scr[key_rows(t), :]
        eq = kv == thr
        prefix = _dot(ltri, eq.astype(BF16)) + ties_before
        sel = ((kv > thr) | (eq & (prefix <= need))) & valid_mask(t)
        return jnp.where(sel, 0.0, NEG_BIG), prefix[kblk - 1:kblk, :]

    zero = jnp.zeros((HEAD_DIM, blk), BF16)
    scale = jnp.asarray(ATTN_SCALE, BF16)
    qTg = []
    for g in range(N_KV_HEADS):
        cols = []
        for r in range(GROUP):
            base = g * GROUP * HEAD_DIM + r * HEAD_DIM
            parts = [zero] * N_KV_HEADS
            parts[g] = qT_ref[0, base:base + HEAD_DIM, :] * scale
            cols.append(jnp.concatenate(parts, axis=0))
        qTg.append(jnp.concatenate(cols, axis=1))
    width = GROUP * blk

    def att_blk(t, carry):
        kmat = kb_ref[0, key_rows(t), :]
        mask1, ties = select_mask(t, carry[N_KV_HEADS])
        mask = jnp.concatenate([mask1] * GROUP, axis=1)
        d0 = jnp.clip(qb - 2 * t, 0, 2)
        d1 = jnp.clip(qb - 2 * t - 1, 0, 2)
        new = []
        for g in range(N_KV_HEADS):
            m, l, acc = carry[g]
            bias = jnp.concatenate([bias_scr[d0, g], bias_scr[d1, g]], axis=0)
            lg = _dot(kmat, qTg[g]) + bias + mask
            m_new = jnp.maximum(m, jnp.max(lg, axis=0, keepdims=True))
            p = jnp.exp(lg - m_new)
            alpha = jnp.exp(m - m_new)
            l = alpha * l + jnp.sum(p, axis=0, keepdims=True)
            rows = slice(g * HEAD_DIM, (g + 1) * HEAD_DIM)
            vt = jnp.concatenate([vTb_ref[0, 2 * t, rows, :], vTb_ref[0, 2 * t + 1, rows, :]], axis=1)
            acc = alpha * acc + _dot(vt, p.astype(BF16))
            new.append((m_new, l, acc))
        return tuple(new) + (ties,)

    init = (jnp.full((1, width), NEG_BIG, F32), jnp.zeros((1, width), F32), jnp.zeros((HEAD_DIM, width), F32))
    res = lax.fori_loop(0, ntrip, att_blk, (init,) * N_KV_HEADS + (jnp.zeros((1, blk), F32),))
    outs = []
    for g in range(N_KV_HEADS):
        _, l, acc = res[g]
        og = acc / l
        outs += [og[:, r * blk:(r + 1) * blk] for r in range(GROUP)]
    o_ref[0] = jnp.concatenate(outs, axis=0).T.astype(BF16)


def _attn_prompt(rel_bias, qT, qiT, wiT, kb, vTb, kib):
    b, _, s = qT.shape
    n_sel = min(TOPK_MAX, s // 4)
    blk = LANES
    per_b = lambda a: pl.BlockSpec((1,) + a.shape[1:], lambda i, j: (i,) + (0,) * (a.ndim - 1))
    qtile = lambda rows: pl.BlockSpec((1, rows, blk), lambda i, j: (i, 0, j))
    return pl.pallas_call(
        functools.partial(_attn_prompt_kernel, n_sel),
        grid=(b, s // blk),
        in_specs=[pl.BlockSpec(memory_space=pltpu.SMEM),
                  qtile(ATTN_WIDTH), qtile(N_IDX_HEADS * IDX_DIM), qtile(8),
                  per_b(kb), per_b(vTb), per_b(kib)],
        out_specs=pl.BlockSpec((1, blk, ATTN_WIDTH), lambda i, j: (i, j, 0)),
        out_shape=jax.ShapeDtypeStruct((b, s, ATTN_WIDTH), BF16),
        scratch_shapes=[pltpu.VMEM((3, N_KV_HEADS, blk, GROUP * blk), F32),
                        pltpu.VMEM((s, blk), F32)],
        compiler_params=_cparams("arbitrary", "arbitrary"),
        name="attn_prompt",
    )(rel_bias, qT, qiT, wiT, kb, vTb, kib)


def _attn_sample_kernel(layer, n_sel, past, t_seq, pt_ref, relb_ref, q_ref, qi_ref, wi_ref,
                        knew_ref, vnew_ref, kinew_ref, ck_hbm, cv_hbm, cki_hbm,
                        o_ref, kbuf, vbuf, kibuf, sem):
    b = pl.program_id(0)
    nb = pl.num_programs(0)
    n_pages = past // PAGE_SIZE
    lp = kibuf.shape[1]
    blk = LANES
    nblk = lp // blk
    slot = b % 2

    def page_rows(p):
        return pl.ds(pl.multiple_of(p * PAGE_SIZE, PAGE_SIZE), PAGE_SIZE)

    def kv_copies(seq, p):
        phys = pt_ref[seq, p]
        cps = []
        for g in range(N_KV_HEADS):
            cps.append(pltpu.make_async_copy(ck_hbm.at[layer, phys, :, g, :], kbuf.at[g, page_rows(p), :], sem.at[0]))
            cps.append(pltpu.make_async_copy(cv_hbm.at[layer, phys, :, g, :], vbuf.at[g, page_rows(p), :], sem.at[1]))
        return cps

    def ki_copy(seq, sl, p):
        return pltpu.make_async_copy(cki_hbm.at[layer, pt_ref[seq, p]], kibuf.at[sl, page_rows(p), :],
                                     sem.at[2 + sl])

    def for_pages(fn):
        def body(p, c):
            fn(p)
            return c
        lax.fori_loop(0, n_pages, body, 0)

    def start_ki(seq, sl):
        for_pages(lambda p: ki_copy(seq, sl, p).start())

    @pl.when(b == 0)
    def _():
        for g in range(N_KV_HEADS):
            kbuf[g, past:, :] = jnp.zeros((blk, HEAD_DIM), F32)
            vbuf[g, past:, :] = jnp.zeros((blk, HEAD_DIM), F32)
        for sl in range(2):
            kibuf[sl, past:, :] = jnp.zeros((blk, IDX_DIM), F32)
        start_ki(0, 0)

    for_pages(lambda p: [cp.start() for cp in kv_copies(b, p)])

    @pl.when(b + 1 < nb)
    def _():
        start_ki(b + 1, 1 - slot)

    for g in range(N_KV_HEADS):
        kbuf[g, past:past + t_seq, :] = knew_ref[0, g]
        vbuf[g, past:past + t_seq, :] = vnew_ref[0, g]
    kibuf[slot, past:past + t_seq, :] = kinew_ref[0]
    for_pages(lambda p: ki_copy(b, slot, p).wait())

    d = _dot_nt(qi_ref[0], kibuf[slot].astype(BF16))
    s = jnp.maximum(d[0:t_seq], 0.0) * wi_ref[0, 0:t_seq, 0:1]
    for h in range(1, N_IDX_HEADS):
        s = s + jnp.maximum(d[h * t_seq:(h + 1) * t_seq], 0.0) * wi_ref[0, h * t_seq:(h + 1) * t_seq, 0:1]
    kpos = lax.broadcasted_iota(I32, (t_seq, lp), 1)
    qpos = past + lax.broadcasted_iota(I32, (t_seq, lp), 0)
    valid = kpos <= qpos
    s = s * IDX_SCALE
    key = jnp.where(valid, s, -jnp.inf)

    def count(pred):
        return jnp.sum(jnp.where(pred, 1.0, 0.0), axis=1, keepdims=True)

    def stats(mid):
        ge = key >= mid
        return (count(ge), jnp.min(jnp.where(ge, key, jnp.inf), axis=1, keepdims=True),
                jnp.max(jnp.where(ge, -jnp.inf, key), axis=1, keepdims=True))

    thr = _kth_largest(stats, lambda mid: count(key >= mid), float(n_sel), count(valid),
                       jnp.min(jnp.where(valid, s, jnp.inf), axis=1, keepdims=True),
                       jnp.max(key, axis=1, keepdims=True))
    need = n_sel - count(key > thr)
    eq = key == thr
    eqf = jnp.where(eq, 1.0, 0.0)
    utri = (lax.broadcasted_iota(I32, (blk, blk), 0) <= lax.broadcasted_iota(I32, (blk, blk), 1)).astype(BF16)
    carry = jnp.zeros((t_seq, 1), F32)
    ranks = []
    for kb in range(nblk):
        e = eqf[:, kb * blk:(kb + 1) * blk]
        ranks.append(_dot(e.astype(BF16), utri) + carry)
        carry = carry + jnp.sum(e, axis=1, keepdims=True)
    prefix = jnp.concatenate(ranks, axis=1)
    sel = jnp.where(((key > thr) | (eq & (prefix <= need))) & valid, 1.0, 0.0)

    for_pages(lambda p: [cp.wait() for cp in kv_copies(b, p)])
    far = lp - 2 * blk
    near_dist = (past + lax.broadcasted_iota(I32, (t_seq, 2 * blk), 0)
                 - (far + lax.broadcasted_iota(I32, (t_seq, 2 * blk), 1)))
    selg = jnp.concatenate([sel] * GROUP, axis=0) > 0.0
    for g in range(N_KV_HEADS):
        lg = _dot_nt(q_ref[0, g], kbuf[g].astype(BF16)) * ATTN_SCALE
        bias_rows = []
        for r in range(GROUP):
            h = g * GROUP + r
            far_bias = jnp.full((t_seq, far), relb_ref[N_BUCKETS - 1, h], F32)
            bias_rows.append(jnp.concatenate([far_bias, _t5_bias(near_dist, relb_ref, h)], axis=1))
        lg = lg + jnp.concatenate(bias_rows, axis=0)
        m = jnp.max(jnp.where(selg, lg, NEG_BIG), axis=1, keepdims=True)
        p = jnp.where(selg, jnp.exp(lg - m), 0.0)
        l = jnp.sum(p, axis=1, keepdims=True)
        o_ref[0, g] = _dot(p.astype(BF16), vbuf[g].astype(BF16)) / l


def _attn_sample(layer, page_table, rel_bias, q_g, qi_r, wi_r, k_new, v_new, ki_new, cache_k, cache_v, cache_ki,
                 past):
    nb, _, t_seq, _ = k_new.shape
    n_sel = min(TOPK_MAX, (past + t_seq) // 4)
    lp = past + LANES
    rows = GROUP * t_seq
    per_b = lambda a: pl.BlockSpec((1,) + a.shape[1:], lambda i, pt: (i,) + (0,) * (a.ndim - 1))
    hbm = pl.BlockSpec(memory_space=pl.ANY)
    grid_spec = pltpu.PrefetchScalarGridSpec(
        num_scalar_prefetch=1,
        grid=(nb,),
        in_specs=[pl.BlockSpec(memory_space=pltpu.SMEM),
                  per_b(q_g), per_b(qi_r), per_b(wi_r), per_b(k_new), per_b(v_new), per_b(ki_new),
                  hbm, hbm, hbm],
        out_specs=pl.BlockSpec((1, N_KV_HEADS, rows, HEAD_DIM), lambda i, pt: (i, 0, 0, 0)),
        scratch_shapes=[pltpu.VMEM((N_KV_HEADS, lp, HEAD_DIM), F32),
                        pltpu.VMEM((N_KV_HEADS, lp, HEAD_DIM), F32),
                        pltpu.VMEM((2, lp, IDX_DIM), F32),
                        pltpu.SemaphoreType.DMA((4,))],
    )
    return pl.pallas_call(
        functools.partial(_attn_sample_kernel, layer, n_sel, past, t_seq),
        grid_spec=grid_spec,
        out_shape=jax.ShapeDtypeStruct((nb, N_KV_HEADS, rows, HEAD_DIM), F32),
        compiler_params=_cparams("arbitrary"),
        name="attn_sample",
    )(page_table, rel_bias, q_g, qi_r, wi_r, k_new, v_new, ki_new, cache_k, cache_v, cache_ki)


def _merge_kernel(alpha, x_ref, oa_ref, oc_ref, ga_ref, gb_ref, g1_ref, sc2_ref, sh2_ref,
                  woa_ref, woc_ref, wout_ref, wq_ref, lng_ref, lnb_ref,
                  x1_ref, h2T_ref, pqT_ref):
    ta = _dot(oa_ref[...], woa_ref[...])
    tc = _dot(oc_ref[...], woc_ref[...])
    merged = jax.nn.sigmoid(ga_ref[...]) * ta + jax.nn.sigmoid(gb_ref[...]) * tc
    out = _dot(merged.astype(BF16), wout_ref[...])
    x1 = _layer_norm(alpha * x_ref[...] + g1_ref[0] * out, lng_ref[...], lnb_ref[...])
    x1_ref[...] = x1
    h2 = x1 * (1.0 + sc2_ref[0]) + sh2_ref[0]
    h2b = h2.astype(BF16)
    h2T_ref[...] = h2.T.astype(BF16)
    pqT_ref[...] = _dot(h2b, wq_ref[...]).T.astype(BF16)


def _merge(alpha, x, oattn, oconv, ga, gb, g1, sc2, sh2, woa, woc, wout, wq, ln_g, ln_b, rows_per_vec):
    n, d = x.shape
    tm = min(512, rows_per_vec) if g1.shape[1] == 1 else g1.shape[1]
    tiles_per_vec = rows_per_vec // tm
    tile = lambda width: pl.BlockSpec((tm, width), lambda i: (i, 0))
    vec = pl.BlockSpec((1,) + g1.shape[1:], lambda i: (i // tiles_per_vec, 0, 0))
    full = lambda a: pl.BlockSpec(a.shape, lambda i: (0,) * a.ndim)
    pq_w = wq.shape[1]
    return pl.pallas_call(
        functools.partial(_merge_kernel, alpha),
        grid=(n // tm,),
        in_specs=[tile(d), tile(ATTN_WIDTH), tile(D_CONV), tile(d), tile(d), vec, vec, vec,
                  full(woa), full(woc), full(wout), full(wq), full(ln_g), full(ln_b)],
        out_specs=(tile(d), pl.BlockSpec((d, tm), lambda i: (0, i)), pl.BlockSpec((pq_w, tm), lambda i: (0, i))),
        out_shape=(jax.ShapeDtypeStruct((n, d), F32), jax.ShapeDtypeStruct((d, n), BF16),
                   jax.ShapeDtypeStruct((pq_w, n), BF16)),
        compiler_params=_cparams("arbitrary"),
        name="merge_ln1_peerq",
    )(x, oattn, oconv, ga, gb, g1, sc2, sh2, woa, woc, wout, wq, ln_g, ln_b)


def _top16(s, v_scr):
    iota = lax.broadcasted_iota(I32, s.shape, 0)

    def body(r, carry):
        x, rank = carry
        m = jnp.max(x, axis=0, keepdims=True)
        idx = jnp.min(jnp.where(x == m, iota, N_KEYS), axis=0, keepdims=True)
        hit = iota == idx
        v_scr[pl.ds(r, 1), :] = m
        return jnp.where(hit, -jnp.inf, x), jnp.where(hit, lax.convert_element_type(r, F32), rank)

    _, rank = lax.fori_loop(0, PEER_TOPK, body, (s, jnp.full(s.shape, float(PEER_TOPK), F32)))
    return rank


def _pair_candidates(v1, v2):
    return jnp.concatenate([v1[0:1] + v2] + [v1[a:a + 1] + v2[0:8] for a in range(1, 8)]
                           + [v1[8:16] + v2[0:1]], axis=0)


def _peer_select_kernel(pqT_ref, k1_ref, k2_ref, cnt1_ref, w1_ref, rank2_ref, e2_ref,
                        v_scr, r_scr, vx1_scr, vx2_scr):
    tn = pqT_ref.shape[1]
    k = PEER_TOPK
    tiles = (N_KEYS // BF16_ROWS, BF16_ROWS, tn)

    def scores(h):
        base = h * 2 * PEER_HALF
        return (_dot(k1_ref[...], pqT_ref[base:base + PEER_HALF, :]),
                _dot(k2_ref[...], pqT_ref[base + PEER_HALF:base + 2 * PEER_HALF, :]))

    def emit(h, s1, s2, rank1, rank2, cnt_rows, z, top1, top2):
        cnt1 = jnp.zeros(s1.shape, F32)
        for a in range(k):
            cnt1 = jnp.where(rank1 == float(a), cnt_rows[a], cnt1)
        cnt1_ref[h] = cnt1
        w1_ref[h] = jnp.exp(s1 - top1) * (0.5 / z)
        rank2_ref[h] = rank2.astype(BF16).reshape(tiles)
        e2_ref[h] = jnp.exp(s2 - top2).astype(BF16).reshape(tiles)

    def finish(h, par, x, z):
        s1, s2 = scores(h)
        rank1, rank2 = r_scr[par, 0], r_scr[par, 1]
        taken = jnp.where(x == -jnp.inf, 1.0, 0.0)
        cnt_rows = ([jnp.sum(taken[0:k], axis=0, keepdims=True)]
                    + [jnp.sum(taken[k + 8 * (a - 1):k + 8 * a], axis=0, keepdims=True) for a in range(1, 8)]
                    + [taken[k + 56 + a:k + 57 + a] for a in range(8)])
        n_pairs = cnt_rows[0]
        for row in cnt_rows[1:]:
            n_pairs = n_pairs + row
        n1 = jnp.sum(jnp.where(rank1 < float(k), 1.0, 0.0), axis=0, keepdims=True)
        n2 = jnp.sum(jnp.where(rank2 < float(k), 1.0, 0.0), axis=0, keepdims=True)
        tied = (n1 != float(k)) | (n2 != float(k)) | (n_pairs != float(k))
        emit(h, s1, s2, rank1, rank2, cnt_rows, z, v_scr[par, 0, 0:1, :], v_scr[par, 1, 0:1, :])

        @pl.when(jnp.sum(jnp.where(tied, 1.0, 0.0)) > 0.0)
        def _():
            rank1 = _top16(s1, vx1_scr)
            rank2 = _top16(s2, vx2_scr)
            v1, v2 = vx1_scr[...], vx2_scr[...]
            top = v1[0:1] + v2[0:1]
            r16 = lax.broadcasted_iota(I32, (k, tn), 0)
            r8 = lax.broadcasted_iota(I32, (8, tn), 0)
            flat = jnp.concatenate([r16] + [a * k + r8 for a in range(1, 8)] + [(8 + r8) * k], axis=0)

            def pick_exact(r, carry):
                x, cnt, z = carry
                m = jnp.max(x, axis=0, keepdims=True)
                idx = jnp.min(jnp.where(x == m, flat, k * k), axis=0, keepdims=True)
                cnt = cnt + jnp.where(r16 == (idx >> 4), 1.0, 0.0)
                return jnp.where(flat == idx, -jnp.inf, x), cnt, z + jnp.exp(m - top)

            _, cnt, z = lax.fori_loop(0, k, pick_exact, (_pair_candidates(v1, v2), jnp.zeros((k, tn), F32),
                                                          jnp.zeros((1, tn), F32)))
            emit(h, s1, s2, rank1, rank2, [cnt[a:a + 1] for a in range(k)], z, v1[0:1], v2[0:1])

    pending = None
    for h in range(PEER_HEADS + 1):
        par = h % 2
        lists = h < PEER_HEADS
        init = []
        if lists:
            init += list(scores(h))
            for half in range(2):
                r_scr[par, half] = jnp.full((N_KEYS, tn), float(k), F32)
        if pending is not None:
            init += [pending[0], jnp.zeros((1, tn), F32)]
            top = pending[1]

        def body(r, carry, lists=lists, pairs=pending is not None, par=par):
            out = []
            if lists:
                for half in range(2):
                    x = carry[half]
                    m = jnp.max(x, axis=0, keepdims=True)
                    hit = x == m
                    v_scr[par, half, pl.ds(r, 1), :] = m
                    r_scr[par, half] = jnp.where(hit, lax.convert_element_type(r, F32), r_scr[par, half])
                    out.append(jnp.where(hit, -jnp.inf, x))
            if pairs:
                x, z = carry[-2], carry[-1]
                m = jnp.max(x, axis=0, keepdims=True)
                out += [jnp.where(x == m, -jnp.inf, x), z + jnp.exp(m - top)]
            return tuple(out)

        res = lax.fori_loop(0, k, body, tuple(init))
        if pending is not None:
            finish(h - 1, 1 - par, res[-2], res[-1])
        if lists:
            v1, v2 = v_scr[par, 0], v_scr[par, 1]
            pending = (_pair_candidates(v1, v2), v1[0:1] + v2[0:1])
        else:
            pending = None


def _peer_select(pqT, k1, k2):
    _, n = pqT.shape
    tn = LANES
    nt = n // tn
    tab = jax.ShapeDtypeStruct((PEER_HEADS, N_KEYS, n), F32)
    tab_spec = pl.BlockSpec((PEER_HEADS, N_KEYS, tn), lambda i: (0, 0, i))
    tabb = jax.ShapeDtypeStruct((PEER_HEADS, N_KEYS // BF16_ROWS, BF16_ROWS, n), BF16)
    tabb_spec = pl.BlockSpec((PEER_HEADS, N_KEYS // BF16_ROWS, BF16_ROWS, tn), lambda i: (0, 0, 0, i))
    list_scr = pltpu.VMEM((PEER_TOPK, tn), F32)
    return pl.pallas_call(
        _peer_select_kernel,
        grid=(nt,),
        in_specs=[pl.BlockSpec((pqT.shape[0], tn), lambda i: (0, i)),
                  pl.BlockSpec(k1.shape, lambda i: (0, 0)),
                  pl.BlockSpec(k2.shape, lambda i: (0, 0))],
        out_specs=(tab_spec, tab_spec, tabb_spec, tabb_spec),
        out_shape=(tab, tab, tabb, tabb),
        scratch_shapes=[pltpu.VMEM((2, 2, PEER_TOPK, tn), F32), pltpu.VMEM((2, 2, N_KEYS, tn), F32),
                        list_scr, list_scr],
        compiler_params=_cparams("arbitrary"),
        name="peer_select",
    )(pqT, k1, k2)


def _peer_dense_kernel(alpha, i1_per_step, h2T_ref, u_ref, vT_ref, cnt1_ref, w1_ref, rank2_ref, e2_ref,
                       x1_ref, g2_ref, lng_ref, lnb_ref, y_ref, acc_scr, aT_scr, act_scr):
    j = pl.program_id(1)

    @pl.when(j == 0)
    def _():
        acc_scr[...] = jnp.zeros(acc_scr.shape, F32)

    tn = aT_scr.shape[1]
    tiles = (N_KEYS // BF16_ROWS, BF16_ROWS, tn)
    zero = jnp.zeros(tiles, BF16)
    for ii in range(i1_per_step):
        i1 = j * i1_per_step + ii
        gate = zero
        for h in range(PEER_HEADS):
            c1 = jnp.broadcast_to(cnt1_ref[h, pl.ds(i1, 1), :], (BF16_ROWS, tn)).astype(BF16)
            w1 = jnp.broadcast_to(w1_ref[h, pl.ds(i1, 1), :], (BF16_ROWS, tn)).astype(BF16)
            gate = gate + jnp.where(rank2_ref[h] < c1[None], e2_ref[h], zero) * w1[None]
        act_scr[ii * N_KEYS:(ii + 1) * N_KEYS, :] = gate.reshape(N_KEYS, tn)
    aT_scr[...] = _dot(u_ref[...], h2T_ref[...])
    for ii in range(i1_per_step):
        rows = slice(ii * N_KEYS, (ii + 1) * N_KEYS)
        a = aT_scr[rows, :].astype(BF16)
        t = jnp.tanh(a * (_GELU_C1 + _GELU_C2 * (a * a)))
        act_scr[rows, :] = (a + a * t) * act_scr[rows, :]
    acc_scr[...] += _dot(vT_ref[...], act_scr[...])

    @pl.when(j == pl.num_programs(1) - 1)
    def _():
        peer = acc_scr[...].T
        y_ref[...] = _layer_norm(alpha * x1_ref[...] + g2_ref[0] * peer, lng_ref[...], lnb_ref[...])


def _peer_dense(alpha, h2T, u_b, vT_b, tabs, x1, g2, ln_g, ln_b, rows_per_vec):
    d, n = h2T.shape
    tn = min(512, rows_per_vec) if g2.shape[1] == 1 else g2.shape[1]
    tiles_per_vec = rows_per_vec // tn
    i1_per_step = 8
    te = i1_per_step * N_KEYS
    n_exp = u_b.shape[0]
    tab_spec = pl.BlockSpec((PEER_HEADS, N_KEYS, tn), lambda i, j: (0, 0, i))
    tabb_spec = pl.BlockSpec((PEER_HEADS, N_KEYS // BF16_ROWS, BF16_ROWS, tn), lambda i, j: (0, 0, 0, i))
    return pl.pallas_call(
        functools.partial(_peer_dense_kernel, alpha, i1_per_step),
        grid=(n // tn, n_exp // te),
        in_specs=[pl.BlockSpec((d, tn), lambda i, j: (0, i)),
                  pl.BlockSpec((te, d), lambda i, j: (j, 0)),
                  pl.BlockSpec((d, te), lambda i, j: (0, j)),
                  tab_spec, tab_spec, tabb_spec, tabb_spec,
                  pl.BlockSpec((tn, d), lambda i, j: (i, 0)),
                  pl.BlockSpec((1,) + g2.shape[1:], lambda i, j: (i // tiles_per_vec, 0, 0)),
                  pl.BlockSpec(ln_g.shape, lambda i, j: (0, 0)),
                  pl.BlockSpec(ln_b.shape, lambda i, j: (0, 0))],
        out_specs=pl.BlockSpec((tn, d), lambda i, j: (i, 0)),
        out_shape=jax.ShapeDtypeStruct((n, d), F32),
        scratch_shapes=[pltpu.VMEM((d, tn), F32), pltpu.VMEM((te, tn), F32), pltpu.VMEM((te, tn), BF16)],
        compiler_params=_cparams("arbitrary", "arbitrary"),
        name="peer_dense_ln2",
    )(h2T, u_b, vT_b, *tabs, x1, g2, ln_g, ln_b)


def _transpose_cast_kernel(x_ref, o_ref):
    o_ref[...] = x_ref[...].T.astype(BF16)


def _transpose_cast(x):
    r, c = x.shape
    tr = min(512, r)
    return pl.pallas_call(
        _transpose_cast_kernel,
        grid=(r // tr,),
        in_specs=[pl.BlockSpec((tr, c), lambda i: (i, 0))],
        out_specs=pl.BlockSpec((c, tr), lambda i: (0, i)),
        out_shape=jax.ShapeDtypeStruct((c, r), BF16),
        compiler_params=_cparams("arbitrary"),
        name="transpose_cast",
    )(x)


def _pack_w_in(w):
    widths = (ATTN_WIDTH, KV_WIDTH, KV_WIDTH, N_IDX_HEADS * IDX_DIM, N_IDX_HEADS, IDX_DIM,
              D_CONV, D_CONV, D_CONV, w.shape[0], w.shape[0])
    parts, start = [], 0
    for wd in widths:
        parts.append(w[:, start:start + wd])
        start += wd
    q, k, v, qi, wi, ki, bg, cg, xin, ga, gb = parts
    pad = jnp.zeros((w.shape[0], _KIWI - IDX_DIM - N_IDX_HEADS), w.dtype)
    return jnp.concatenate([q, k, v, qi, ki, wi, pad, bg, cg, xin, ga, gb], axis=1).astype(BF16)


def kernel(x_prompt, x_sample, c_prompt, c_sample, cache_k, cache_v, cache_kidx, state_conv, page_table,
           rel_bias, w_ada, b_ada, w_in, conv_w, conv_b, w_o_attn, w_o_conv, w_out, ln1_g, ln1_b,
           ln2_g, ln2_b, peer_wq, peer_k1, peer_k2, peer_u, peer_v):
    depth = w_in.shape[0]
    alpha = (2 * depth) ** 0.25
    bp, sp, d = x_prompt.shape
    bs, ts, _ = x_sample.shape
    past = page_table.shape[1] * PAGE_SIZE
    ns = bs * ts

    xp, xs = x_prompt, x_sample.reshape(ns, d)
    outs = [[] for _ in range(8)]
    for l in range(depth):
        c_all = jnp.concatenate([c_prompt, c_sample], axis=0)
        c_all = jnp.pad(c_all, ((0, (-c_all.shape[0]) % 16), (0, 0)))
        mod = _ada(c_all, w_ada[l], b_ada[l])[:bp + bs]
        sh1, sc1, g1, sh2, sc2, g2 = jnp.split(mod, 6, axis=-1)
        as_p = lambda a: a[:bp].reshape(bp, 1, d)
        as_s = lambda a: jnp.repeat(a[bp:], ts, axis=0).reshape(1, ns, d)

        w_all = _pack_w_in(w_in[l])
        woa, woc, wout = w_o_attn[l].astype(BF16), w_o_conv[l].astype(BF16), w_out[l].astype(BF16)
        wq = peer_wq[l].astype(BF16)
        k1, k2 = peer_k1[l].astype(BF16), peer_k2[l].astype(BF16)
        u_b = peer_u[l].astype(BF16)
        vT_b = _transpose_cast(peer_v[l])
        lg1, lb1 = ln1_g[l].reshape(1, d), ln1_b[l].reshape(1, d)
        lg2, lb2 = ln2_g[l].reshape(1, d), ln2_b[l].reshape(1, d)

        (qT, k_p, v_p, kb_p, vTb_p, qiT, ki_p, kib_p, wiT, oconv_p, ga_p, gb_p, ulast_p) = _inproj_prompt(
            xp, as_p(sc1), as_p(sh1), w_all, conv_w[l], conv_b[l])
        oattn_p = _attn_prompt(rel_bias, qT, qiT, wiT, kb_p, vTb_p, kib_p)
        x1_p, h2T_p, pqT_p = _merge(alpha, xp.reshape(bp * sp, d), oattn_p.reshape(bp * sp, ATTN_WIDTH),
                                    oconv_p.reshape(bp * sp, D_CONV), ga_p.reshape(bp * sp, d),
                                    gb_p.reshape(bp * sp, d), as_p(g1), as_p(sc2), as_p(sh2),
                                    woa, woc, wout, wq, lg1, lb1, sp)
        tabs_p = _peer_select(pqT_p, k1, k2)
        y_p = _peer_dense(alpha, h2T_p, u_b, vT_b, tabs_p, x1_p, as_p(g2), lg2, lb2, sp)
        xp = y_p.reshape(bp, sp, d)

        st = state_conv[l]
        st0 = jnp.repeat(st[:, 0], ts, axis=0)
        st1 = jnp.repeat(st[:, 1], ts, axis=0)
        (q_s, k_s, v_s, qi_s, kiwi_s, oconv_s, ga_s, gb_s, u_s) = _inproj_sample(
            xs, as_s(sc1)[0], as_s(sh1)[0], w_all, conv_w[l], conv_b[l], st0, st1, ts)
        ki_s = kiwi_s[:, :IDX_DIM]
        wi_s = kiwi_s[:, IDX_DIM:IDX_DIM + N_IDX_HEADS]
        q_g = q_s.reshape(bs, ts, N_KV_HEADS, GROUP, HEAD_DIM).transpose(0, 2, 3, 1, 4).reshape(
            bs, N_KV_HEADS, GROUP * ts, HEAD_DIM).astype(BF16)
        qi_r = qi_s.reshape(bs, ts, N_IDX_HEADS, IDX_DIM).transpose(0, 2, 1, 3).reshape(
            bs, N_IDX_HEADS * ts, IDX_DIM).astype(BF16)
        wi_r = jnp.broadcast_to(wi_s.reshape(bs, ts, N_IDX_HEADS).transpose(0, 2, 1).reshape(
            bs, N_IDX_HEADS * ts, 1), (bs, N_IDX_HEADS * ts, LANES))
        by_head = lambda a: a.reshape(bs, ts, N_KV_HEADS, HEAD_DIM).transpose(0, 2, 1, 3)
        o_s = _attn_sample(l, page_table, rel_bias, q_g, qi_r, wi_r, by_head(k_s), by_head(v_s),
                           ki_s.reshape(bs, ts, IDX_DIM), cache_k, cache_v, cache_kidx, past)
        oattn_s = o_s.reshape(bs, N_KV_HEADS, GROUP, ts, HEAD_DIM).transpose(0, 3, 1, 2, 4).reshape(
            ns, ATTN_WIDTH).astype(BF16)
        x1_s, h2T_s, pqT_s = _merge(alpha, xs, oattn_s, oconv_s, ga_s, gb_s, as_s(g1), as_s(sc2), as_s(sh2),
                                    woa, woc, wout, wq, lg1, lb1, ns)
        tabs_s = _peer_select(pqT_s, k1, k2)
        xs = _peer_dense(alpha, h2T_s, u_b, vT_b, tabs_s, x1_s, as_s(g2), lg2, lb2, ns)

        for lst, val in zip(outs, (
                k_p.reshape(bp, sp, N_KV_HEADS, HEAD_DIM), v_p.reshape(bp, sp, N_KV_HEADS, HEAD_DIM), ki_p,
                ulast_p[:, 8 - (CONV_WIDTH - 1):],
                k_s.reshape(bs, ts, N_KV_HEADS, HEAD_DIM), v_s.reshape(bs, ts, N_KV_HEADS, HEAD_DIM),
                ki_s.reshape(bs, ts, IDX_DIM),
                u_s.reshape(bs, ts, D_CONV)[:, ts - (CONV_WIDTH - 1):])):
            lst.append(val)

    return (xp, xs.reshape(bs, ts, d)) + tuple(jnp.stack(o) for o in outs)
```

```python
import functools
import math

import jax
import jax.numpy as jnp
from jax import lax
from jax.experimental import pallas as pl
from jax.experimental.pallas import tpu as pltpu

F32 = jnp.float32
BF16 = jnp.bfloat16
I32 = jnp.int32

N_HEADS = 8
N_KV_HEADS = 2
HEAD_DIM = 64
GROUP = N_HEADS // N_KV_HEADS
ATTN_WIDTH = N_HEADS * HEAD_DIM
KV_WIDTH = N_KV_HEADS * HEAD_DIM
ATTN_SCALE = HEAD_DIM ** -0.5
N_IDX_HEADS = 4
IDX_DIM = 64
IDX_SCALE = (IDX_DIM * N_IDX_HEADS) ** -0.5
TOPK_MAX = 256
N_BUCKETS = 32
MAX_DISTANCE = 128
D_CONV = 512
CONV_WIDTH = 3
N_KEYS = 128
PEER_HEADS = 8
PEER_HALF = 64
PEER_TOPK = 16
LN_EPS = 1e-5
PAGE_SIZE = 128

LANES = 128
BF16_ROWS = 16
VMEM_LIMIT = 56 * 1024 * 1024

_GELU_C1 = math.sqrt(2.0 / math.pi)
_GELU_C2 = _GELU_C1 * 0.044715

NEG_BIG = -1e30

_KIWI = LANES
_OFF_Q = 0
_OFF_K = _OFF_Q + ATTN_WIDTH
_OFF_V = _OFF_K + KV_WIDTH
_OFF_QI = _OFF_V + KV_WIDTH
_OFF_KIWI = _OFF_QI + N_IDX_HEADS * IDX_DIM
_OFF_BG = _OFF_KIWI + _KIWI
_OFF_CG = _OFF_BG + D_CONV
_OFF_XIN = _OFF_CG + D_CONV
_OFF_GA = _OFF_XIN + D_CONV


def _cparams(*sem):
    return pltpu.CompilerParams(dimension_semantics=sem, vmem_limit_bytes=VMEM_LIMIT)


def _dot(a, b):
    return jnp.dot(a, b, preferred_element_type=F32)


def _dot_nt(a, b):
    return lax.dot_general(a, b, (((1,), (1,)), ((), ())), preferred_element_type=F32)


_COARSE_STEPS = 10


def _kth_largest(stats, count_ge, k, n_valid, v_min, v_max):
    enough = n_valid >= k
    lo0 = jnp.where(enough, v_min, -jnp.inf)
    hi0 = jnp.where(enough, v_max, -jnp.inf)

    def cond(state):
        lo, hi = state
        return jnp.max(jnp.where(lo < hi, 1.0, 0.0)) > 0.0

    def midpoint(lo, hi):
        mid = lo + (hi - lo) * 0.5
        return jnp.where(mid > lo, mid, hi)

    def coarse_step(_, state):
        lo, hi = state
        mid = midpoint(lo, hi)
        cnt = count_ge(mid)
        active = lo < hi
        return (jnp.where(active & (cnt >= k), mid, lo), jnp.where(active & (cnt < k), mid, hi))

    def step(state):
        lo, hi = state
        cnt, above, below = stats(midpoint(lo, hi))
        active = lo < hi
        return (jnp.where(active & (cnt >= k), above, lo), jnp.where(active & (cnt < k), below, hi))

    state = lax.fori_loop(0, _COARSE_STEPS, coarse_step, (lo0, hi0))
    lo, _ = lax.while_loop(cond, lambda state: step(step(state)), state)
    return lo


def _t5_bias(dist, relb_ref, h):
    n = jnp.maximum(dist, 0)
    max_exact = N_BUCKETS // 2
    nf = jnp.maximum(n, 1).astype(F32)
    large = max_exact + jnp.floor(jnp.log(nf / max_exact) / math.log(MAX_DISTANCE / max_exact)
                                  * (N_BUCKETS - max_exact)).astype(I32)
    large = jnp.minimum(large, N_BUCKETS - 1)
    bucket = jnp.where(n < max_exact, n, large)
    out = jnp.zeros(dist.shape, F32)
    for b in range(N_BUCKETS):
        out = jnp.where(bucket == b, relb_ref[b, h], out)
    return out


def _layer_norm(x, g, b):
    mu = jnp.mean(x, axis=-1, keepdims=True)
    var = jnp.mean(jnp.square(x - mu), axis=-1, keepdims=True)
    return (x - mu) * lax.rsqrt(var + LN_EPS) * g + b


def _ada_kernel(c_ref, w_ref, b_ref, o_ref):
    o_ref[...] = _dot(c_ref[...].astype(BF16), w_ref[...].astype(BF16)) + b_ref[...]


def _ada(c, w, b):
    m, d = c.shape
    n = w.shape[1]
    tn = n // 4
    return pl.pallas_call(
        _ada_kernel,
        grid=(n // tn,),
        in_specs=[pl.BlockSpec((m, d), lambda j: (0, 0)),
                  pl.BlockSpec((d, tn), lambda j: (0, j)),
                  pl.BlockSpec((1, tn), lambda j: (0, j))],
        out_specs=pl.BlockSpec((m, tn), lambda j: (0, j)),
        out_shape=jax.ShapeDtypeStruct((m, n), F32),
        compiler_params=_cparams("arbitrary"),
        name="ada_mod",
    )(c, w, b.reshape(1, n))


def _conv_out(bg, u, u1, u2, cw_ref, cb_ref):
    y = cb_ref[...] + u2 * cw_ref[0:1, :] + u1 * cw_ref[1:2, :] + u * cw_ref[2:3, :]
    return bg * y


def _inproj_prompt_kernel(x_ref, sc_ref, sh_ref, w_ref, cw_ref, cb_ref,
                          qT_ref, k_ref, v_ref, kb_ref, vTb_ref, qiT_ref, ki_ref, kib_ref, wiT_ref,
                          oconv_ref, ga_ref, gb_ref, ulast_ref, uprev_scr):
    s = pl.program_id(1)
    tm = x_ref.shape[1]
    h = (x_ref[0] * (1.0 + sc_ref[0]) + sh_ref[0]).astype(BF16)

    def proj(a, width):
        return _dot(h, w_ref[:, a:a + width])

    qT_ref[0] = proj(_OFF_Q, ATTN_WIDTH).T.astype(BF16)
    k = proj(_OFF_K, KV_WIDTH)
    k_ref[0] = k
    kb_ref[0] = k.astype(BF16)
    v = proj(_OFF_V, KV_WIDTH)
    v_ref[0] = v
    for i in range(tm // LANES):
        vTb_ref[0, i] = v[i * LANES:(i + 1) * LANES].T.astype(BF16)
    qiT_ref[0] = proj(_OFF_QI, N_IDX_HEADS * IDX_DIM).T.astype(BF16)
    kiwi = proj(_OFF_KIWI, _KIWI)
    ki_ref[0] = kiwi[:, :IDX_DIM]
    kib_ref[0] = kiwi[:, :IDX_DIM].astype(BF16)
    wiT_ref[0] = kiwi.T[IDX_DIM:IDX_DIM + 8]

    @pl.when(s == 0)
    def _():
        uprev_scr[...] = jnp.zeros(uprev_scr.shape, F32)

    bg = proj(_OFF_BG, D_CONV)
    u = proj(_OFF_CG, D_CONV) * proj(_OFF_XIN, D_CONV)
    row = lax.broadcasted_iota(I32, u.shape, 0)
    up = uprev_scr[...]
    u1 = jnp.where(row < 1, pltpu.roll(up, 1, 0), pltpu.roll(u, 1, 0))
    u2 = jnp.where(row < 2, pltpu.roll(up, 2, 0), pltpu.roll(u, 2, 0))
    oconv_ref[0] = _conv_out(bg, u, u1, u2, cw_ref, cb_ref).astype(BF16)
    uprev_scr[...] = u
    ulast_ref[0] = u[tm - 8:, :]

    ga_ref[0] = proj(_OFF_GA, w_ref.shape[0])
    gb_ref[0] = proj(_OFF_GA + w_ref.shape[0], w_ref.shape[0])


def _inproj_prompt(x, sc, sh, w_all, conv_w, conv_b):
    b, s, d = x.shape
    tm = min(512, s)
    nkb = tm // LANES
    row = lambda width, dt: jax.ShapeDtypeStruct((b, s, width), dt)
    out_shape = (
        jax.ShapeDtypeStruct((b, ATTN_WIDTH, s), BF16),
        row(KV_WIDTH, F32), row(KV_WIDTH, F32), row(KV_WIDTH, BF16),
        jax.ShapeDtypeStruct((b, s // LANES, KV_WIDTH, LANES), BF16),
        jax.ShapeDtypeStruct((b, N_IDX_HEADS * IDX_DIM, s), BF16),
        row(IDX_DIM, F32), row(IDX_DIM, BF16),
        jax.ShapeDtypeStruct((b, 8, s), F32),
        row(D_CONV, BF16),
        row(d, F32), row(d, F32),
        jax.ShapeDtypeStruct((b, 8, D_CONV), F32),
    )
    tile = lambda width: pl.BlockSpec((1, tm, width), lambda i, j: (i, j, 0))
    tileT = lambda rows: pl.BlockSpec((1, rows, tm), lambda i, j: (i, 0, j))
    out_specs = (
        tileT(ATTN_WIDTH), tile(KV_WIDTH), tile(KV_WIDTH), tile(KV_WIDTH),
        pl.BlockSpec((1, nkb, KV_WIDTH, LANES), lambda i, j: (i, j, 0, 0)),
        tileT(N_IDX_HEADS * IDX_DIM), tile(IDX_DIM), tile(IDX_DIM), tileT(8),
        tile(D_CONV), tile(d), tile(d),
        pl.BlockSpec((1, 8, D_CONV), lambda i, j: (i, 0, 0)),
    )
    vec = pl.BlockSpec((1, 1, d), lambda i, j: (i, 0, 0))
    return pl.pallas_call(
        _inproj_prompt_kernel,
        grid=(b, s // tm),
        in_specs=[tile(d), vec, vec,
                  pl.BlockSpec(w_all.shape, lambda i, j: (0, 0)),
                  pl.BlockSpec(conv_w.shape, lambda i, j: (0, 0)),
                  pl.BlockSpec((1, D_CONV), lambda i, j: (0, 0))],
        out_specs=out_specs,
        out_shape=out_shape,
        scratch_shapes=[pltpu.VMEM((tm, D_CONV), F32)],
        compiler_params=_cparams("arbitrary", "arbitrary"),
        name="inproj_prompt",
    )(x, sc, sh, w_all, conv_w, conv_b.reshape(1, D_CONV))


def _inproj_sample_kernel(t_seq, x_ref, sc_ref, sh_ref, w_ref, cw_ref, cb_ref, s0_ref, s1_ref,
                          q_ref, k_ref, v_ref, qi_ref, kiwi_ref, oconv_ref, ga_ref, gb_ref, u_ref):
    h = (x_ref[...] * (1.0 + sc_ref[...]) + sh_ref[...]).astype(BF16)

    def proj(a, width):
        return _dot(h, w_ref[:, a:a + width])

    q_ref[...] = proj(_OFF_Q, ATTN_WIDTH)
    k_ref[...] = proj(_OFF_K, KV_WIDTH)
    v_ref[...] = proj(_OFF_V, KV_WIDTH)
    qi_ref[...] = proj(_OFF_QI, N_IDX_HEADS * IDX_DIM)
    kiwi_ref[...] = proj(_OFF_KIWI, _KIWI)
    bg = proj(_OFF_BG, D_CONV)
    u = proj(_OFF_CG, D_CONV) * proj(_OFF_XIN, D_CONV)
    t = lax.broadcasted_iota(I32, u.shape, 0) % t_seq
    u1 = jnp.where(t == 0, s1_ref[...], pltpu.roll(u, 1, 0))
    u2 = jnp.where(t == 0, s0_ref[...], jnp.where(t == 1, s1_ref[...], pltpu.roll(u, 2, 0)))
    oconv_ref[...] = _conv_out(bg, u, u1, u2, cw_ref, cb_ref).astype(BF16)
    u_ref[...] = u
    ga_ref[...] = proj(_OFF_GA, w_ref.shape[0])
    gb_ref[...] = proj(_OFF_GA + w_ref.shape[0], w_ref.shape[0])


def _inproj_sample(x, sc_rows, sh_rows, w_all, conv_w, conv_b, st0_rows, st1_rows, t_seq):
    n, d = x.shape
    full = lambda a: pl.BlockSpec(a.shape, lambda i: (0,) * a.ndim)
    o = lambda width, dt: jax.ShapeDtypeStruct((n, width), dt)
    out_shape = (o(ATTN_WIDTH, F32), o(KV_WIDTH, F32), o(KV_WIDTH, F32), o(N_IDX_HEADS * IDX_DIM, F32),
                 o(_KIWI, F32), o(D_CONV, BF16), o(d, F32), o(d, F32), o(D_CONV, F32))
    args = (x, sc_rows, sh_rows, w_all, conv_w, conv_b.reshape(1, D_CONV), st0_rows, st1_rows)
    return pl.pallas_call(
        functools.partial(_inproj_sample_kernel, t_seq),
        grid=(1,),
        in_specs=[full(a) for a in args],
        out_specs=tuple(pl.BlockSpec(s.shape, lambda i: (0, 0)) for s in out_shape),
        out_shape=out_shape,
        compiler_params=_cparams("arbitrary"),
        name="inproj_sample",
    )(*args)


def _attn_prompt_kernel(n_sel, relb_ref, qT_ref, qiT_ref, wiT_ref, kb_ref, vTb_ref, kib_ref,
                        o_ref, bias_scr, key_scr):
    qb = pl.program_id(1)
    blk = LANES
    kblk = 2 * blk
    ntrip = qb // 2 + 1

    @pl.when((pl.program_id(0) == 0) & (qb == 0))
    def _():
        j = lax.broadcasted_iota(I32, (blk, blk), 0)
        i = lax.broadcasted_iota(I32, (blk, blk), 1)
        for delta in range(3):
            dist = i - j + blk * delta
            for h in range(N_HEADS):
                g, r = divmod(h, GROUP)
                bias_scr[delta, g, :, r * blk:(r + 1) * blk] = _t5_bias(dist, relb_ref, h)

    row = lax.broadcasted_iota(I32, (kblk, blk), 0)
    lane = lax.broadcasted_iota(I32, (kblk, blk), 1)

    def valid_mask(t):
        return (t * kblk + row) <= (qb * blk + lane)

    def key_rows(t):
        return pl.ds(pl.multiple_of(t * kblk, kblk), kblk)

    qiT = jnp.concatenate([qiT_ref[0, h * IDX_DIM:(h + 1) * IDX_DIM, :] for h in range(N_IDX_HEADS)], axis=1)
    wi = [wiT_ref[0, h:h + 1, :] for h in range(N_IDX_HEADS)]
    fold = lambda a: a.reshape(kblk // 8, 8, blk)
    tall = lambda a: jnp.sum(a, axis=0, keepdims=True)
    part0 = (jnp.zeros((8, blk), F32), jnp.full((8, blk), jnp.inf, F32), jnp.full((8, blk), -jnp.inf, F32))

    def score_blk(t, carry):
        cnt, lo, hi = carry
        d = _dot(kib_ref[0, key_rows(t), :], qiT)
        s = jnp.maximum(d[:, 0:blk], 0.0) * wi[0]
        for h in range(1, N_IDX_HEADS):
            s = s + jnp.maximum(d[:, h * blk:(h + 1) * blk], 0.0) * wi[h]
        s = s * IDX_SCALE
        valid = valid_mask(t)
        key_scr[key_rows(t), :] = jnp.where(valid, s, -jnp.inf)
        return (cnt + jnp.sum(fold(jnp.where(valid, 1.0, 0.0)), axis=0),
                jnp.minimum(lo, jnp.min(fold(jnp.where(valid, s, jnp.inf)), axis=0)),
                jnp.maximum(hi, jnp.max(fold(jnp.where(valid, s, -jnp.inf)), axis=0)))

    cnt, lo, hi = lax.fori_loop(0, ntrip, score_blk, part0)
    n_valid = tall(cnt)
    v_min = jnp.min(lo, axis=0, keepdims=True)
    v_max = jnp.max(hi, axis=0, keepdims=True)

    def stats(mid):
        def body(t, carry):
            cnt, above, below = carry
            x = key_scr[key_rows(t), :]
            ge = x >= mid
            return (cnt + jnp.sum(fold(jnp.where(ge, 1.0, 0.0)), axis=0),
                    jnp.minimum(above, jnp.min(fold(jnp.where(ge, x, jnp.inf)), axis=0)),
                    jnp.maximum(below, jnp.max(fold(jnp.where(ge, -jnp.inf, x)), axis=0)))
        cnt, above, below = lax.fori_loop(0, ntrip, body, part0)
        return tall(cnt), jnp.min(above, axis=0, keepdims=True), jnp.max(below, axis=0, keepdims=True)

    def count(pred):
        def body(t, acc):
            return acc + jnp.sum(fold(jnp.where(pred(key_scr[key_rows(t), :]), 1.0, 0.0)), axis=0)
        return tall(lax.fori_loop(0, ntrip, body, jnp.zeros((8, blk), F32)))

    thr = _kth_largest(stats, lambda mid: count(lambda x: x >= mid), float(n_sel), n_valid, v_min, v_max)

    need = n_sel - count(lambda x: x > thr)
    ltri = (lax.broadcasted_iota(I32, (kblk, kblk), 1) <= lax.broadcasted_iota(I32, (kblk, kblk), 0)).astype(BF16)

    def select_mask(t, ties_before):
        kv = key_scr[key_rows(t), :]
        eq = kv == thr
        prefix = _dot(ltri, eq.astype(BF16)) + ties_before
        sel = ((kv > thr) | (eq & (prefix <= need))) & valid_mask(t)
        return jnp.where(sel, 0.0, NEG_BIG), prefix[kblk - 1:kblk, :]

    zero = jnp.zeros((HEAD_DIM, blk), BF16)
    scale = jnp.asarray(ATTN_SCALE, BF16)
    qTg = []
    for g in range(N_KV_HEADS):
        cols = []
        for r in range(GROUP):
            base = g * GROUP * HEAD_DIM + r * HEAD_DIM
            parts = [zero] * N_KV_HEADS
            parts[g] = qT_ref[0, base:base + HEAD_DIM, :] * scale
            cols.append(jnp.concatenate(parts, axis=0))
        qTg.append(jnp.concatenate(cols, axis=1))
    width = GROUP * blk

    def att_blk(t, carry):
        kmat = kb_ref[0, key_rows(t), :]
        mask1, ties = select_mask(t, carry[N_KV_HEADS])
        mask = jnp.concatenate([mask1] * GROUP, axis=1)
        d0 = jnp.clip(qb - 2 * t, 0, 2)
        d1 = jnp.clip(qb - 2 * t - 1, 0, 2)
        new = []
        for g in range(N_KV_HEADS):
            m, l, acc = carry[g]
            bias = jnp.concatenate([bias_scr[d0, g], bias_scr[d1, g]], axis=0)
            lg = _dot(kmat, qTg[g]) + bias + mask
            m_new = jnp.maximum(m, jnp.max(lg, axis=0, keepdims=True))
            p = jnp.exp(lg - m_new)
            alpha = jnp.exp(m - m_new)
            l = alpha * l + jnp.sum(p, axis=0, keepdims=True)
            rows = slice(g * HEAD_DIM, (g + 1) * HEAD_DIM)
            vt = jnp.concatenate([vTb_ref[0, 2 * t, rows, :], vTb_ref[0, 2 * t + 1, rows, :]], axis=1)
            acc = alpha * acc + _dot(vt, p.astype(BF16))
            new.append((m_new, l, acc))
        return tuple(new) + (ties,)

    init = (jnp.full((1, width), NEG_BIG, F32), jnp.zeros((1, width), F32), jnp.zeros((HEAD_DIM, width), F32))
    res = lax.fori_loop(0, ntrip, att_blk, (init,) * N_KV_HEADS + (jnp.zeros((1, blk), F32),))
    outs = []
    for g in range(N_KV_HEADS):
        _, l, acc = res[g]
        og = acc / l
        outs += [og[:, r * blk:(r + 1) * blk] for r in range(GROUP)]
    o_ref[0] = jnp.concatenate(outs, axis=0).T.astype(BF16)


def _attn_prompt(rel_bias, qT, qiT, wiT, kb, vTb, kib):
    b, _, s = qT.shape
    n_sel = min(TOPK_MAX, s // 4)
    blk = LANES
    per_b = lambda a: pl.BlockSpec((1,) + a.shape[1:], lambda i, j: (i,) + (0,) * (a.ndim - 1))
    qtile = lambda rows: pl.BlockSpec((1, rows, blk), lambda i, j: (i, 0, j))
    return pl.pallas_call(
        functools.partial(_attn_prompt_kernel, n_sel),
        grid=(b, s // blk),
        in_specs=[pl.BlockSpec(memory_space=pltpu.SMEM),
                  qtile(ATTN_WIDTH), qtile(N_IDX_HEADS * IDX_DIM), qtile(8),
                  per_b(kb), per_b(vTb), per_b(kib)],
        out_specs=pl.BlockSpec((1, blk, ATTN_WIDTH), lambda i, j: (i, j, 0)),
        out_shape=jax.ShapeDtypeStruct((b, s, ATTN_WIDTH), BF16),
        scratch_shapes=[pltpu.VMEM((3, N_KV_HEADS, blk, GROUP * blk), F32),
                        pltpu.VMEM((s, blk), F32)],
        compiler_params=_cparams("arbitrary", "arbitrary"),
        name="attn_prompt",
    )(rel_bias, qT, qiT, wiT, kb, vTb, kib)


def _attn_sample_kernel(n_sel, past, t_seq, pt_ref, relb_ref, qpad_ref, qi_ref, wi_ref,
                        knew_ref, vnew_ref, kinew_ref, ck_hbm, cv_hbm, cki_hbm,
                        o_ref, kbuf, vbuf, kibuf, sem):
    b = pl.program_id(0)
    nb = pl.num_programs(0)
    n_pages = past // PAGE_SIZE
    lp = kbuf.shape[1]
    blk = LANES
    nblk = lp // blk
    slot = b % 2

    def page_copies(seq, sl, p):
        phys = pt_ref[seq, p]
        rows = pl.ds(pl.multiple_of(p * PAGE_SIZE, PAGE_SIZE), PAGE_SIZE)
        return (pltpu.make_async_copy(ck_hbm.at[phys], kbuf.at[sl, rows, :], sem.at[sl, 0]),
                pltpu.make_async_copy(cv_hbm.at[phys], vbuf.at[sl, rows, :], sem.at[sl, 1]),
                pltpu.make_async_copy(cki_hbm.at[phys], kibuf.at[sl, rows, :], sem.at[sl, 2]))

    def start_all(seq, sl):
        def body(p, c):
            for cp in page_copies(seq, sl, p):
                cp.start()
            return c
        lax.fori_loop(0, n_pages, body, 0)

    def wait_all(seq, sl):
        def body(p, c):
            for cp in page_copies(seq, sl, p):
                cp.wait()
            return c
        lax.fori_loop(0, n_pages, body, 0)

    @pl.when(b == 0)
    def _():
        for sl in range(2):
            kbuf[sl, past:, :] = jnp.zeros((blk, KV_WIDTH), F32)
            vbuf[sl, past:, :] = jnp.zeros((blk, KV_WIDTH), F32)
            kibuf[sl, past:, :] = jnp.zeros((blk, IDX_DIM), F32)
        start_all(0, 0)

    @pl.when(b + 1 < nb)
    def _():
        start_all(b + 1, 1 - slot)

    kbuf[slot, past:past + t_seq, :] = knew_ref[0]
    vbuf[slot, past:past + t_seq, :] = vnew_ref[0]
    kibuf[slot, past:past + t_seq, :] = kinew_ref[0]
    wait_all(b, slot)

    d = _dot_nt(qi_ref[0], kibuf[slot].astype(BF16))
    s = jnp.maximum(d[0:t_seq], 0.0) * wi_ref[0, 0:t_seq, 0:1]
    for h in range(1, N_IDX_HEADS):
        s = s + jnp.maximum(d[h * t_seq:(h + 1) * t_seq], 0.0) * wi_ref[0, h * t_seq:(h + 1) * t_seq, 0:1]
    kpos = lax.broadcasted_iota(I32, (t_seq, lp), 1)
    qpos = past + lax.broadcasted_iota(I32, (t_seq, lp), 0)
    valid = kpos <= qpos
    s = s * IDX_SCALE
    key = jnp.where(valid, s, -jnp.inf)

    def count(pred):
        return jnp.sum(jnp.where(pred, 1.0, 0.0), axis=1, keepdims=True)

    def stats(mid):
        ge = key >= mid
        return (count(ge), jnp.min(jnp.where(ge, key, jnp.inf), axis=1, keepdims=True),
                jnp.max(jnp.where(ge, -jnp.inf, key), axis=1, keepdims=True))

    thr = _kth_largest(stats, lambda mid: count(key >= mid), float(n_sel), count(valid),
                       jnp.min(jnp.where(valid, s, jnp.inf), axis=1, keepdims=True),
                       jnp.max(key, axis=1, keepdims=True))
    need = n_sel - count(key > thr)
    eq = key == thr
    eqf = jnp.where(eq, 1.0, 0.0)
    utri = (lax.broadcasted_iota(I32, (blk, blk), 0) <= lax.broadcasted_iota(I32, (blk, blk), 1)).astype(BF16)
    carry = jnp.zeros((t_seq, 1), F32)
    ranks = []
    for kb in range(nblk):
        e = eqf[:, kb * blk:(kb + 1) * blk]
        ranks.append(_dot(e.astype(BF16), utri) + carry)
        carry = carry + jnp.sum(e, axis=1, keepdims=True)
    prefix = jnp.concatenate(ranks, axis=1)
    sel = jnp.where(((key > thr) | (eq & (prefix <= need))) & valid, 1.0, 0.0)

    lg = _dot_nt(qpad_ref[0], kbuf[slot].astype(BF16)) * ATTN_SCALE
    far = lp - 2 * blk
    near_dist = (past + lax.broadcasted_iota(I32, (t_seq, 2 * blk), 0)
                 - (far + lax.broadcasted_iota(I32, (t_seq, 2 * blk), 1)))
    bias_rows = []
    for h in range(N_HEADS):
        far_bias = jnp.full((t_seq, far), relb_ref[N_BUCKETS - 1, h], F32)
        bias_rows.append(jnp.concatenate([far_bias, _t5_bias(near_dist, relb_ref, h)], axis=1))
    lg = lg + jnp.concatenate(bias_rows, axis=0)
    selh = jnp.concatenate([sel] * N_HEADS, axis=0) > 0.0
    m = jnp.max(jnp.where(selh, lg, NEG_BIG), axis=1, keepdims=True)
    p = jnp.where(selh, jnp.exp(lg - m), 0.0)
    l = jnp.sum(p, axis=1, keepdims=True)
    o = _dot(p.astype(BF16), vbuf[slot].astype(BF16))
    o_ref[0] = o / l


def _attn_sample(page_table, rel_bias, qpad, qi_r, wi_r, k_new, v_new, ki_new, cache_k, cache_v, cache_ki,
                 past):
    nb, t_seq, _ = k_new.shape
    n_sel = min(TOPK_MAX, (past + t_seq) // 4)
    lp = past + LANES
    rows = N_HEADS * t_seq
    per_b = lambda a: pl.BlockSpec((1,) + a.shape[1:], lambda i, pt: (i,) + (0,) * (a.ndim - 1))
    hbm = pl.BlockSpec(memory_space=pl.ANY)
    grid_spec = pltpu.PrefetchScalarGridSpec(
        num_scalar_prefetch=1,
        grid=(nb,),
        in_specs=[pl.BlockSpec(memory_space=pltpu.SMEM),
                  per_b(qpad), per_b(qi_r), per_b(wi_r), per_b(k_new), per_b(v_new), per_b(ki_new),
                  hbm, hbm, hbm],
        out_specs=pl.BlockSpec((1, rows, KV_WIDTH), lambda i, pt: (i, 0, 0)),
        scratch_shapes=[pltpu.VMEM((2, lp, KV_WIDTH), F32),
                        pltpu.VMEM((2, lp, KV_WIDTH), F32),
                        pltpu.VMEM((2, lp, IDX_DIM), F32),
                        pltpu.SemaphoreType.DMA((2, 3))],
    )
    return pl.pallas_call(
        functools.partial(_attn_sample_kernel, n_sel, past, t_seq),
        grid_spec=grid_spec,
        out_shape=jax.ShapeDtypeStruct((nb, rows, KV_WIDTH), F32),
        compiler_params=_cparams("arbitrary"),
        name="attn_sample",
    )(page_table, rel_bias, qpad, qi_r, wi_r, k_new, v_new, ki_new, cache_k, cache_v, cache_ki)


def _merge_kernel(alpha, x_ref, oa_ref, oc_ref, ga_ref, gb_ref, g1_ref, sc2_ref, sh2_ref,
                  woa_ref, woc_ref, wout_ref, wq_ref, lng_ref, lnb_ref,
                  x1_ref, h2T_ref, pqT_ref):
    ta = _dot(oa_ref[...], woa_ref[...])
    tc = _dot(oc_ref[...], woc_ref[...])
    merged = jax.nn.sigmoid(ga_ref[...]) * ta + jax.nn.sigmoid(gb_ref[...]) * tc
    out = _dot(merged.astype(BF16), wout_ref[...])
    x1 = _layer_norm(alpha * x_ref[...] + g1_ref[0] * out, lng_ref[...], lnb_ref[...])
    x1_ref[...] = x1
    h2 = x1 * (1.0 + sc2_ref[0]) + sh2_ref[0]
    h2b = h2.astype(BF16)
    h2T_ref[...] = h2.T.astype(BF16)
    pqT_ref[...] = _dot(h2b, wq_ref[...]).T.astype(BF16)


def _merge(alpha, x, oattn, oconv, ga, gb, g1, sc2, sh2, woa, woc, wout, wq, ln_g, ln_b, rows_per_vec):
    n, d = x.shape
    tm = min(512, rows_per_vec) if g1.shape[1] == 1 else g1.shape[1]
    tiles_per_vec = rows_per_vec // tm
    tile = lambda width: pl.BlockSpec((tm, width), lambda i: (i, 0))
    vec = pl.BlockSpec((1,) + g1.shape[1:], lambda i: (i // tiles_per_vec, 0, 0))
    full = lambda a: pl.BlockSpec(a.shape, lambda i: (0,) * a.ndim)
    pq_w = wq.shape[1]
    return pl.pallas_call(
        functools.partial(_merge_kernel, alpha),
        grid=(n // tm,),
        in_specs=[tile(d), tile(ATTN_WIDTH), tile(D_CONV), tile(d), tile(d), vec, vec, vec,
                  full(woa), full(woc), full(wout), full(wq), full(ln_g), full(ln_b)],
        out_specs=(tile(d), pl.BlockSpec((d, tm), lambda i: (0, i)), pl.BlockSpec((pq_w, tm), lambda i: (0, i))),
        out_shape=(jax.ShapeDtypeStruct((n, d), F32), jax.ShapeDtypeStruct((d, n), BF16),
                   jax.ShapeDtypeStruct((pq_w, n), BF16)),
        compiler_params=_cparams("arbitrary"),
        name="merge_ln1_peerq",
    )(x, oattn, oconv, ga, gb, g1, sc2, sh2, woa, woc, wout, wq, ln_g, ln_b)


def _top16(s, v_scr):
    iota = lax.broadcasted_iota(I32, s.shape, 0)

    def body(r, carry):
        x, rank = carry
        m = jnp.max(x, axis=0, keepdims=True)
        idx = jnp.min(jnp.where(x == m, iota, N_KEYS), axis=0, keepdims=True)
        hit = iota == idx
        v_scr[pl.ds(r, 1), :] = m
        return jnp.where(hit, -jnp.inf, x), jnp.where(hit, lax.convert_element_type(r, F32), rank)

    _, rank = lax.fori_loop(0, PEER_TOPK, body, (s, jnp.full(s.shape, float(PEER_TOPK), F32)))
    return rank


def _pair_candidates(v1, v2):
    return jnp.concatenate([v1[0:1] + v2] + [v1[a:a + 1] + v2[0:8] for a in range(1, 8)]
                           + [v1[8:16] + v2[0:1]], axis=0)


def _peer_select_kernel(pqT_ref, k1_ref, k2_ref, cnt1_ref, w1_ref, rank2_ref, e2_ref,
                        v_scr, r_scr, vx1_scr, vx2_scr):
    tn = pqT_ref.shape[1]
    k = PEER_TOPK
    tiles = (N_KEYS // BF16_ROWS, BF16_ROWS, tn)

    def scores(h):
        base = h * 2 * PEER_HALF
        return (_dot(k1_ref[...], pqT_ref[base:base + PEER_HALF, :]),
                _dot(k2_ref[...], pqT_ref[base + PEER_HALF:base + 2 * PEER_HALF, :]))

    def emit(h, s1, s2, rank1, rank2, cnt_rows, z, top1, top2):
        cnt1 = jnp.zeros(s1.shape, F32)
        for a in range(k):
            cnt1 = jnp.where(rank1 == float(a), cnt_rows[a], cnt1)
        cnt1_ref[h] = cnt1
        w1_ref[h] = jnp.exp(s1 - top1) * (0.5 / z)
        rank2_ref[h] = rank2.astype(BF16).reshape(tiles)
        e2_ref[h] = jnp.exp(s2 - top2).astype(BF16).reshape(tiles)

    def finish(h, par, x, z):
        s1, s2 = scores(h)
        rank1, rank2 = r_scr[par, 0], r_scr[par, 1]
        taken = jnp.where(x == -jnp.inf, 1.0, 0.0)
        cnt_rows = ([jnp.sum(taken[0:k], axis=0, keepdims=True)]
                    + [jnp.sum(taken[k + 8 * (a - 1):k + 8 * a], axis=0, keepdims=True) for a in range(1, 8)]
                    + [taken[k + 56 + a:k + 57 + a] for a in range(8)])
        n_pairs = cnt_rows[0]
        for row in cnt_rows[1:]:
            n_pairs = n_pairs + row
        n1 = jnp.sum(jnp.where(rank1 < float(k), 1.0, 0.0), axis=0, keepdims=True)
        n2 = jnp.sum(jnp.where(rank2 < float(k), 1.0, 0.0), axis=0, keepdims=True)
        tied = (n1 != float(k)) | (n2 != float(k)) | (n_pairs != float(k))
        emit(h, s1, s2, rank1, rank2, cnt_rows, z, v_scr[par, 0, 0:1, :], v_scr[par, 1, 0:1, :])

        @pl.when(jnp.sum(jnp.where(tied, 1.0, 0.0)) > 0.0)
        def _():
            rank1 = _top16(s1, vx1_scr)
            rank2 = _top16(s2, vx2_scr)
            v1, v2 = vx1_scr[...], vx2_scr[...]
            top = v1[0:1] + v2[0:1]
            r16 = lax.broadcasted_iota(I32, (k, tn), 0)
            r8 = lax.broadcasted_iota(I32, (8, tn), 0)
            flat = jnp.concatenate([r16] + [a * k + r8 for a in range(1, 8)] + [(8 + r8) * k], axis=0)

            def pick_exact(r, carry):
                x, cnt, z = carry
                m = jnp.max(x, axis=0, keepdims=True)
                idx = jnp.min(jnp.where(x == m, flat, k * k), axis=0, keepdims=True)
                cnt = cnt + jnp.where(r16 == (idx >> 4), 1.0, 0.0)
                return jnp.where(flat == idx, -jnp.inf, x), cnt, z + jnp.exp(m - top)

            _, cnt, z = lax.fori_loop(0, k, pick_exact, (_pair_candidates(v1, v2), jnp.zeros((k, tn), F32),
                                                          jnp.zeros((1, tn), F32)))
            emit(h, s1, s2, rank1, rank2, [cnt[a:a + 1] for a in range(k)], z, v1[0:1], v2[0:1])

    pending = None
    for h in range(PEER_HEADS + 1):
        par = h % 2
        lists = h < PEER_HEADS
        init = []
        if lists:
            init += list(scores(h))
            for half in range(2):
                r_scr[par, half] = jnp.full((N_KEYS, tn), float(k), F32)
        if pending is not None:
            init += [pending[0], jnp.zeros((1, tn), F32)]
            top = pending[1]

        def body(r, carry, lists=lists, pairs=pending is not None, par=par):
            out = []
            if lists:
                for half in range(2):
                    x = carry[half]
                    m = jnp.max(x, axis=0, keepdims=True)
                    hit = x == m
                    v_scr[par, half, pl.ds(r, 1), :] = m
                    r_scr[par, half] = jnp.where(hit, lax.convert_element_type(r, F32), r_scr[par, half])
                    out.append(jnp.where(hit, -jnp.inf, x))
            if pairs:
                x, z = carry[-2], carry[-1]
                m = jnp.max(x, axis=0, keepdims=True)
                out += [jnp.where(x == m, -jnp.inf, x), z + jnp.exp(m - top)]
            return tuple(out)

        res = lax.fori_loop(0, k, body, tuple(init))
        if pending is not None:
            finish(h - 1, 1 - par, res[-2], res[-1])
        if lists:
            v1, v2 = v_scr[par, 0], v_scr[par, 1]
            pending = (_pair_candidates(v1, v2), v1[0:1] + v2[0:1])
        else:
            pending = None


def _peer_select(pqT, k1, k2):
    _, n = pqT.shape
    tn = LANES
    nt = n // tn
    tab = jax.ShapeDtypeStruct((PEER_HEADS, N_KEYS, n), F32)
    tab_spec = pl.BlockSpec((PEER_HEADS, N_KEYS, tn), lambda i: (0, 0, i))
    tabb = jax.ShapeDtypeStruct((PEER_HEADS, N_KEYS // BF16_ROWS, BF16_ROWS, n), BF16)
    tabb_spec = pl.BlockSpec((PEER_HEADS, N_KEYS // BF16_ROWS, BF16_ROWS, tn), lambda i: (0, 0, 0, i))
    list_scr = pltpu.VMEM((PEER_TOPK, tn), F32)
    return pl.pallas_call(
        _peer_select_kernel,
        grid=(nt,),
        in_specs=[pl.BlockSpec((pqT.shape[0], tn), lambda i: (0, i)),
                  pl.BlockSpec(k1.shape, lambda i: (0, 0)),
                  pl.BlockSpec(k2.shape, lambda i: (0, 0))],
        out_specs=(tab_spec, tab_spec, tabb_spec, tabb_spec),
        out_shape=(tab, tab, tabb, tabb),
        scratch_shapes=[pltpu.VMEM((2, 2, PEER_TOPK, tn), F32), pltpu.VMEM((2, 2, N_KEYS, tn), F32),
                        list_scr, list_scr],
        compiler_params=_cparams("arbitrary"),
        name="peer_select",
    )(pqT, k1, k2)


def _peer_dense_kernel(alpha, i1_per_step, h2T_ref, u_ref, vT_ref, cnt1_ref, w1_ref, rank2_ref, e2_ref,
                       x1_ref, g2_ref, lng_ref, lnb_ref, y_ref, acc_scr, aT_scr, act_scr):
    j = pl.program_id(1)

    @pl.when(j == 0)
    def _():
        acc_scr[...] = jnp.zeros(acc_scr.shape, F32)

    tn = aT_scr.shape[1]
    tiles = (N_KEYS // BF16_ROWS, BF16_ROWS, tn)
    zero = jnp.zeros(tiles, BF16)
    for ii in range(i1_per_step):
        i1 = j * i1_per_step + ii
        gate = zero
        for h in range(PEER_HEADS):
            c1 = jnp.broadcast_to(cnt1_ref[h, pl.ds(i1, 1), :], (BF16_ROWS, tn)).astype(BF16)
            w1 = jnp.broadcast_to(w1_ref[h, pl.ds(i1, 1), :], (BF16_ROWS, tn)).astype(BF16)
            gate = gate + jnp.where(rank2_ref[h] < c1[None], e2_ref[h], zero) * w1[None]
        act_scr[ii * N_KEYS:(ii + 1) * N_KEYS, :] = gate.reshape(N_KEYS, tn)
    aT_scr[...] = _dot(u_ref[...], h2T_ref[...])
    for ii in range(i1_per_step):
        rows = slice(ii * N_KEYS, (ii + 1) * N_KEYS)
        a = aT_scr[rows, :].astype(BF16)
        t = jnp.tanh(a * (_GELU_C1 + _GELU_C2 * (a * a)))
        act_scr[rows, :] = (a + a * t) * act_scr[rows, :]
    acc_scr[...] += _dot(vT_ref[...], act_scr[...])

    @pl.when(j == pl.num_programs(1) - 1)
    def _():
        peer = acc_scr[...].T
        y_ref[...] = _layer_norm(alpha * x1_ref[...] + g2_ref[0] * peer, lng_ref[...], lnb_ref[...])


def _peer_dense(alpha, h2T, u_b, vT_b, tabs, x1, g2, ln_g, ln_b, rows_per_vec):
    d, n = h2T.shape
    tn = min(512, rows_per_vec) if g2.shape[1] == 1 else g2.shape[1]
    tiles_per_vec = rows_per_vec // tn
    i1_per_step = 16
    te = i1_per_step * N_KEYS
    n_exp = u_b.shape[0]
    tab_spec = pl.BlockSpec((PEER_HEADS, N_KEYS, tn), lambda i, j: (0, 0, i))
    tabb_spec = pl.BlockSpec((PEER_HEADS, N_KEYS // BF16_ROWS, BF16_ROWS, tn), lambda i, j: (0, 0, 0, i))
    return pl.pallas_call(
        functools.partial(_peer_dense_kernel, alpha, i1_per_step),
        grid=(n // tn, n_exp // te),
        in_specs=[pl.BlockSpec((d, tn), lambda i, j: (0, i)),
                  pl.BlockSpec((te, d), lambda i, j: (j, 0)),
                  pl.BlockSpec((d, te), lambda i, j: (0, j)),
                  tab_spec, tab_spec, tabb_spec, tabb_spec,
                  pl.BlockSpec((tn, d), lambda i, j: (i, 0)),
                  pl.BlockSpec((1,) + g2.shape[1:], lambda i, j: (i // tiles_per_vec, 0, 0)),
                  pl.BlockSpec(ln_g.shape, lambda i, j: (0, 0)),
                  pl.BlockSpec(ln_b.shape, lambda i, j: (0, 0))],
        out_specs=pl.BlockSpec((tn, d), lambda i, j: (i, 0)),
        out_shape=jax.ShapeDtypeStruct((n, d), F32),
        scratch_shapes=[pltpu.VMEM((d, tn), F32), pltpu.VMEM((te, tn), F32), pltpu.VMEM((te, tn), BF16)],
        compiler_params=_cparams("arbitrary", "arbitrary"),
        name="peer_dense_ln2",
    )(h2T, u_b, vT_b, *tabs, x1, g2, ln_g, ln_b)


def _transpose_cast_kernel(x_ref, o_ref):
    o_ref[...] = x_ref[...].T.astype(BF16)


def _transpose_cast(x):
    r, c = x.shape
    tr = min(512, r)
    return pl.pallas_call(
        _transpose_cast_kernel,
        grid=(r // tr,),
        in_specs=[pl.BlockSpec((tr, c), lambda i: (i, 0))],
        out_specs=pl.BlockSpec((c, tr), lambda i: (0, i)),
        out_shape=jax.ShapeDtypeStruct((c, r), BF16),
        compiler_params=_cparams("arbitrary"),
        name="transpose_cast",
    )(x)


def _pack_w_in(w):
    widths = (ATTN_WIDTH, KV_WIDTH, KV_WIDTH, N_IDX_HEADS * IDX_DIM, N_IDX_HEADS, IDX_DIM,
              D_CONV, D_CONV, D_CONV, w.shape[0], w.shape[0])
    parts, start = [], 0
    for wd in widths:
        parts.append(w[:, start:start + wd])
        start += wd
    q, k, v, qi, wi, ki, bg, cg, xin, ga, gb = parts
    pad = jnp.zeros((w.shape[0], _KIWI - IDX_DIM - N_IDX_HEADS), w.dtype)
    return jnp.concatenate([q, k, v, qi, ki, wi, pad, bg, cg, xin, ga, gb], axis=1).astype(BF16)


def kernel(x_prompt, x_sample, c_prompt, c_sample, cache_k, cache_v, cache_kidx, state_conv, page_table,
           rel_bias, w_ada, b_ada, w_in, conv_w, conv_b, w_o_attn, w_o_conv, w_out, ln1_g, ln1_b,
           ln2_g, ln2_b, peer_wq, peer_k1, peer_k2, peer_u, peer_v):
    depth = w_in.shape[0]
    alpha = (2 * depth) ** 0.25
    bp, sp, d = x_prompt.shape
    bs, ts, _ = x_sample.shape
    n_pool = cache_k.shape[1]
    past = page_table.shape[1] * PAGE_SIZE
    ns = bs * ts

    xp, xs = x_prompt, x_sample.reshape(ns, d)
    outs = [[] for _ in range(8)]
    for l in range(depth):
        c_all = jnp.concatenate([c_prompt, c_sample], axis=0)
        c_all = jnp.pad(c_all, ((0, (-c_all.shape[0]) % 16), (0, 0)))
        mod = _ada(c_all, w_ada[l], b_ada[l])[:bp + bs]
        sh1, sc1, g1, sh2, sc2, g2 = jnp.split(mod, 6, axis=-1)
        as_p = lambda a: a[:bp].reshape(bp, 1, d)
        as_s = lambda a: jnp.repeat(a[bp:], ts, axis=0).reshape(1, ns, d)

        w_all = _pack_w_in(w_in[l])
        woa, woc, wout = w_o_attn[l].astype(BF16), w_o_conv[l].astype(BF16), w_out[l].astype(BF16)
        wq = peer_wq[l].astype(BF16)
        k1, k2 = peer_k1[l].astype(BF16), peer_k2[l].astype(BF16)
        u_b = peer_u[l].astype(BF16)
        vT_b = _transpose_cast(peer_v[l])
        lg1, lb1 = ln1_g[l].reshape(1, d), ln1_b[l].reshape(1, d)
        lg2, lb2 = ln2_g[l].reshape(1, d), ln2_b[l].reshape(1, d)

        (qT, k_p, v_p, kb_p, vTb_p, qiT, ki_p, kib_p, wiT, oconv_p, ga_p, gb_p, ulast_p) = _inproj_prompt(
            xp, as_p(sc1), as_p(sh1), w_all, conv_w[l], conv_b[l])
        oattn_p = _attn_prompt(rel_bias, qT, qiT, wiT, kb_p, vTb_p, kib_p)
        x1_p, h2T_p, pqT_p = _merge(alpha, xp.reshape(bp * sp, d), oattn_p.reshape(bp * sp, ATTN_WIDTH),
                                    oconv_p.reshape(bp * sp, D_CONV), ga_p.reshape(bp * sp, d),
                                    gb_p.reshape(bp * sp, d), as_p(g1), as_p(sc2), as_p(sh2),
                                    woa, woc, wout, wq, lg1, lb1, sp)
        tabs_p = _peer_select(pqT_p, k1, k2)
        y_p = _peer_dense(alpha, h2T_p, u_b, vT_b, tabs_p, x1_p, as_p(g2), lg2, lb2, sp)
        xp = y_p.reshape(bp, sp, d)

        st = state_conv[l]
        st0 = jnp.repeat(st[:, 0], ts, axis=0)
        st1 = jnp.repeat(st[:, 1], ts, axis=0)
        (q_s, k_s, v_s, qi_s, kiwi_s, oconv_s, ga_s, gb_s, u_s) = _inproj_sample(
            xs, as_s(sc1)[0], as_s(sh1)[0], w_all, conv_w[l], conv_b[l], st0, st1, ts)
        ki_s = kiwi_s[:, :IDX_DIM]
        wi_s = kiwi_s[:, IDX_DIM:IDX_DIM + N_IDX_HEADS]
        q5 = q_s.reshape(bs, ts, N_KV_HEADS, GROUP, HEAD_DIM).transpose(0, 2, 3, 1, 4)
        qpad = jnp.zeros((bs, N_KV_HEADS, GROUP * ts, N_KV_HEADS, HEAD_DIM), F32)
        for g in range(N_KV_HEADS):
            qpad = qpad.at[:, g, :, g, :].set(q5[:, g].reshape(bs, GROUP * ts, HEAD_DIM))
        qpad = qpad.reshape(bs, N_HEADS * ts, KV_WIDTH).astype(BF16)
        qi_r = qi_s.reshape(bs, ts, N_IDX_HEADS, IDX_DIM).transpose(0, 2, 1, 3).reshape(
            bs, N_IDX_HEADS * ts, IDX_DIM).astype(BF16)
        wi_r = jnp.broadcast_to(wi_s.reshape(bs, ts, N_IDX_HEADS).transpose(0, 2, 1).reshape(
            bs, N_IDX_HEADS * ts, 1), (bs, N_IDX_HEADS * ts, LANES))
        o_s = _attn_sample(page_table, rel_bias, qpad, qi_r, wi_r,
                           k_s.reshape(bs, ts, KV_WIDTH), v_s.reshape(bs, ts, KV_WIDTH),
                           ki_s.reshape(bs, ts, IDX_DIM),
                           cache_k[l].reshape(n_pool, PAGE_SIZE, KV_WIDTH),
                           cache_v[l].reshape(n_pool, PAGE_SIZE, KV_WIDTH), cache_kidx[l], past)
        o5 = o_s.reshape(bs, N_KV_HEADS, GROUP, ts, N_KV_HEADS, HEAD_DIM)
        oattn_s = jnp.stack([o5[:, g, :, :, g, :] for g in range(N_KV_HEADS)], axis=1)
        oattn_s = oattn_s.transpose(0, 3, 1, 2, 4).reshape(ns, ATTN_WIDTH).astype(BF16)
        x1_s, h2T_s, pqT_s = _merge(alpha, xs, oattn_s, oconv_s, ga_s, gb_s, as_s(g1), as_s(sc2), as_s(sh2),
                                    woa, woc, wout, wq, lg1, lb1, ns)
        tabs_s = _peer_select(pqT_s, k1, k2)
        xs = _peer_dense(alpha, h2T_s, u_b, vT_b, tabs_s, x1_s, as_s(g2), lg2, lb2, ns)

        for lst, val in zip(outs, (
                k_p.reshape(bp, sp, N_KV_HEADS, HEAD_DIM), v_p.reshape(bp, sp, N_KV_HEADS, HEAD_DIM), ki_p,
                ulast_p[:, 8 - (CONV_WIDTH - 1):],
                k_s.reshape(bs, ts, N_KV_HEADS, HEAD_DIM), v_s.reshape(bs, ts, N_KV_HEADS, HEAD_DIM),
                ki_s.reshape(bs, ts, IDX_DIM),
                u_s.reshape(bs, ts, D_CONV)[:, ts - (CONV_WIDTH - 1):])):
            lst.append(val)

    return (xp, xs.reshape(bs, ts, d)) + tuple(jnp.stack(o) for o in outs)
```

```python
import functools
import math

import jax
import jax.numpy as jnp
from jax import lax
from jax.experimental import pallas as pl
from jax.experimental.pallas import tpu as pltpu

F32 = jnp.float32
BF16 = jnp.bfloat16
I32 = jnp.int32

N_HEADS = 8
N_KV_HEADS = 2
HEAD_DIM = 64
GROUP = N_HEADS // N_KV_HEADS
ATTN_WIDTH = N_HEADS * HEAD_DIM
KV_WIDTH = N_KV_HEADS * HEAD_DIM
ATTN_SCALE = HEAD_DIM ** -0.5
N_IDX_HEADS = 4
IDX_DIM = 64
IDX_SCALE = (IDX_DIM * N_IDX_HEADS) ** -0.5
TOPK_MAX = 256
N_BUCKETS = 32
MAX_DISTANCE = 128
D_CONV = 512
CONV_WIDTH = 3
N_KEYS = 128
PEER_HEADS = 8
PEER_HALF = 64
PEER_TOPK = 16
LN_EPS = 1e-5
PAGE_SIZE = 128

LANES = 128
BF16_ROWS = 16
VMEM_LIMIT = 56 * 1024 * 1024

_GELU_C1 = math.sqrt(2.0 / math.pi)
_GELU_C2 = _GELU_C1 * 0.044715

NEG_BIG = -1e30

_KIWI = LANES
_OFF_Q = 0
_OFF_K = _OFF_Q + ATTN_WIDTH
_OFF_V = _OFF_K + KV_WIDTH
_OFF_QI = _OFF_V + KV_WIDTH
_OFF_KIWI = _OFF_QI + N_IDX_HEADS * IDX_DIM
_OFF_BG = _OFF_KIWI + _KIWI
_OFF_CG = _OFF_BG + D_CONV
_OFF_XIN = _OFF_CG + D_CONV
_OFF_GA = _OFF_XIN + D_CONV


def _cparams(*sem):
    return pltpu.CompilerParams(dimension_semantics=sem, vmem_limit_bytes=VMEM_LIMIT)


def _dot(a, b):
    return jnp.dot(a, b, preferred_element_type=F32)


def _dot_nt(a, b):
    return lax.dot_general(a, b, (((1,), (1,)), ((), ())), preferred_element_type=F32)


_COARSE_STEPS = 10


def _kth_largest(stats, count_ge, k, n_valid, v_min, v_max):
    enough = n_valid >= k
    lo0 = jnp.where(enough, v_min, -jnp.inf)
    hi0 = jnp.where(enough, v_max, -jnp.inf)

    def cond(state):
        lo, hi = state
        return jnp.max(jnp.where(lo < hi, 1.0, 0.0)) > 0.0

    def midpoint(lo, hi):
        mid = lo + (hi - lo) * 0.5
        return jnp.where(mid > lo, mid, hi)

    def coarse_step(_, state):
        lo, hi = state
        mid = midpoint(lo, hi)
        cnt = count_ge(mid)
        active = lo < hi
        return (jnp.where(active & (cnt >= k), mid, lo), jnp.where(active & (cnt < k), mid, hi))

    def step(state):
        lo, hi = state
        cnt, above, below = stats(midpoint(lo, hi))
        active = lo < hi
        return (jnp.where(active & (cnt >= k), above, lo), jnp.where(active & (cnt < k), below, hi))

    state = lax.fori_loop(0, _COARSE_STEPS, coarse_step, (lo0, hi0))
    lo, _ = lax.while_loop(cond, lambda state: step(step(state)), state)
    return lo


def _t5_bias(dist, relb_ref, h):
    n = jnp.maximum(dist, 0)
    max_exact = N_BUCKETS // 2
    nf = jnp.maximum(n, 1).astype(F32)
    large = max_exact + jnp.floor(jnp.log(nf / max_exact) / math.log(MAX_DISTANCE / max_exact)
                                  * (N_BUCKETS - max_exact)).astype(I32)
    large = jnp.minimum(large, N_BUCKETS - 1)
    bucket = jnp.where(n < max_exact, n, large)
    out = jnp.zeros(dist.shape, F32)
    for b in range(N_BUCKETS):
        out = jnp.where(bucket == b, relb_ref[b, h], out)
    return out


def _layer_norm(x, g, b):
    mu = jnp.mean(x, axis=-1, keepdims=True)
    var = jnp.mean(jnp.square(x - mu), axis=-1, keepdims=True)
    return (x - mu) * lax.rsqrt(var + LN_EPS) * g + b


def _ada_kernel(c_ref, w_ref, b_ref, o_ref):
    o_ref[...] = _dot(c_ref[...].astype(BF16), w_ref[...].astype(BF16)) + b_ref[...]


def _ada(c, w, b):
    m, d = c.shape
    n = w.shape[1]
    tn = n // 4
    return pl.pallas_call(
        _ada_kernel,
        grid=(n // tn,),
        in_specs=[pl.BlockSpec((m, d), lambda j: (0, 0)),
                  pl.BlockSpec((d, tn), lambda j: (0, j)),
                  pl.BlockSpec((1, tn), lambda j: (0, j))],
        out_specs=pl.BlockSpec((m, tn), lambda j: (0, j)),
        out_shape=jax.ShapeDtypeStruct((m, n), F32),
        compiler_params=_cparams("arbitrary"),
        name="ada_mod",
    )(c, w, b.reshape(1, n))


def _conv_out(bg, u, u1, u2, cw_ref, cb_ref):
    y = cb_ref[...] + u2 * cw_ref[0:1, :] + u1 * cw_ref[1:2, :] + u * cw_ref[2:3, :]
    return bg * y


def _inproj_prompt_kernel(x_ref, sc_ref, sh_ref, w_ref, cw_ref, cb_ref,
                          qT_ref, k_ref, v_ref, kb_ref, vTb_ref, qiT_ref, ki_ref, kib_ref, wiT_ref,
                          oconv_ref, ga_ref, gb_ref, ulast_ref, uprev_scr):
    s = pl.program_id(1)
    tm = x_ref.shape[1]
    h = (x_ref[0] * (1.0 + sc_ref[0]) + sh_ref[0]).astype(BF16)

    def proj(a, width):
        return _dot(h, w_ref[:, a:a + width])

    qT_ref[0] = proj(_OFF_Q, ATTN_WIDTH).T.astype(BF16)
    k = proj(_OFF_K, KV_WIDTH)
    k_ref[0] = k
    kb_ref[0] = k.astype(BF16)
    v = proj(_OFF_V, KV_WIDTH)
    v_ref[0] = v
    for i in range(tm // LANES):
        vTb_ref[0, i] = v[i * LANES:(i + 1) * LANES].T.astype(BF16)
    qiT_ref[0] = proj(_OFF_QI, N_IDX_HEADS * IDX_DIM).T.astype(BF16)
    kiwi = proj(_OFF_KIWI, _KIWI)
    ki_ref[0] = kiwi[:, :IDX_DIM]
    kib_ref[0] = kiwi[:, :IDX_DIM].astype(BF16)
    wiT_ref[0] = kiwi.T[IDX_DIM:IDX_DIM + 8]

    @pl.when(s == 0)
    def _():
        uprev_scr[...] = jnp.zeros(uprev_scr.shape, F32)

    bg = proj(_OFF_BG, D_CONV)
    u = proj(_OFF_CG, D_CONV) * proj(_OFF_XIN, D_CONV)
    row = lax.broadcasted_iota(I32, u.shape, 0)
    up = uprev_scr[...]
    u1 = jnp.where(row < 1, pltpu.roll(up, 1, 0), pltpu.roll(u, 1, 0))
    u2 = jnp.where(row < 2, pltpu.roll(up, 2, 0), pltpu.roll(u, 2, 0))
    oconv_ref[0] = _conv_out(bg, u, u1, u2, cw_ref, cb_ref).astype(BF16)
    uprev_scr[...] = u
    ulast_ref[0] = u[tm - 8:, :]

    ga_ref[0] = proj(_OFF_GA, w_ref.shape[0])
    gb_ref[0] = proj(_OFF_GA + w_ref.shape[0], w_ref.shape[0])


def _inproj_prompt(x, sc, sh, w_all, conv_w, conv_b):
    b, s, d = x.shape
    tm = min(512, s)
    nkb = tm // LANES
    row = lambda width, dt: jax.ShapeDtypeStruct((b, s, width), dt)
    out_shape = (
        jax.ShapeDtypeStruct((b, ATTN_WIDTH, s), BF16),
        row(KV_WIDTH, F32), row(KV_WIDTH, F32), row(KV_WIDTH, BF16),
        jax.ShapeDtypeStruct((b, s // LANES, KV_WIDTH, LANES), BF16),
        jax.ShapeDtypeStruct((b, N_IDX_HEADS * IDX_DIM, s), BF16),
        row(IDX_DIM, F32), row(IDX_DIM, BF16),
        jax.ShapeDtypeStruct((b, 8, s), F32),
        row(D_CONV, BF16),
        row(d, F32), row(d, F32),
        jax.ShapeDtypeStruct((b, 8, D_CONV), F32),
    )
    tile = lambda width: pl.BlockSpec((1, tm, width), lambda i, j: (i, j, 0))
    tileT = lambda rows: pl.BlockSpec((1, rows, tm), lambda i, j: (i, 0, j))
    out_specs = (
        tileT(ATTN_WIDTH), tile(KV_WIDTH), tile(KV_WIDTH), tile(KV_WIDTH),
        pl.BlockSpec((1, nkb, KV_WIDTH, LANES), lambda i, j: (i, j, 0, 0)),
        tileT(N_IDX_HEADS * IDX_DIM), tile(IDX_DIM), tile(IDX_DIM), tileT(8),
        tile(D_CONV), tile(d), tile(d),
        pl.BlockSpec((1, 8, D_CONV), lambda i, j: (i, 0, 0)),
    )
    vec = pl.BlockSpec((1, 1, d), lambda i, j: (i, 0, 0))
    return pl.pallas_call(
        _inproj_prompt_kernel,
        grid=(b, s // tm),
        in_specs=[tile(d), vec, vec,
                  pl.BlockSpec(w_all.shape, lambda i, j: (0, 0)),
                  pl.BlockSpec(conv_w.shape, lambda i, j: (0, 0)),
                  pl.BlockSpec((1, D_CONV), lambda i, j: (0, 0))],
        out_specs=out_specs,
        out_shape=out_shape,
        scratch_shapes=[pltpu.VMEM((tm, D_CONV), F32)],
        compiler_params=_cparams("arbitrary", "arbitrary"),
        name="inproj_prompt",
    )(x, sc, sh, w_all, conv_w, conv_b.reshape(1, D_CONV))


def _inproj_sample_kernel(t_seq, x_ref, sc_ref, sh_ref, w_ref, cw_ref, cb_ref, s0_ref, s1_ref,
                          q_ref, k_ref, v_ref, qi_ref, kiwi_ref, oconv_ref, ga_ref, gb_ref, u_ref):
    h = (x_ref[...] * (1.0 + sc_ref[...]) + sh_ref[...]).astype(BF16)

    def proj(a, width):
        return _dot(h, w_ref[:, a:a + width])

    q_ref[...] = proj(_OFF_Q, ATTN_WIDTH)
    k_ref[...] = proj(_OFF_K, KV_WIDTH)
    v_ref[...] = proj(_OFF_V, KV_WIDTH)
    qi_ref[...] = proj(_OFF_QI, N_IDX_HEADS * IDX_DIM)
    kiwi_ref[...] = proj(_OFF_KIWI, _KIWI)
    bg = proj(_OFF_BG, D_CONV)
    u = proj(_OFF_CG, D_CONV) * proj(_OFF_XIN, D_CONV)
    t = lax.broadcasted_iota(I32, u.shape, 0) % t_seq
    u1 = jnp.where(t == 0, s1_ref[...], pltpu.roll(u, 1, 0))
    u2 = jnp.where(t == 0, s0_ref[...], jnp.where(t == 1, s1_ref[...], pltpu.roll(u, 2, 0)))
    oconv_ref[...] = _conv_out(bg, u, u1, u2, cw_ref, cb_ref).astype(BF16)
    u_ref[...] = u
    ga_ref[...] = proj(_OFF_GA, w_ref.shape[0])
    gb_ref[...] = proj(_OFF_GA + w_ref.shape[0], w_ref.shape[0])


def _inproj_sample(x, sc_rows, sh_rows, w_all, conv_w, conv_b, st0_rows, st1_rows, t_seq):
    n, d = x.shape
    full = lambda a: pl.BlockSpec(a.shape, lambda i: (0,) * a.ndim)
    o = lambda width, dt: jax.ShapeDtypeStruct((n, width), dt)
    out_shape = (o(ATTN_WIDTH, F32), o(KV_WIDTH, F32), o(KV_WIDTH, F32), o(N_IDX_HEADS * IDX_DIM, F32),
                 o(_KIWI, F32), o(D_CONV, BF16), o(d, F32), o(d, F32), o(D_CONV, F32))
    args = (x, sc_rows, sh_rows, w_all, conv_w, conv_b.reshape(1, D_CONV), st0_rows, st1_rows)
    return pl.pallas_call(
        functools.partial(_inproj_sample_kernel, t_seq),
        grid=(1,),
        in_specs=[full(a) for a in args],
        out_specs=tuple(pl.BlockSpec(s.shape, lambda i: (0, 0)) for s in out_shape),
        out_shape=out_shape,
        compiler_params=_cparams("arbitrary"),
        name="inproj_sample",
    )(*args)


def _attn_prompt_kernel(n_sel, relb_ref, qT_ref, qiT_ref, wiT_ref, kb_ref, vTb_ref, kib_ref,
                        o_ref, bias_scr, key_scr):
    qb = pl.program_id(1)
    blk = LANES
    kblk = 2 * blk
    ntrip = qb // 2 + 1

    @pl.when((pl.program_id(0) == 0) & (qb == 0))
    def _():
        j = lax.broadcasted_iota(I32, (blk, blk), 0)
        i = lax.broadcasted_iota(I32, (blk, blk), 1)
        for delta in range(3):
            dist = i - j + blk * delta
            for h in range(N_HEADS):
                g, r = divmod(h, GROUP)
                bias_scr[delta, g, :, r * blk:(r + 1) * blk] = _t5_bias(dist, relb_ref, h)

    row = lax.broadcasted_iota(I32, (kblk, blk), 0)
    lane = lax.broadcasted_iota(I32, (kblk, blk), 1)

    def valid_mask(t):
        return (t * kblk + row) <= (qb * blk + lane)

    def key_rows(t):
        return pl.ds(pl.multiple_of(t * kblk, kblk), kblk)

    qiT = jnp.concatenate([qiT_ref[0, h * IDX_DIM:(h + 1) * IDX_DIM, :] for h in range(N_IDX_HEADS)], axis=1)
    wi = [wiT_ref[0, h:h + 1, :] for h in range(N_IDX_HEADS)]
    pr = 32
    fold = lambda a: a.reshape(kblk // pr, pr, blk)
    tall = lambda a: jnp.sum(a, axis=0, keepdims=True)
    part0 = (jnp.zeros((pr, blk), F32), jnp.full((pr, blk), jnp.inf, F32), jnp.full((pr, blk), -jnp.inf, F32))

    def score_blk(t, carry):
        cnt, lo, hi = carry
        d = _dot(kib_ref[0, key_rows(t), :], qiT)
        s = jnp.maximum(d[:, 0:blk], 0.0) * wi[0]
        for h in range(1, N_IDX_HEADS):
            s = s + jnp.maximum(d[:, h * blk:(h + 1) * blk], 0.0) * wi[h]
        s = s * IDX_SCALE
        valid = valid_mask(t)
        key_scr[key_rows(t), :] = jnp.where(valid, s, -jnp.inf)
        return (cnt + jnp.sum(fold(jnp.where(valid, 1.0, 0.0)), axis=0),
                jnp.minimum(lo, jnp.min(fold(jnp.where(valid, s, jnp.inf)), axis=0)),
                jnp.maximum(hi, jnp.max(fold(jnp.where(valid, s, -jnp.inf)), axis=0)))

    cnt, lo, hi = lax.fori_loop(0, ntrip, score_blk, part0)
    n_valid = tall(cnt)
    v_min = jnp.min(lo, axis=0, keepdims=True)
    v_max = jnp.max(hi, axis=0, keepdims=True)

    def stats(mid):
        def body(t, carry):
            cnt, above, below = carry
            x = key_scr[key_rows(t), :]
            ge = x >= mid
            return (cnt + jnp.sum(fold(jnp.where(ge, 1.0, 0.0)), axis=0),
                    jnp.minimum(above, jnp.min(fold(jnp.where(ge, x, jnp.inf)), axis=0)),
                    jnp.maximum(below, jnp.max(fold(jnp.where(ge, -jnp.inf, x)), axis=0)))
        cnt, above, below = lax.fori_loop(0, ntrip, body, part0)
        return tall(cnt), jnp.min(above, axis=0, keepdims=True), jnp.max(below, axis=0, keepdims=True)

    def count(pred):
        def body(t, acc):
            return acc + jnp.sum(fold(jnp.where(pred(key_scr[key_rows(t), :]), 1.0, 0.0)), axis=0)
        return tall(lax.fori_loop(0, ntrip, body, jnp.zeros((pr, blk), F32)))

    thr = _kth_largest(stats, lambda mid: count(lambda x: x >= mid), float(n_sel), n_valid, v_min, v_max)

    need = n_sel - count(lambda x: x > thr)
    ltri = (lax.broadcasted_iota(I32, (kblk, kblk), 1) <= lax.broadcasted_iota(I32, (kblk, kblk), 0)).astype(BF16)

    def select_mask(t, ties_before):
        kv = key_scr[key_rows(t), :]
        eq = kv == thr
        prefix = _dot(ltri, eq.astype(BF16)) + ties_before
        sel = ((kv > thr) | (eq & (prefix <= need))) & valid_mask(t)
        return jnp.where(sel, 0.0, NEG_BIG), prefix[kblk - 1:kblk, :]

    zero = jnp.zeros((HEAD_DIM, blk), BF16)
    scale = jnp.asarray(ATTN_SCALE, BF16)
    qTg = []
    for g in range(N_KV_HEADS):
        cols = []
        for r in range(GROUP):
            base = g * GROUP * HEAD_DIM + r * HEAD_DIM
            parts = [zero] * N_KV_HEADS
            parts[g] = qT_ref[0, base:base + HEAD_DIM, :] * scale
            cols.append(jnp.concatenate(parts, axis=0))
        qTg.append(jnp.concatenate(cols, axis=1))
    width = GROUP * blk

    def att_blk(t, carry):
        kmat = kb_ref[0, key_rows(t), :]
        mask1, ties = select_mask(t, carry[N_KV_HEADS])
        mask = jnp.concatenate([mask1] * GROUP, axis=1)
        d0 = jnp.clip(qb - 2 * t, 0, 2)
        d1 = jnp.clip(qb - 2 * t - 1, 0, 2)
        new = []
        for g in range(N_KV_HEADS):
            m, l, acc = carry[g]
            bias = jnp.concatenate([bias_scr[d0, g], bias_scr[d1, g]], axis=0)
            lg = _dot(kmat, qTg[g]) + bias + mask
            m_new = jnp.maximum(m, jnp.max(lg, axis=0, keepdims=True))
            p = jnp.exp(lg - m_new)
            alpha = jnp.exp(m - m_new)
            l = alpha * l + jnp.sum(p, axis=0, keepdims=True)
            rows = slice(g * HEAD_DIM, (g + 1) * HEAD_DIM)
            vt = jnp.concatenate([vTb_ref[0, 2 * t, rows, :], vTb_ref[0, 2 * t + 1, rows, :]], axis=1)
            acc = alpha * acc + _dot(vt, p.astype(BF16))
            new.append((m_new, l, acc))
        return tuple(new) + (ties,)

    init = (jnp.full((1, width), NEG_BIG, F32), jnp.zeros((1, width), F32), jnp.zeros((HEAD_DIM, width), F32))
    res = lax.fori_loop(0, ntrip, att_blk, (init,) * N_KV_HEADS + (jnp.zeros((1, blk), F32),))
    outs = []
    for g in range(N_KV_HEADS):
        _, l, acc = res[g]
        og = acc / l
        outs += [og[:, r * blk:(r + 1) * blk] for r in range(GROUP)]
    o_ref[0] = jnp.concatenate(outs, axis=0).T.astype(BF16)


def _attn_prompt(rel_bias, qT, qiT, wiT, kb, vTb, kib):
    b, _, s = qT.shape
    n_sel = min(TOPK_MAX, s // 4)
    blk = LANES
    per_b = lambda a: pl.BlockSpec((1,) + a.shape[1:], lambda i, j: (i,) + (0,) * (a.ndim - 1))
    qtile = lambda rows: pl.BlockSpec((1, rows, blk), lambda i, j: (i, 0, j))
    return pl.pallas_call(
        functools.partial(_attn_prompt_kernel, n_sel),
        grid=(b, s // blk),
        in_specs=[pl.BlockSpec(memory_space=pltpu.SMEM),
                  qtile(ATTN_WIDTH), qtile(N_IDX_HEADS * IDX_DIM), qtile(8),
                  per_b(kb), per_b(vTb), per_b(kib)],
        out_specs=pl.BlockSpec((1, blk, ATTN_WIDTH), lambda i, j: (i, j, 0)),
        out_shape=jax.ShapeDtypeStruct((b, s, ATTN_WIDTH), BF16),
        scratch_shapes=[pltpu.VMEM((3, N_KV_HEADS, blk, GROUP * blk), F32),
                        pltpu.VMEM((s, blk), F32)],
        compiler_params=_cparams("arbitrary", "arbitrary"),
        name="attn_prompt",
    )(rel_bias, qT, qiT, wiT, kb, vTb, kib)


def _attn_sample_kernel(n_sel, past, t_seq, pt_ref, relb_ref, qpad_ref, qi_ref, wi_ref,
                        knew_ref, vnew_ref, kinew_ref, ck_hbm, cv_hbm, cki_hbm,
                        o_ref, kbuf, vbuf, kibuf, sem):
    b = pl.program_id(0)
    nb = pl.num_programs(0)
    n_pages = past // PAGE_SIZE
    lp = kbuf.shape[1]
    blk = LANES
    nblk = lp // blk
    slot = b % 2

    def page_copies(seq, sl, p):
        phys = pt_ref[seq, p]
        rows = pl.ds(pl.multiple_of(p * PAGE_SIZE, PAGE_SIZE), PAGE_SIZE)
        return (pltpu.make_async_copy(ck_hbm.at[phys], kbuf.at[sl, rows, :], sem.at[sl, 0]),
                pltpu.make_async_copy(cv_hbm.at[phys], vbuf.at[sl, rows, :], sem.at[sl, 1]),
                pltpu.make_async_copy(cki_hbm.at[phys], kibuf.at[sl, rows, :], sem.at[sl, 2]))

    def start_all(seq, sl):
        def body(p, c):
            for cp in page_copies(seq, sl, p):
                cp.start()
            return c
        lax.fori_loop(0, n_pages, body, 0)

    def wait_all(seq, sl):
        def body(p, c):
            for cp in page_copies(seq, sl, p):
                cp.wait()
            return c
        lax.fori_loop(0, n_pages, body, 0)

    @pl.when(b == 0)
    def _():
        for sl in range(2):
            kbuf[sl, past:, :] = jnp.zeros((blk, KV_WIDTH), F32)
            vbuf[sl, past:, :] = jnp.zeros((blk, KV_WIDTH), F32)
            kibuf[sl, past:, :] = jnp.zeros((blk, IDX_DIM), F32)
        start_all(0, 0)

    @pl.when(b + 1 < nb)
    def _():
        start_all(b + 1, 1 - slot)

    kbuf[slot, past:past + t_seq, :] = knew_ref[0]
    vbuf[slot, past:past + t_seq, :] = vnew_ref[0]
    kibuf[slot, past:past + t_seq, :] = kinew_ref[0]
    wait_all(b, slot)

    d = _dot_nt(qi_ref[0], kibuf[slot].astype(BF16))
    s = jnp.maximum(d[0:t_seq], 0.0) * wi_ref[0, 0:t_seq, 0:1]
    for h in range(1, N_IDX_HEADS):
        s = s + jnp.maximum(d[h * t_seq:(h + 1) * t_seq], 0.0) * wi_ref[0, h * t_seq:(h + 1) * t_seq, 0:1]
    kpos = lax.broadcasted_iota(I32, (t_seq, lp), 1)
    qpos = past + lax.broadcasted_iota(I32, (t_seq, lp), 0)
    valid = kpos <= qpos
    s = s * IDX_SCALE
    key = jnp.where(valid, s, -jnp.inf)

    def count(pred):
        return jnp.sum(jnp.where(pred, 1.0, 0.0), axis=1, keepdims=True)

    def stats(mid):
        ge = key >= mid
        return (count(ge), jnp.min(jnp.where(ge, key, jnp.inf), axis=1, keepdims=True),
                jnp.max(jnp.where(ge, -jnp.inf, key), axis=1, keepdims=True))

    thr = _kth_largest(stats, lambda mid: count(key >= mid), float(n_sel), count(valid),
                       jnp.min(jnp.where(valid, s, jnp.inf), axis=1, keepdims=True),
                       jnp.max(key, axis=1, keepdims=True))
    need = n_sel - count(key > thr)
    eq = key == thr
    eqf = jnp.where(eq, 1.0, 0.0)
    utri = (lax.broadcasted_iota(I32, (blk, blk), 0) <= lax.broadcasted_iota(I32, (blk, blk), 1)).astype(BF16)
    carry = jnp.zeros((t_seq, 1), F32)
    ranks = []
    for kb in range(nblk):
        e = eqf[:, kb * blk:(kb + 1) * blk]
        ranks.append(_dot(e.astype(BF16), utri) + carry)
        carry = carry + jnp.sum(e, axis=1, keepdims=True)
    prefix = jnp.concatenate(ranks, axis=1)
    sel = jnp.where(((key > thr) | (eq & (prefix <= need))) & valid, 1.0, 0.0)

    lg = _dot_nt(qpad_ref[0], kbuf[slot].astype(BF16)) * ATTN_SCALE
    far = lp - 2 * blk
    near_dist = (past + lax.broadcasted_iota(I32, (t_seq, 2 * blk), 0)
                 - (far + lax.broadcasted_iota(I32, (t_seq, 2 * blk), 1)))
    bias_rows = []
    for h in range(N_HEADS):
        far_bias = jnp.full((t_seq, far), relb_ref[N_BUCKETS - 1, h], F32)
        bias_rows.append(jnp.concatenate([far_bias, _t5_bias(near_dist, relb_ref, h)], axis=1))
    lg = lg + jnp.concatenate(bias_rows, axis=0)
    selh = jnp.concatenate([sel] * N_HEADS, axis=0) > 0.0
    m = jnp.max(jnp.where(selh, lg, NEG_BIG), axis=1, keepdims=True)
    p = jnp.where(selh, jnp.exp(lg - m), 0.0)
    l = jnp.sum(p, axis=1, keepdims=True)
    o = _dot(p.astype(BF16), vbuf[slot].astype(BF16))
    o_ref[0] = o / l


def _attn_sample(page_table, rel_bias, qpad, qi_r, wi_r, k_new, v_new, ki_new, cache_k, cache_v, cache_ki,
                 past):
    nb, t_seq, _ = k_new.shape
    n_sel = min(TOPK_MAX, (past + t_seq) // 4)
    lp = past + LANES
    rows = N_HEADS * t_seq
    per_b = lambda a: pl.BlockSpec((1,) + a.shape[1:], lambda i, pt: (i,) + (0,) * (a.ndim - 1))
    hbm = pl.BlockSpec(memory_space=pl.ANY)
    grid_spec = pltpu.PrefetchScalarGridSpec(
        num_scalar_prefetch=1,
        grid=(nb,),
        in_specs=[pl.BlockSpec(memory_space=pltpu.SMEM),
                  per_b(qpad), per_b(qi_r), per_b(wi_r), per_b(k_new), per_b(v_new), per_b(ki_new),
                  hbm, hbm, hbm],
        out_specs=pl.BlockSpec((1, rows, KV_WIDTH), lambda i, pt: (i, 0, 0)),
        scratch_shapes=[pltpu.VMEM((2, lp, KV_WIDTH), F32),
                        pltpu.VMEM((2, lp, KV_WIDTH), F32),
                        pltpu.VMEM((2, lp, IDX_DIM), F32),
                        pltpu.SemaphoreType.DMA((2, 3))],
    )
    return pl.pallas_call(
        functools.partial(_attn_sample_kernel, n_sel, past, t_seq),
        grid_spec=grid_spec,
        out_shape=jax.ShapeDtypeStruct((nb, rows, KV_WIDTH), F32),
        compiler_params=_cparams("arbitrary"),
        name="attn_sample",
    )(page_table, rel_bias, qpad, qi_r, wi_r, k_new, v_new, ki_new, cache_k, cache_v, cache_ki)


def _merge_kernel(alpha, x_ref, oa_ref, oc_ref, ga_ref, gb_ref, g1_ref, sc2_ref, sh2_ref,
                  woa_ref, woc_ref, wout_ref, wq_ref, lng_ref, lnb_ref,
                  x1_ref, h2T_ref, pqT_ref):
    ta = _dot(oa_ref[...], woa_ref[...])
    tc = _dot(oc_ref[...], woc_ref[...])
    merged = jax.nn.sigmoid(ga_ref[...]) * ta + jax.nn.sigmoid(gb_ref[...]) * tc
    out = _dot(merged.astype(BF16), wout_ref[...])
    x1 = _layer_norm(alpha * x_ref[...] + g1_ref[0] * out, lng_ref[...], lnb_ref[...])
    x1_ref[...] = x1
    h2 = x1 * (1.0 + sc2_ref[0]) + sh2_ref[0]
    h2b = h2.astype(BF16)
    h2T_ref[...] = h2.T.astype(BF16)
    pqT_ref[...] = _dot(h2b, wq_ref[...]).T.astype(BF16)


def _merge(alpha, x, oattn, oconv, ga, gb, g1, sc2, sh2, woa, woc, wout, wq, ln_g, ln_b, rows_per_vec):
    n, d = x.shape
    tm = min(512, rows_per_vec) if g1.shape[1] == 1 else g1.shape[1]
    tiles_per_vec = rows_per_vec // tm
    tile = lambda width: pl.BlockSpec((tm, width), lambda i: (i, 0))
    vec = pl.BlockSpec((1,) + g1.shape[1:], lambda i: (i // tiles_per_vec, 0, 0))
    full = lambda a: pl.BlockSpec(a.shape, lambda i: (0,) * a.ndim)
    pq_w = wq.shape[1]
    return pl.pallas_call(
        functools.partial(_merge_kernel, alpha),
        grid=(n // tm,),
        in_specs=[tile(d), tile(ATTN_WIDTH), tile(D_CONV), tile(d), tile(d), vec, vec, vec,
                  full(woa), full(woc), full(wout), full(wq), full(ln_g), full(ln_b)],
        out_specs=(tile(d), pl.BlockSpec((d, tm), lambda i: (0, i)), pl.BlockSpec((pq_w, tm), lambda i: (0, i))),
        out_shape=(jax.ShapeDtypeStruct((n, d), F32), jax.ShapeDtypeStruct((d, n), BF16),
                   jax.ShapeDtypeStruct((pq_w, n), BF16)),
        compiler_params=_cparams("arbitrary"),
        name="merge_ln1_peerq",
    )(x, oattn, oconv, ga, gb, g1, sc2, sh2, woa, woc, wout, wq, ln_g, ln_b)


def _top16(s, v_scr):
    iota = lax.broadcasted_iota(I32, s.shape, 0)

    def body(r, carry):
        x, rank = carry
        m = jnp.max(x, axis=0, keepdims=True)
        idx = jnp.min(jnp.where(x == m, iota, N_KEYS), axis=0, keepdims=True)
        hit = iota == idx
        v_scr[pl.ds(r, 1), :] = m
        return jnp.where(hit, -jnp.inf, x), jnp.where(hit, lax.convert_element_type(r, F32), rank)

    _, rank = lax.fori_loop(0, PEER_TOPK, body, (s, jnp.full(s.shape, float(PEER_TOPK), F32)))
    return rank


def _pair_candidates(v1, v2):
    return jnp.concatenate([v1[0:1] + v2] + [v1[a:a + 1] + v2[0:8] for a in range(1, 8)]
                           + [v1[8:16] + v2[0:1]], axis=0)


def _peer_select_kernel(pqT_ref, k1_ref, k2_ref, cnt1_ref, w1_ref, rank2_ref, e2_ref,
                        v_scr, r_scr, vx1_scr, vx2_scr):
    tn = pqT_ref.shape[1]
    k = PEER_TOPK
    tiles = (N_KEYS // BF16_ROWS, BF16_ROWS, tn)

    def scores(h):
        base = h * 2 * PEER_HALF
        return (_dot(k1_ref[...], pqT_ref[base:base + PEER_HALF, :]),
                _dot(k2_ref[...], pqT_ref[base + PEER_HALF:base + 2 * PEER_HALF, :]))

    def emit(h, s1, s2, rank1, rank2, cnt_rows, z, top1, top2):
        cnt1 = jnp.zeros(s1.shape, F32)
        for a in range(k):
            cnt1 = jnp.where(rank1 == float(a), cnt_rows[a], cnt1)
        cnt1_ref[h] = cnt1
        w1_ref[h] = jnp.exp(s1 - top1) * (0.5 / z)
        rank2_ref[h] = rank2.astype(BF16).reshape(tiles)
        e2_ref[h] = jnp.exp(s2 - top2).astype(BF16).reshape(tiles)

    def finish(h, par, x, z):
        s1, s2 = scores(h)
        rank1, rank2 = r_scr[par, 0], r_scr[par, 1]
        taken = jnp.where(x == -jnp.inf, 1.0, 0.0)
        cnt_rows = ([jnp.sum(taken[0:k], axis=0, keepdims=True)]
                    + [jnp.sum(taken[k + 8 * (a - 1):k + 8 * a], axis=0, keepdims=True) for a in range(1, 8)]
                    + [taken[k + 56 + a:k + 57 + a] for a in range(8)])
        n_pairs = cnt_rows[0]
        for row in cnt_rows[1:]:
            n_pairs = n_pairs + row
        n1 = jnp.sum(jnp.where(rank1 < float(k), 1.0, 0.0), axis=0, keepdims=True)
        n2 = jnp.sum(jnp.where(rank2 < float(k), 1.0, 0.0), axis=0, keepdims=True)
        tied = (n1 != float(k)) | (n2 != float(k)) | (n_pairs != float(k))
        emit(h, s1, s2, rank1, rank2, cnt_rows, z, v_scr[par, 0, 0:1, :], v_scr[par, 1, 0:1, :])

        @pl.when(jnp.sum(jnp.where(tied, 1.0, 0.0)) > 0.0)
        def _():
            rank1 = _top16(s1, vx1_scr)
            rank2 = _top16(s2, vx2_scr)
            v1, v2 = vx1_scr[...], vx2_scr[...]
            top = v1[0:1] + v2[0:1]
            r16 = lax.broadcasted_iota(I32, (k, tn), 0)
            r8 = lax.broadcasted_iota(I32, (8, tn), 0)
            flat = jnp.concatenate([r16] + [a * k + r8 for a in range(1, 8)] + [(8 + r8) * k], axis=0)

            def pick_exact(r, carry):
                x, cnt, z = carry
                m = jnp.max(x, axis=0, keepdims=True)
                idx = jnp.min(jnp.where(x == m, flat, k * k), axis=0, keepdims=True)
                cnt = cnt + jnp.where(r16 == (idx >> 4), 1.0, 0.0)
                return jnp.where(flat == idx, -jnp.inf, x), cnt, z + jnp.exp(m - top)

            _, cnt, z = lax.fori_loop(0, k, pick_exact, (_pair_candidates(v1, v2), jnp.zeros((k, tn), F32),
                                                          jnp.zeros((1, tn), F32)))
            emit(h, s1, s2, rank1, rank2, [cnt[a:a + 1] for a in range(k)], z, v1[0:1], v2[0:1])

    pending = None
    for h in range(PEER_HEADS + 1):
        par = h % 2
        lists = h < PEER_HEADS
        init = []
        if lists:
            init += list(scores(h))
            for half in range(2):
                r_scr[par, half] = jnp.full((N_KEYS, tn), float(k), F32)
        if pending is not None:
            init += [pending[0], jnp.zeros((1, tn), F32)]
            top = pending[1]

        def body(r, carry, lists=lists, pairs=pending is not None, par=par):
            out = []
            if lists:
                for half in range(2):
                    x = carry[half]
                    m = jnp.max(x, axis=0, keepdims=True)
                    hit = x == m
                    v_scr[par, half, pl.ds(r, 1), :] = m
                    r_scr[par, half] = jnp.where(hit, lax.convert_element_type(r, F32), r_scr[par, half])
                    out.append(jnp.where(hit, -jnp.inf, x))
            if pairs:
                x, z = carry[-2], carry[-1]
                m = jnp.max(x, axis=0, keepdims=True)
                out += [jnp.where(x == m, -jnp.inf, x), z + jnp.exp(m - top)]
            return tuple(out)

        res = lax.fori_loop(0, k, body, tuple(init))
        if pending is not None:
            finish(h - 1, 1 - par, res[-2], res[-1])
        if lists:
            v1, v2 = v_scr[par, 0], v_scr[par, 1]
            pending = (_pair_candidates(v1, v2), v1[0:1] + v2[0:1])
        else:
            pending = None


def _peer_select(pqT, k1, k2):
    _, n = pqT.shape
    tn = LANES
    nt = n // tn
    tab = jax.ShapeDtypeStruct((PEER_HEADS, N_KEYS, n), F32)
    tab_spec = pl.BlockSpec((PEER_HEADS, N_KEYS, tn), lambda i: (0, 0, i))
    tabb = jax.ShapeDtypeStruct((PEER_HEADS, N_KEYS // BF16_ROWS, BF16_ROWS, n), BF16)
    tabb_spec = pl.BlockSpec((PEER_HEADS, N_KEYS // BF16_ROWS, BF16_ROWS, tn), lambda i: (0, 0, 0, i))
    list_scr = pltpu.VMEM((PEER_TOPK, tn), F32)
    return pl.pallas_call(
        _peer_select_kernel,
        grid=(nt,),
        in_specs=[pl.BlockSpec((pqT.shape[0], tn), lambda i: (0, i)),
                  pl.BlockSpec(k1.shape, lambda i: (0, 0)),
                  pl.BlockSpec(k2.shape, lambda i: (0, 0))],
        out_specs=(tab_spec, tab_spec, tabb_spec, tabb_spec),
        out_shape=(tab, tab, tabb, tabb),
        scratch_shapes=[pltpu.VMEM((2, 2, PEER_TOPK, tn), F32), pltpu.VMEM((2, 2, N_KEYS, tn), F32),
                        list_scr, list_scr],
        compiler_params=_cparams("arbitrary"),
        name="peer_select",
    )(pqT, k1, k2)


def _peer_dense_kernel(alpha, i1_per_step, h2T_ref, u_ref, vT_ref, cnt1_ref, w1_ref, rank2_ref, e2_ref,
                       x1_ref, g2_ref, lng_ref, lnb_ref, y_ref, acc_scr, aT_scr, act_scr):
    j = pl.program_id(1)

    @pl.when(j == 0)
    def _():
        acc_scr[...] = jnp.zeros(acc_scr.shape, F32)

    tn = aT_scr.shape[1]
    tiles = (N_KEYS // BF16_ROWS, BF16_ROWS, tn)
    zero = jnp.zeros(tiles, BF16)
    for ii in range(i1_per_step):
        i1 = j * i1_per_step + ii
        gate = zero
        for h in range(PEER_HEADS):
            c1 = jnp.broadcast_to(cnt1_ref[h, pl.ds(i1, 1), :], (BF16_ROWS, tn)).astype(BF16)
            w1 = jnp.broadcast_to(w1_ref[h, pl.ds(i1, 1), :], (BF16_ROWS, tn)).astype(BF16)
            gate = gate + jnp.where(rank2_ref[h] < c1[None], e2_ref[h], zero) * w1[None]
        act_scr[ii * N_KEYS:(ii + 1) * N_KEYS, :] = gate.reshape(N_KEYS, tn)
    aT_scr[...] = _dot(u_ref[...], h2T_ref[...])
    for ii in range(i1_per_step):
        rows = slice(ii * N_KEYS, (ii + 1) * N_KEYS)
        a = aT_scr[rows, :].astype(BF16)
        t = jnp.tanh(a * (_GELU_C1 + _GELU_C2 * (a * a)))
        act_scr[rows, :] = (a + a * t) * act_scr[rows, :]
    acc_scr[...] += _dot(vT_ref[...], act_scr[...])

    @pl.when(j == pl.num_programs(1) - 1)
    def _():
        peer = acc_scr[...].T
        y_ref[...] = _layer_norm(alpha * x1_ref[...] + g2_ref[0] * peer, lng_ref[...], lnb_ref[...])


def _peer_dense(alpha, h2T, u_b, vT_b, tabs, x1, g2, ln_g, ln_b, rows_per_vec):
    d, n = h2T.shape
    tn = min(512, rows_per_vec) if g2.shape[1] == 1 else g2.shape[1]
    tiles_per_vec = rows_per_vec // tn
    i1_per_step = 16
    te = i1_per_step * N_KEYS
    n_exp = u_b.shape[0]
    tab_spec = pl.BlockSpec((PEER_HEADS, N_KEYS, tn), lambda i, j: (0, 0, i))
    tabb_spec = pl.BlockSpec((PEER_HEADS, N_KEYS // BF16_ROWS, BF16_ROWS, tn), lambda i, j: (0, 0, 0, i))
    return pl.pallas_call(
        functools.partial(_peer_dense_kernel, alpha, i1_per_step),
        grid=(n // tn, n_exp // te),
        in_specs=[pl.BlockSpec((d, tn), lambda i, j: (0, i)),
                  pl.BlockSpec((te, d), lambda i, j: (j, 0)),
                  pl.BlockSpec((d, te), lambda i, j: (0, j)),
                  tab_spec, tab_spec, tabb_spec, tabb_spec,
                  pl.BlockSpec((tn, d), lambda i, j: (i, 0)),
                  pl.BlockSpec((1,) + g2.shape[1:], lambda i, j: (i // tiles_per_vec, 0, 0)),
                  pl.BlockSpec(ln_g.shape, lambda i, j: (0, 0)),
                  pl.BlockSpec(ln_b.shape, lambda i, j: (0, 0))],
        out_specs=pl.BlockSpec((tn, d), lambda i, j: (i, 0)),
        out_shape=jax.ShapeDtypeStruct((n, d), F32),
        scratch_shapes=[pltpu.VMEM((d, tn), F32), pltpu.VMEM((te, tn), F32), pltpu.VMEM((te, tn), BF16)],
        compiler_params=_cparams("arbitrary", "arbitrary"),
        name="peer_dense_ln2",
    )(h2T, u_b, vT_b, *tabs, x1, g2, ln_g, ln_b)


def _transpose_cast_kernel(x_ref, o_ref):
    o_ref[...] = x_ref[...].T.astype(BF16)


def _transpose_cast(x):
    r, c = x.shape
    tr = min(512, r)
    return pl.pallas_call(
        _transpose_cast_kernel,
        grid=(r // tr,),
        in_specs=[pl.BlockSpec((tr, c), lambda i: (i, 0))],
        out_specs=pl.BlockSpec((c, tr), lambda i: (0, i)),
        out_shape=jax.ShapeDtypeStruct((c, r), BF16),
        compiler_params=_cparams("arbitrary"),
        name="transpose_cast",
    )(x)


def _pack_w_in(w):
    widths = (ATTN_WIDTH, KV_WIDTH, KV_WIDTH, N_IDX_HEADS * IDX_DIM, N_IDX_HEADS, IDX_DIM,
              D_CONV, D_CONV, D_CONV, w.shape[0], w.shape[0])
    parts, start = [], 0
    for wd in widths:
        parts.append(w[:, start:start + wd])
        start += wd
    q, k, v, qi, wi, ki, bg, cg, xin, ga, gb = parts
    pad = jnp.zeros((w.shape[0], _KIWI - IDX_DIM - N_IDX_HEADS), w.dtype)
    return jnp.concatenate([q, k, v, qi, ki, wi, pad, bg, cg, xin, ga, gb], axis=1).astype(BF16)


def kernel(x_prompt, x_sample, c_prompt, c_sample, cache_k, cache_v, cache_kidx, state_conv, page_table,
           rel_bias, w_ada, b_ada, w_in, conv_w, conv_b, w_o_attn, w_o_conv, w_out, ln1_g, ln1_b,
           ln2_g, ln2_b, peer_wq, peer_k1, peer_k2, peer_u, peer_v):
    depth = w_in.shape[0]
    alpha = (2 * depth) ** 0.25
    bp, sp, d = x_prompt.shape
    bs, ts, _ = x_sample.shape
    n_pool = cache_k.shape[1]
    past = page_table.shape[1] * PAGE_SIZE
    ns = bs * ts

    xp, xs = x_prompt, x_sample.reshape(ns, d)
    outs = [[] for _ in range(8)]
    for l in range(depth):
        c_all = jnp.concatenate([c_prompt, c_sample], axis=0)
        c_all = jnp.pad(c_all, ((0, (-c_all.shape[0]) % 16), (0, 0)))
        mod = _ada(c_all, w_ada[l], b_ada[l])[:bp + bs]
        sh1, sc1, g1, sh2, sc2, g2 = jnp.split(mod, 6, axis=-1)
        as_p = lambda a: a[:bp].reshape(bp, 1, d)
        as_s = lambda a: jnp.repeat(a[bp:], ts, axis=0).reshape(1, ns, d)

        w_all = _pack_w_in(w_in[l])
        woa, woc, wout = w_o_attn[l].astype(BF16), w_o_conv[l].astype(BF16), w_out[l].astype(BF16)
        wq = peer_wq[l].astype(BF16)
        k1, k2 = peer_k1[l].astype(BF16), peer_k2[l].astype(BF16)
        u_b = peer_u[l].astype(BF16)
        vT_b = _transpose_cast(peer_v[l])
        lg1, lb1 = ln1_g[l].reshape(1, d), ln1_b[l].reshape(1, d)
        lg2, lb2 = ln2_g[l].reshape(1, d), ln2_b[l].reshape(1, d)

        (qT, k_p, v_p, kb_p, vTb_p, qiT, ki_p, kib_p, wiT, oconv_p, ga_p, gb_p, ulast_p) = _inproj_prompt(
            xp, as_p(sc1), as_p(sh1), w_all, conv_w[l], conv_b[l])
        oattn_p = _attn_prompt(rel_bias, qT, qiT, wiT, kb_p, vTb_p, kib_p)
        x1_p, h2T_p, pqT_p = _merge(alpha, xp.reshape(bp * sp, d), oattn_p.reshape(bp * sp, ATTN_WIDTH),
                                    oconv_p.reshape(bp * sp, D_CONV), ga_p.reshape(bp * sp, d),
                                    gb_p.reshape(bp * sp, d), as_p(g1), as_p(sc2), as_p(sh2),
                                    woa, woc, wout, wq, lg1, lb1, sp)
        tabs_p = _peer_select(pqT_p, k1, k2)
        y_p = _peer_dense(alpha, h2T_p, u_b, vT_b, tabs_p, x1_p, as_p(g2), lg2, lb2, sp)
        xp = y_p.reshape(bp, sp, d)

        st = state_conv[l]
        st0 = jnp.repeat(st[:, 0], ts, axis=0)
        st1 = jnp.repeat(st[:, 1], ts, axis=0)
        (q_s, k_s, v_s, qi_s, kiwi_s, oconv_s, ga_s, gb_s, u_s) = _inproj_sample(
            xs, as_s(sc1)[0], as_s(sh1)[0], w_all, conv_w[l], conv_b[l], st0, st1, ts)
        ki_s = kiwi_s[:, :IDX_DIM]
        wi_s = kiwi_s[:, IDX_DIM:IDX_DIM + N_IDX_HEADS]
        q5 = q_s.reshape(bs, ts, N_KV_HEADS, GROUP, HEAD_DIM).transpose(0, 2, 3, 1, 4)
        qpad = jnp.zeros((bs, N_KV_HEADS, GROUP * ts, N_KV_HEADS, HEAD_DIM), F32)
        for g in range(N_KV_HEADS):
            qpad = qpad.at[:, g, :, g, :].set(q5[:, g].reshape(bs, GROUP * ts, HEAD_DIM))
        qpad = qpad.reshape(bs, N_HEADS * ts, KV_WIDTH).astype(BF16)
        qi_r = qi_s.reshape(bs, ts, N_IDX_HEADS, IDX_DIM).transpose(0, 2, 1, 3).reshape(
            bs, N_IDX_HEADS * ts, IDX_DIM).astype(BF16)
        wi_r = jnp.broadcast_to(wi_s.reshape(bs, ts, N_IDX_HEADS).transpose(0, 2, 1).reshape(
            bs, N_IDX_HEADS * ts, 1), (bs, N_IDX_HEADS * ts, LANES))
        o_s = _attn_sample(page_table, rel_bias, qpad, qi_r, wi_r,
                           k_s.reshape(bs, ts, KV_WIDTH), v_s.reshape(bs, ts, KV_WIDTH),
                           ki_s.reshape(bs, ts, IDX_DIM),
                           cache_k[l].reshape(n_pool, PAGE_SIZE, KV_WIDTH),
                           cache_v[l].reshape(n_pool, PAGE_SIZE, KV_WIDTH), cache_kidx[l], past)
        o5 = o_s.reshape(bs, N_KV_HEADS, GROUP, ts, N_KV_HEADS, HEAD_DIM)
        oattn_s = jnp.stack([o5[:, g, :, :, g, :] for g in range(N_KV_HEADS)], axis=1)
        oattn_s = oattn_s.transpose(0, 3, 1, 2, 4).reshape(ns, ATTN_WIDTH).astype(BF16)
        x1_s, h2T_s, pqT_s = _merge(alpha, xs, oattn_s, oconv_s, ga_s, gb_s, as_s(g1), as_s(sc2), as_s(sh2),
                                    woa, woc, wout, wq, lg1, lb1, ns)
        tabs_s = _peer_select(pqT_s, k1, k2)
        xs = _peer_dense(alpha, h2T_s, u_b, vT_b, tabs_s, x1_s, as_s(g2), lg2, lb2, ns)

        for lst, val in zip(outs, (
                k_p.reshape(bp, sp, N_KV_HEADS, HEAD_DIM), v_p.reshape(bp, sp, N_KV_HEADS, HEAD_DIM), ki_p,
                ulast_p[:, 8 - (CONV_WIDTH - 1):],
                k_s.reshape(bs, ts, N_KV_HEADS, HEAD_DIM), v_s.reshape(bs, ts, N_KV_HEADS, HEAD_DIM),
                ki_s.reshape(bs, ts, IDX_DIM),
                u_s.reshape(bs, ts, D_CONV)[:, ts - (CONV_WIDTH - 1):])):
            lst.append(val)

    return (xp, xs.reshape(bs, ts, d)) + tuple(jnp.stack(o) for o in outs)
```

```python
import functools
import math

import jax
import jax.numpy as jnp
from jax import lax
from jax.experimental import pallas as pl
from jax.experimental.pallas import tpu as pltpu

F32 = jnp.float32
BF16 = jnp.bfloat16
I32 = jnp.int32

N_HEADS = 8
N_KV_HEADS = 2
HEAD_DIM = 64
GROUP = N_HEADS // N_KV_HEADS
ATTN_WIDTH = N_HEADS * HEAD_DIM
KV_WIDTH = N_KV_HEADS * HEAD_DIM
ATTN_SCALE = HEAD_DIM ** -0.5
N_IDX_HEADS = 4
IDX_DIM = 64
IDX_SCALE = (IDX_DIM * N_IDX_HEADS) ** -0.5
TOPK_MAX = 256
N_BUCKETS = 32
MAX_DISTANCE = 128
D_CONV = 512
CONV_WIDTH = 3
N_KEYS = 128
PEER_HEADS = 8
PEER_HALF = 64
PEER_TOPK = 16
LN_EPS = 1e-5
PAGE_SIZE = 128

LANES = 128
BF16_ROWS = 16
VMEM_LIMIT = 56 * 1024 * 1024

_GELU_C1 = math.sqrt(2.0 / math.pi)
_GELU_C2 = _GELU_C1 * 0.044715

NEG_BIG = -1e30

_KIWI = LANES
_OFF_Q = 0
_OFF_K = _OFF_Q + ATTN_WIDTH
_OFF_V = _OFF_K + KV_WIDTH
_OFF_QI = _OFF_V + KV_WIDTH
_OFF_KIWI = _OFF_QI + N_IDX_HEADS * IDX_DIM
_OFF_BG = _OFF_KIWI + _KIWI
_OFF_CG = _OFF_BG + D_CONV
_OFF_XIN = _OFF_CG + D_CONV
_OFF_GA = _OFF_XIN + D_CONV


def _cparams(*sem):
    return pltpu.CompilerParams(dimension_semantics=sem, vmem_limit_bytes=VMEM_LIMIT)


def _dot(a, b):
    return jnp.dot(a, b, preferred_element_type=F32)


def _dot_nt(a, b):
    return lax.dot_general(a, b, (((1,), (1,)), ((), ())), preferred_element_type=F32)


_COARSE_STEPS = 10


def _kth_largest(stats, count_ge, k, n_valid, v_min, v_max):
    enough = n_valid >= k
    lo0 = jnp.where(enough, v_min, -jnp.inf)
    hi0 = jnp.where(enough, v_max, -jnp.inf)

    def cond(state):
        lo, hi = state
        return jnp.max(jnp.where(lo < hi, 1.0, 0.0)) > 0.0

    def midpoint(lo, hi):
        mid = lo + (hi - lo) * 0.5
        return jnp.where(mid > lo, mid, hi)

    def coarse_step(_, state):
        lo, hi = state
        mid = midpoint(lo, hi)
        cnt = count_ge(mid)
        active = lo < hi
        return (jnp.where(active & (cnt >= k), mid, lo), jnp.where(active & (cnt < k), mid, hi))

    def step(state):
        lo, hi = state
        cnt, above, below = stats(midpoint(lo, hi))
        active = lo < hi
        return (jnp.where(active & (cnt >= k), above, lo), jnp.where(active & (cnt < k), below, hi))

    state = lax.fori_loop(0, _COARSE_STEPS, coarse_step, (lo0, hi0))
    lo, _ = lax.while_loop(cond, lambda state: step(step(state)), state)
    return lo


def _t5_bias(dist, relb_ref, h):
    n = jnp.maximum(dist, 0)
    max_exact = N_BUCKETS // 2
    nf = jnp.maximum(n, 1).astype(F32)
    large = max_exact + jnp.floor(jnp.log(nf / max_exact) / math.log(MAX_DISTANCE / max_exact)
                                  * (N_BUCKETS - max_exact)).astype(I32)
    large = jnp.minimum(large, N_BUCKETS - 1)
    bucket = jnp.where(n < max_exact, n, large)
    out = jnp.zeros(dist.shape, F32)
    for b in range(N_BUCKETS):
        out = jnp.where(bucket == b, relb_ref[b, h], out)
    return out


def _layer_norm(x, g, b):
    mu = jnp.mean(x, axis=-1, keepdims=True)
    var = jnp.mean(jnp.square(x - mu), axis=-1, keepdims=True)
    return (x - mu) * lax.rsqrt(var + LN_EPS) * g + b


def _ada_kernel(c_ref, w_ref, b_ref, o_ref):
    o_ref[...] = _dot(c_ref[...].astype(BF16), w_ref[...].astype(BF16)) + b_ref[...]


def _ada(c, w, b):
    m, d = c.shape
    n = w.shape[1]
    tn = n // 4
    return pl.pallas_call(
        _ada_kernel,
        grid=(n // tn,),
        in_specs=[pl.BlockSpec((m, d), lambda j: (0, 0)),
                  pl.BlockSpec((d, tn), lambda j: (0, j)),
                  pl.BlockSpec((1, tn), lambda j: (0, j))],
        out_specs=pl.BlockSpec((m, tn), lambda j: (0, j)),
        out_shape=jax.ShapeDtypeStruct((m, n), F32),
        compiler_params=_cparams("arbitrary"),
        name="ada_mod",
    )(c, w, b.reshape(1, n))


def _conv_out(bg, u, u1, u2, cw_ref, cb_ref):
    y = cb_ref[...] + u2 * cw_ref[0:1, :] + u1 * cw_ref[1:2, :] + u * cw_ref[2:3, :]
    return bg * y


def _inproj_prompt_kernel(x_ref, sc_ref, sh_ref, w_ref, cw_ref, cb_ref,
                          qT_ref, k_ref, v_ref, kb_ref, vTb_ref, qiT_ref, ki_ref, kib_ref, wiT_ref,
                          oconv_ref, ga_ref, gb_ref, ulast_ref, uprev_scr):
    s = pl.program_id(1)
    tm = x_ref.shape[1]
    h = (x_ref[0] * (1.0 + sc_ref[0]) + sh_ref[0]).astype(BF16)

    def proj(a, width):
        return _dot(h, w_ref[:, a:a + width])

    qT_ref[0] = proj(_OFF_Q, ATTN_WIDTH).T.astype(BF16)
    k = proj(_OFF_K, KV_WIDTH)
    k_ref[0] = k
    kb_ref[0] = k.astype(BF16)
    v = proj(_OFF_V, KV_WIDTH)
    v_ref[0] = v
    for i in range(tm // LANES):
        vTb_ref[0, i] = v[i * LANES:(i + 1) * LANES].T.astype(BF16)
    qiT_ref[0] = proj(_OFF_QI, N_IDX_HEADS * IDX_DIM).T.astype(BF16)
    kiwi = proj(_OFF_KIWI, _KIWI)
    ki_ref[0] = kiwi[:, :IDX_DIM]
    kib_ref[0] = kiwi[:, :IDX_DIM].astype(BF16)
    wiT_ref[0] = kiwi.T[IDX_DIM:IDX_DIM + 8]

    @pl.when(s == 0)
    def _():
        uprev_scr[...] = jnp.zeros(uprev_scr.shape, F32)

    bg = proj(_OFF_BG, D_CONV)
    u = proj(_OFF_CG, D_CONV) * proj(_OFF_XIN, D_CONV)
    row = lax.broadcasted_iota(I32, u.shape, 0)
    up = uprev_scr[...]
    u1 = jnp.where(row < 1, pltpu.roll(up, 1, 0), pltpu.roll(u, 1, 0))
    u2 = jnp.where(row < 2, pltpu.roll(up, 2, 0), pltpu.roll(u, 2, 0))
    oconv_ref[0] = _conv_out(bg, u, u1, u2, cw_ref, cb_ref).astype(BF16)
    uprev_scr[...] = u
    ulast_ref[0] = u[tm - 8:, :]

    ga_ref[0] = proj(_OFF_GA, w_ref.shape[0])
    gb_ref[0] = proj(_OFF_GA + w_ref.shape[0], w_ref.shape[0])


def _inproj_prompt(x, sc, sh, w_all, conv_w, conv_b):
    b, s, d = x.shape
    tm = min(512, s)
    nkb = tm // LANES
    row = lambda width, dt: jax.ShapeDtypeStruct((b, s, width), dt)
    out_shape = (
        jax.ShapeDtypeStruct((b, ATTN_WIDTH, s), BF16),
        row(KV_WIDTH, F32), row(KV_WIDTH, F32), row(KV_WIDTH, BF16),
        jax.ShapeDtypeStruct((b, s // LANES, KV_WIDTH, LANES), BF16),
        jax.ShapeDtypeStruct((b, N_IDX_HEADS * IDX_DIM, s), BF16),
        row(IDX_DIM, F32), row(IDX_DIM, BF16),
        jax.ShapeDtypeStruct((b, 8, s), F32),
        row(D_CONV, BF16),
        row(d, F32), row(d, F32),
        jax.ShapeDtypeStruct((b, 8, D_CONV), F32),
    )
    tile = lambda width: pl.BlockSpec((1, tm, width), lambda i, j: (i, j, 0))
    tileT = lambda rows: pl.BlockSpec((1, rows, tm), lambda i, j: (i, 0, j))
    out_specs = (
        tileT(ATTN_WIDTH), tile(KV_WIDTH), tile(KV_WIDTH), tile(KV_WIDTH),
        pl.BlockSpec((1, nkb, KV_WIDTH, LANES), lambda i, j: (i, j, 0, 0)),
        tileT(N_IDX_HEADS * IDX_DIM), tile(IDX_DIM), tile(IDX_DIM), tileT(8),
        tile(D_CONV), tile(d), tile(d),
        pl.BlockSpec((1, 8, D_CONV), lambda i, j: (i, 0, 0)),
    )
    vec = pl.BlockSpec((1, 1, d), lambda i, j: (i, 0, 0))
    return pl.pallas_call(
        _inproj_prompt_kernel,
        grid=(b, s // tm),
        in_specs=[tile(d), vec, vec,
                  pl.BlockSpec(w_all.shape, lambda i, j: (0, 0)),
                  pl.BlockSpec(conv_w.shape, lambda i, j: (0, 0)),
                  pl.BlockSpec((1, D_CONV), lambda i, j: (0, 0))],
        out_specs=out_specs,
        out_shape=out_shape,
        scratch_shapes=[pltpu.VMEM((tm, D_CONV), F32)],
        compiler_params=_cparams("arbitrary", "arbitrary"),
        name="inproj_prompt",
    )(x, sc, sh, w_all, conv_w, conv_b.reshape(1, D_CONV))


def _inproj_sample_kernel(t_seq, x_ref, sc_ref, sh_ref, w_ref, cw_ref, cb_ref, s0_ref, s1_ref,
                          q_ref, k_ref, v_ref, qi_ref, kiwi_ref, oconv_ref, ga_ref, gb_ref, u_ref):
    h = (x_ref[...] * (1.0 + sc_ref[...]) + sh_ref[...]).astype(BF16)

    def proj(a, width):
        return _dot(h, w_ref[:, a:a + width])

    q_ref[...] = proj(_OFF_Q, ATTN_WIDTH)
    k_ref[...] = proj(_OFF_K, KV_WIDTH)
    v_ref[...] = proj(_OFF_V, KV_WIDTH)
    qi_ref[...] = proj(_OFF_QI, N_IDX_HEADS * IDX_DIM)
    kiwi_ref[...] = proj(_OFF_KIWI, _KIWI)
    bg = proj(_OFF_BG, D_CONV)
    u = proj(_OFF_CG, D_CONV) * proj(_OFF_XIN, D_CONV)
    t = lax.broadcasted_iota(I32, u.shape, 0) % t_seq
    u1 = jnp.where(t == 0, s1_ref[...], pltpu.roll(u, 1, 0))
    u2 = jnp.where(t == 0, s0_ref[...], jnp.where(t == 1, s1_ref[...], pltpu.roll(u, 2, 0)))
    oconv_ref[...] = _conv_out(bg, u, u1, u2, cw_ref, cb_ref).astype(BF16)
    u_ref[...] = u
    ga_ref[...] = proj(_OFF_GA, w_ref.shape[0])
    gb_ref[...] = proj(_OFF_GA + w_ref.shape[0], w_ref.shape[0])


def _inproj_sample(x, sc_rows, sh_rows, w_all, conv_w, conv_b, st0_rows, st1_rows, t_seq):
    n, d = x.shape
    full = lambda a: pl.BlockSpec(a.shape, lambda i: (0,) * a.ndim)
    o = lambda width, dt: jax.ShapeDtypeStruct((n, width), dt)
    out_shape = (o(ATTN_WIDTH, F32), o(KV_WIDTH, F32), o(KV_WIDTH, F32), o(N_IDX_HEADS * IDX_DIM, F32),
                 o(_KIWI, F32), o(D_CONV, BF16), o(d, F32), o(d, F32), o(D_CONV, F32))
    args = (x, sc_rows, sh_rows, w_all, conv_w, conv_b.reshape(1, D_CONV), st0_rows, st1_rows)
    return pl.pallas_call(
        functools.partial(_inproj_sample_kernel, t_seq),
        grid=(1,),
        in_specs=[full(a) for a in args],
        out_specs=tuple(pl.BlockSpec(s.shape, lambda i: (0, 0)) for s in out_shape),
        out_shape=out_shape,
        compiler_params=_cparams("arbitrary"),
        name="inproj_sample",
    )(*args)


def _attn_prompt_kernel(n_sel, relb_ref, qT_ref, qiT_ref, wiT_ref, kb_ref, vTb_ref, kib_ref,
                        o_ref, bias_scr, key_scr):
    qb = pl.program_id(1)
    blk = LANES
    kblk = 2 * blk
    ntrip = qb // 2 + 1

    @pl.when((pl.program_id(0) == 0) & (qb == 0))
    def _():
        j = lax.broadcasted_iota(I32, (blk, blk), 0)
        i = lax.broadcasted_iota(I32, (blk, blk), 1)
        for delta in range(3):
            dist = i - j + blk * delta
            for h in range(N_HEADS):
                g, r = divmod(h, GROUP)
                bias_scr[delta, g, :, r * blk:(r + 1) * blk] = _t5_bias(dist, relb_ref, h)

    row = lax.broadcasted_iota(I32, (kblk, blk), 0)
    lane = lax.broadcasted_iota(I32, (kblk, blk), 1)

    def valid_mask(t):
        return (t * kblk + row) <= (qb * blk + lane)

    def key_rows(t):
        return pl.ds(pl.multiple_of(t * kblk, kblk), kblk)

    qiT = jnp.concatenate([qiT_ref[0, h * IDX_DIM:(h + 1) * IDX_DIM, :] for h in range(N_IDX_HEADS)], axis=1)
    wi = [wiT_ref[0, h:h + 1, :] for h in range(N_IDX_HEADS)]
    pr = 32
    fold = lambda a: a.reshape(kblk // pr, pr, blk)
    tall = lambda a: jnp.sum(a, axis=0, keepdims=True)
    part0 = (jnp.zeros((pr, blk), F32), jnp.full((pr, blk), jnp.inf, F32), jnp.full((pr, blk), -jnp.inf, F32))

    def score_blk(t, carry):
        cnt, lo, hi = carry
        d = _dot(kib_ref[0, key_rows(t), :], qiT)
        s = jnp.maximum(d[:, 0:blk], 0.0) * wi[0]
        for h in range(1, N_IDX_HEADS):
            s = s + jnp.maximum(d[:, h * blk:(h + 1) * blk], 0.0) * wi[h]
        s = s * IDX_SCALE
        valid = valid_mask(t)
        key_scr[key_rows(t), :] = jnp.where(valid, s, -jnp.inf)
        return (cnt + jnp.sum(fold(jnp.where(valid, 1.0, 0.0)), axis=0),
                jnp.minimum(lo, jnp.min(fold(jnp.where(valid, s, jnp.inf)), axis=0)),
                jnp.maximum(hi, jnp.max(fold(jnp.where(valid, s, -jnp.inf)), axis=0)))

    cnt, lo, hi = lax.fori_loop(0, ntrip, score_blk, part0)
    n_valid = tall(cnt)
    v_min = jnp.min(lo, axis=0, keepdims=True)
    v_max = jnp.max(hi, axis=0, keepdims=True)

    def stats(mid):
        def body(t, carry):
            cnt, above, below = carry
            x = key_scr[key_rows(t), :]
            ge = x >= mid
            return (cnt + jnp.sum(fold(jnp.where(ge, 1.0, 0.0)), axis=0),
                    jnp.minimum(above, jnp.min(fold(jnp.where(ge, x, jnp.inf)), axis=0)),
                    jnp.maximum(below, jnp.max(fold(jnp.where(ge, -jnp.inf, x)), axis=0)))
        cnt, above, below = lax.fori_loop(0, ntrip, body, part0)
        return tall(cnt), jnp.min(above, axis=0, keepdims=True), jnp.max(below, axis=0, keepdims=True)

    def count(pred):
        def body(t, acc):
            return acc + jnp.sum(fold(jnp.where(pred(key_scr[key_rows(t), :]), 1.0, 0.0)), axis=0)
        return tall(lax.fori_loop(0, ntrip, body, jnp.zeros((pr, blk), F32)))

    thr = _kth_largest(stats, lambda mid: count(lambda x: x >= mid), float(n_sel), n_valid, v_min, v_max)

    need = n_sel - count(lambda x: x > thr)
    ltri = (lax.broadcasted_iota(I32, (kblk, kblk), 1) <= lax.broadcasted_iota(I32, (kblk, kblk), 0)).astype(BF16)

    def select_mask(t, ties_before):
        kv = key_scr[key_rows(t), :]
        eq = kv == thr
        prefix = _dot(ltri, eq.astype(BF16)) + ties_before
        sel = ((kv > thr) | (eq & (prefix <= need))) & valid_mask(t)
        return jnp.where(sel, 0.0, NEG_BIG), prefix[kblk - 1:kblk, :]

    zero = jnp.zeros((HEAD_DIM, blk), BF16)
    scale = jnp.asarray(ATTN_SCALE, BF16)
    qTg = []
    for g in range(N_KV_HEADS):
        cols = []
        for r in range(GROUP):
            base = g * GROUP * HEAD_DIM + r * HEAD_DIM
            parts = [zero] * N_KV_HEADS
            parts[g] = qT_ref[0, base:base + HEAD_DIM, :] * scale
            cols.append(jnp.concatenate(parts, axis=0))
        qTg.append(jnp.concatenate(cols, axis=1))
    width = GROUP * blk

    def att_blk(t, carry):
        kmat = kb_ref[0, key_rows(t), :]
        mask1, ties = select_mask(t, carry[N_KV_HEADS])
        mask = jnp.concatenate([mask1] * GROUP, axis=1)
        d0 = jnp.clip(qb - 2 * t, 0, 2)
        d1 = jnp.clip(qb - 2 * t - 1, 0, 2)
        new = []
        for g in range(N_KV_HEADS):
            m, l, acc = carry[g]
            bias = jnp.concatenate([bias_scr[d0, g], bias_scr[d1, g]], axis=0)
            lg = _dot(kmat, qTg[g]) + bias + mask
            m_new = jnp.maximum(m, jnp.max(lg, axis=0, keepdims=True))
            p = jnp.exp(lg - m_new)
            alpha = jnp.exp(m - m_new)
            l = alpha * l + jnp.sum(p, axis=0, keepdims=True)
            rows = slice(g * HEAD_DIM, (g + 1) * HEAD_DIM)
            vt = jnp.concatenate([vTb_ref[0, 2 * t, rows, :], vTb_ref[0, 2 * t + 1, rows, :]], axis=1)
            acc = alpha * acc + _dot(vt, p.astype(BF16))
            new.append((m_new, l, acc))
        return tuple(new) + (ties,)

    init = (jnp.full((1, width), NEG_BIG, F32), jnp.zeros((1, width), F32), jnp.zeros((HEAD_DIM, width), F32))
    res = lax.fori_loop(0, ntrip, att_blk, (init,) * N_KV_HEADS + (jnp.zeros((1, blk), F32),))
    outs = []
    for g in range(N_KV_HEADS):
        _, l, acc = res[g]
        og = acc / l
        outs += [og[:, r * blk:(r + 1) * blk] for r in range(GROUP)]
    o_ref[0] = jnp.concatenate(outs, axis=0).T.astype(BF16)


def _attn_prompt(rel_bias, qT, qiT, wiT, kb, vTb, kib):
    b, _, s = qT.shape
    n_sel = min(TOPK_MAX, s // 4)
    blk = LANES
    per_b = lambda a: pl.BlockSpec((1,) + a.shape[1:], lambda i, j: (i,) + (0,) * (a.ndim - 1))
    qtile = lambda rows: pl.BlockSpec((1, rows, blk), lambda i, j: (i, 0, j))
    return pl.pallas_call(
        functools.partial(_attn_prompt_kernel, n_sel),
        grid=(b, s // blk),
        in_specs=[pl.BlockSpec(memory_space=pltpu.SMEM),
                  qtile(ATTN_WIDTH), qtile(N_IDX_HEADS * IDX_DIM), qtile(8),
                  per_b(kb), per_b(vTb), per_b(kib)],
        out_specs=pl.BlockSpec((1, blk, ATTN_WIDTH), lambda i, j: (i, j, 0)),
        out_shape=jax.ShapeDtypeStruct((b, s, ATTN_WIDTH), BF16),
        scratch_shapes=[pltpu.VMEM((3, N_KV_HEADS, blk, GROUP * blk), F32),
                        pltpu.VMEM((s, blk), F32)],
        compiler_params=_cparams("arbitrary", "arbitrary"),
        name="attn_prompt",
    )(rel_bias, qT, qiT, wiT, kb, vTb, kib)


def _attn_sample_kernel(n_sel, past, t_seq, pt_ref, relb_ref, qpad_ref, qi_ref, wi_ref,
                        knew_ref, vnew_ref, kinew_ref, ck_hbm, cv_hbm, cki_hbm,
                        o_ref, kbuf, vbuf, kibuf, sem):
    b = pl.program_id(0)
    nb = pl.num_programs(0)
    n_pages = past // PAGE_SIZE
    lp = kbuf.shape[1]
    blk = LANES
    nblk = lp // blk
    slot = b % 2

    def page_copies(seq, sl, p):
        phys = pt_ref[seq, p]
        rows = pl.ds(pl.multiple_of(p * PAGE_SIZE, PAGE_SIZE), PAGE_SIZE)
        return (pltpu.make_async_copy(ck_hbm.at[phys], kbuf.at[sl, rows, :], sem.at[sl, 0]),
                pltpu.make_async_copy(cv_hbm.at[phys], vbuf.at[sl, rows, :], sem.at[sl, 1]),
                pltpu.make_async_copy(cki_hbm.at[phys], kibuf.at[sl, rows, :], sem.at[sl, 2]))

    def start_all(seq, sl):
        def body(p, c):
            for i, cp in enumerate(page_copies(seq, sl, p)):
                cp.start(priority=i % 2)
            return c
        lax.fori_loop(0, n_pages, body, 0)

    def wait_all(seq, sl):
        def body(p, c):
            for cp in page_copies(seq, sl, p):
                cp.wait()
            return c
        lax.fori_loop(0, n_pages, body, 0)

    @pl.when(b == 0)
    def _():
        for sl in range(2):
            kbuf[sl, past:, :] = jnp.zeros((blk, KV_WIDTH), F32)
            vbuf[sl, past:, :] = jnp.zeros((blk, KV_WIDTH), F32)
            kibuf[sl, past:, :] = jnp.zeros((blk, IDX_DIM), F32)
        start_all(0, 0)

    @pl.when(b + 1 < nb)
    def _():
        start_all(b + 1, 1 - slot)

    kbuf[slot, past:past + t_seq, :] = knew_ref[0]
    vbuf[slot, past:past + t_seq, :] = vnew_ref[0]
    kibuf[slot, past:past + t_seq, :] = kinew_ref[0]
    wait_all(b, slot)

    d = _dot_nt(qi_ref[0], kibuf[slot].astype(BF16))
    s = jnp.maximum(d[0:t_seq], 0.0) * wi_ref[0, 0:t_seq, 0:1]
    for h in range(1, N_IDX_HEADS):
        s = s + jnp.maximum(d[h * t_seq:(h + 1) * t_seq], 0.0) * wi_ref[0, h * t_seq:(h + 1) * t_seq, 0:1]
    kpos = lax.broadcasted_iota(I32, (t_seq, lp), 1)
    qpos = past + lax.broadcasted_iota(I32, (t_seq, lp), 0)
    valid = kpos <= qpos
    s = s * IDX_SCALE
    key = jnp.where(valid, s, -jnp.inf)

    def count(pred):
        return jnp.sum(jnp.where(pred, 1.0, 0.0), axis=1, keepdims=True)

    def stats(mid):
        ge = key >= mid
        return (count(ge), jnp.min(jnp.where(ge, key, jnp.inf), axis=1, keepdims=True),
                jnp.max(jnp.where(ge, -jnp.inf, key), axis=1, keepdims=True))

    thr = _kth_largest(stats, lambda mid: count(key >= mid), float(n_sel), count(valid),
                       jnp.min(jnp.where(valid, s, jnp.inf), axis=1, keepdims=True),
                       jnp.max(key, axis=1, keepdims=True))
    need = n_sel - count(key > thr)
    eq = key == thr
    eqf = jnp.where(eq, 1.0, 0.0)
    utri = (lax.broadcasted_iota(I32, (blk, blk), 0) <= lax.broadcasted_iota(I32, (blk, blk), 1)).astype(BF16)
    carry = jnp.zeros((t_seq, 1), F32)
    ranks = []
    for kb in range(nblk):
        e = eqf[:, kb * blk:(kb + 1) * blk]
        ranks.append(_dot(e.astype(BF16), utri) + carry)
        carry = carry + jnp.sum(e, axis=1, keepdims=True)
    prefix = jnp.concatenate(ranks, axis=1)
    sel = jnp.where(((key > thr) | (eq & (prefix <= need))) & valid, 1.0, 0.0)

    lg = _dot_nt(qpad_ref[0], kbuf[slot].astype(BF16)) * ATTN_SCALE
    far = lp - 2 * blk
    near_dist = (past + lax.broadcasted_iota(I32, (t_seq, 2 * blk), 0)
                 - (far + lax.broadcasted_iota(I32, (t_seq, 2 * blk), 1)))
    bias_rows = []
    for h in range(N_HEADS):
        far_bias = jnp.full((t_seq, far), relb_ref[N_BUCKETS - 1, h], F32)
        bias_rows.append(jnp.concatenate([far_bias, _t5_bias(near_dist, relb_ref, h)], axis=1))
    lg = lg + jnp.concatenate(bias_rows, axis=0)
    selh = jnp.concatenate([sel] * N_HEADS, axis=0) > 0.0
    m = jnp.max(jnp.where(selh, lg, NEG_BIG), axis=1, keepdims=True)
    p = jnp.where(selh, jnp.exp(lg - m), 0.0)
    l = jnp.sum(p, axis=1, keepdims=True)
    o = _dot(p.astype(BF16), vbuf[slot].astype(BF16))
    o_ref[0] = o / l


def _attn_sample(page_table, rel_bias, qpad, qi_r, wi_r, k_new, v_new, ki_new, cache_k, cache_v, cache_ki,
                 past):
    nb, t_seq, _ = k_new.shape
    n_sel = min(TOPK_MAX, (past + t_seq) // 4)
    lp = past + LANES
    rows = N_HEADS * t_seq
    per_b = lambda a: pl.BlockSpec((1,) + a.shape[1:], lambda i, pt: (i,) + (0,) * (a.ndim - 1))
    hbm = pl.BlockSpec(memory_space=pl.ANY)
    grid_spec = pltpu.PrefetchScalarGridSpec(
        num_scalar_prefetch=1,
        grid=(nb,),
        in_specs=[pl.BlockSpec(memory_space=pltpu.SMEM),
                  per_b(qpad), per_b(qi_r), per_b(wi_r), per_b(k_new), per_b(v_new), per_b(ki_new),
                  hbm, hbm, hbm],
        out_specs=pl.BlockSpec((1, rows, KV_WIDTH), lambda i, pt: (i, 0, 0)),
        scratch_shapes=[pltpu.VMEM((2, lp, KV_WIDTH), F32),
                        pltpu.VMEM((2, lp, KV_WIDTH), F32),
                        pltpu.VMEM((2, lp, IDX_DIM), F32),
                        pltpu.SemaphoreType.DMA((2, 3))],
    )
    return pl.pallas_call(
        functools.partial(_attn_sample_kernel, n_sel, past, t_seq),
        grid_spec=grid_spec,
        out_shape=jax.ShapeDtypeStruct((nb, rows, KV_WIDTH), F32),
        compiler_params=_cparams("arbitrary"),
        name="attn_sample",
    )(page_table, rel_bias, qpad, qi_r, wi_r, k_new, v_new, ki_new, cache_k, cache_v, cache_ki)


def _merge_kernel(alpha, x_ref, oa_ref, oc_ref, ga_ref, gb_ref, g1_ref, sc2_ref, sh2_ref,
                  woa_ref, woc_ref, wout_ref, wq_ref, lng_ref, lnb_ref,
                  x1_ref, h2T_ref, pqT_ref):
    ta = _dot(oa_ref[...], woa_ref[...])
    tc = _dot(oc_ref[...], woc_ref[...])
    merged = jax.nn.sigmoid(ga_ref[...]) * ta + jax.nn.sigmoid(gb_ref[...]) * tc
    out = _dot(merged.astype(BF16), wout_ref[...])
    x1 = _layer_norm(alpha * x_ref[...] + g1_ref[0] * out, lng_ref[...], lnb_ref[...])
    x1_ref[...] = x1
    h2 = x1 * (1.0 + sc2_ref[0]) + sh2_ref[0]
    h2b = h2.astype(BF16)
    h2T_ref[...] = h2.T.astype(BF16)
    pqT_ref[...] = _dot(h2b, wq_ref[...]).T.astype(BF16)


def _merge(alpha, x, oattn, oconv, ga, gb, g1, sc2, sh2, woa, woc, wout, wq, ln_g, ln_b, rows_per_vec):
    n, d = x.shape
    tm = min(512, rows_per_vec) if g1.shape[1] == 1 else g1.shape[1]
    tiles_per_vec = rows_per_vec // tm
    tile = lambda width: pl.BlockSpec((tm, width), lambda i: (i, 0))
    vec = pl.BlockSpec((1,) + g1.shape[1:], lambda i: (i // tiles_per_vec, 0, 0))
    full = lambda a: pl.BlockSpec(a.shape, lambda i: (0,) * a.ndim)
    pq_w = wq.shape[1]
    return pl.pallas_call(
        functools.partial(_merge_kernel, alpha),
        grid=(n // tm,),
        in_specs=[tile(d), tile(ATTN_WIDTH), tile(D_CONV), tile(d), tile(d), vec, vec, vec,
                  full(woa), full(woc), full(wout), full(wq), full(ln_g), full(ln_b)],
        out_specs=(tile(d), pl.BlockSpec((d, tm), lambda i: (0, i)), pl.BlockSpec((pq_w, tm), lambda i: (0, i))),
        out_shape=(jax.ShapeDtypeStruct((n, d), F32), jax.ShapeDtypeStruct((d, n), BF16),
                   jax.ShapeDtypeStruct((pq_w, n), BF16)),
        compiler_params=_cparams("arbitrary"),
        name="merge_ln1_peerq",
    )(x, oattn, oconv, ga, gb, g1, sc2, sh2, woa, woc, wout, wq, ln_g, ln_b)


def _top16(s, v_scr):
    iota = lax.broadcasted_iota(I32, s.shape, 0)

    def body(r, carry):
        x, rank = carry
        m = jnp.max(x, axis=0, keepdims=True)
        idx = jnp.min(jnp.where(x == m, iota, N_KEYS), axis=0, keepdims=True)
        hit = iota == idx
        v_scr[pl.ds(r, 1), :] = m
        return jnp.where(hit, -jnp.inf, x), jnp.where(hit, lax.convert_element_type(r, F32), rank)

    _, rank = lax.fori_loop(0, PEER_TOPK, body, (s, jnp.full(s.shape, float(PEER_TOPK), F32)))
    return rank


def _pair_candidates(v1, v2):
    return jnp.concatenate([v1[0:1] + v2] + [v1[a:a + 1] + v2[0:8] for a in range(1, 8)]
                           + [v1[8:16] + v2[0:1]], axis=0)


def _peer_select_kernel(pqT_ref, k1_ref, k2_ref, cnt1_ref, w1_ref, rank2_ref, e2_ref,
                        v_scr, r_scr, vx1_scr, vx2_scr):
    tn = pqT_ref.shape[1]
    k = PEER_TOPK
    tiles = (N_KEYS // BF16_ROWS, BF16_ROWS, tn)

    def scores(h):
        base = h * 2 * PEER_HALF
        return (_dot(k1_ref[...], pqT_ref[base:base + PEER_HALF, :]),
                _dot(k2_ref[...], pqT_ref[base + PEER_HALF:base + 2 * PEER_HALF, :]))

    def emit(h, s1, s2, rank1, rank2, cnt_rows, z, top1, top2):
        cnt1 = jnp.zeros(s1.shape, F32)
        for a in range(k):
            cnt1 = jnp.where(rank1 == float(a), cnt_rows[a], cnt1)
        cnt1_ref[h] = cnt1
        w1_ref[h] = jnp.exp(s1 - top1) * (0.5 / z)
        rank2_ref[h] = rank2.astype(BF16).reshape(tiles)
        e2_ref[h] = jnp.exp(s2 - top2).astype(BF16).reshape(tiles)

    def finish(h, par, x, z):
        s1, s2 = scores(h)
        rank1, rank2 = r_scr[par, 0], r_scr[par, 1]
        taken = jnp.where(x == -jnp.inf, 1.0, 0.0)
        cnt_rows = ([jnp.sum(taken[0:k], axis=0, keepdims=True)]
                    + [jnp.sum(taken[k + 8 * (a - 1):k + 8 * a], axis=0, keepdims=True) for a in range(1, 8)]
                    + [taken[k + 56 + a:k + 57 + a] for a in range(8)])
        n_pairs = cnt_rows[0]
        for row in cnt_rows[1:]:
            n_pairs = n_pairs + row
        n1 = jnp.sum(jnp.where(rank1 < float(k), 1.0, 0.0), axis=0, keepdims=True)
        n2 = jnp.sum(jnp.where(rank2 < float(k), 1.0, 0.0), axis=0, keepdims=True)
        tied = (n1 != float(k)) | (n2 != float(k)) | (n_pairs != float(k))
        emit(h, s1, s2, rank1, rank2, cnt_rows, z, v_scr[par, 0, 0:1, :], v_scr[par, 1, 0:1, :])

        @pl.when(jnp.sum(jnp.where(tied, 1.0, 0.0)) > 0.0)
        def _():
            rank1 = _top16(s1, vx1_scr)
            rank2 = _top16(s2, vx2_scr)
            v1, v2 = vx1_scr[...], vx2_scr[...]
            top = v1[0:1] + v2[0:1]
            r16 = lax.broadcasted_iota(I32, (k, tn), 0)
            r8 = lax.broadcasted_iota(I32, (8, tn), 0)
            flat = jnp.concatenate([r16] + [a * k + r8 for a in range(1, 8)] + [(8 + r8) * k], axis=0)

            def pick_exact(r, carry):
                x, cnt, z = carry
                m = jnp.max(x, axis=0, keepdims=True)
                idx = jnp.min(jnp.where(x == m, flat, k * k), axis=0, keepdims=True)
                cnt = cnt + jnp.where(r16 == (idx >> 4), 1.0, 0.0)
                return jnp.where(flat == idx, -jnp.inf, x), cnt, z + jnp.exp(m - top)

            _, cnt, z = lax.fori_loop(0, k, pick_exact, (_pair_candidates(v1, v2), jnp.zeros((k, tn), F32),
                                                          jnp.zeros((1, tn), F32)))
            emit(h, s1, s2, rank1, rank2, [cnt[a:a + 1] for a in range(k)], z, v1[0:1], v2[0:1])

    pending = None
    for h in range(PEER_HEADS + 1):
        par = h % 2
        lists = h < PEER_HEADS
        init = []
        if lists:
            init += list(scores(h))
            for half in range(2):
                r_scr[par, half] = jnp.full((N_KEYS, tn), float(k), F32)
        if pending is not None:
            init += [pending[0], jnp.zeros((1, tn), F32)]
            top = pending[1]

        def body(r, carry, lists=lists, pairs=pending is not None, par=par):
            out = []
            if lists:
                for half in range(2):
                    x = carry[half]
                    m = jnp.max(x, axis=0, keepdims=True)
                    hit = x == m
                    v_scr[par, half, pl.ds(r, 1), :] = m
                    r_scr[par, half] = jnp.where(hit, lax.convert_element_type(r, F32), r_scr[par, half])
                    out.append(jnp.where(hit, -jnp.inf, x))
            if pairs:
                x, z = carry[-2], carry[-1]
                m = jnp.max(x, axis=0, keepdims=True)
                out += [jnp.where(x == m, -jnp.inf, x), z + jnp.exp(m - top)]
            return tuple(out)

        res = lax.fori_loop(0, k, body, tuple(init))
        if pending is not None:
            finish(h - 1, 1 - par, res[-2], res[-1])
        if lists:
            v1, v2 = v_scr[par, 0], v_scr[par, 1]
            pending = (_pair_candidates(v1, v2), v1[0:1] + v2[0:1])
        else:
            pending = None


def _peer_select(pqT, k1, k2):
    _, n = pqT.shape
    tn = LANES
    nt = n // tn
    tab = jax.ShapeDtypeStruct((PEER_HEADS, N_KEYS, n), F32)
    tab_spec = pl.BlockSpec((PEER_HEADS, N_KEYS, tn), lambda i: (0, 0, i))
    tabb = jax.ShapeDtypeStruct((PEER_HEADS, N_KEYS // BF16_ROWS, BF16_ROWS, n), BF16)
    tabb_spec = pl.BlockSpec((PEER_HEADS, N_KEYS // BF16_ROWS, BF16_ROWS, tn), lambda i: (0, 0, 0, i))
    list_scr = pltpu.VMEM((PEER_TOPK, tn), F32)
    return pl.pallas_call(
        _peer_select_kernel,
        grid=(nt,),
        in_specs=[pl.BlockSpec((pqT.shape[0], tn), lambda i: (0, i)),
                  pl.BlockSpec(k1.shape, lambda i: (0, 0)),
                  pl.BlockSpec(k2.shape, lambda i: (0, 0))],
        out_specs=(tab_spec, tab_spec, tabb_spec, tabb_spec),
        out_shape=(tab, tab, tabb, tabb),
        scratch_shapes=[pltpu.VMEM((2, 2, PEER_TOPK, tn), F32), pltpu.VMEM((2, 2, N_KEYS, tn), F32),
                        list_scr, list_scr],
        compiler_params=_cparams("arbitrary"),
        name="peer_select",
    )(pqT, k1, k2)


def _peer_dense_kernel(alpha, i1_per_step, h2T_ref, u_ref, vT_ref, cnt1_ref, w1_ref, rank2_ref, e2_ref,
                       x1_ref, g2_ref, lng_ref, lnb_ref, y_ref, acc_scr, aT_scr, act_scr):
    j = pl.program_id(1)

    @pl.when(j == 0)
    def _():
        acc_scr[...] = jnp.zeros(acc_scr.shape, F32)

    tn = aT_scr.shape[1]
    tiles = (N_KEYS // BF16_ROWS, BF16_ROWS, tn)
    zero = jnp.zeros(tiles, BF16)
    for ii in range(i1_per_step):
        i1 = j * i1_per_step + ii
        gate = zero
        for h in range(PEER_HEADS):
            c1 = jnp.broadcast_to(cnt1_ref[h, pl.ds(i1, 1), :], (BF16_ROWS, tn)).astype(BF16)
            w1 = jnp.broadcast_to(w1_ref[h, pl.ds(i1, 1), :], (BF16_ROWS, tn)).astype(BF16)
            gate = gate + jnp.where(rank2_ref[h] < c1[None], e2_ref[h], zero) * w1[None]
        act_scr[ii * N_KEYS:(ii + 1) * N_KEYS, :] = gate.reshape(N_KEYS, tn)
    aT_scr[...] = _dot(u_ref[...], h2T_ref[...])
    for ii in range(i1_per_step):
        rows = slice(ii * N_KEYS, (ii + 1) * N_KEYS)
        a = aT_scr[rows, :].astype(BF16)
        t = jnp.tanh(a * (_GELU_C1 + _GELU_C2 * (a * a)))
        act_scr[rows, :] = (a + a * t) * act_scr[rows, :]
    acc_scr[...] += _dot(vT_ref[...], act_scr[...])

    @pl.when(j == pl.num_programs(1) - 1)
    def _():
        peer = acc_scr[...].T
        y_ref[...] = _layer_norm(alpha * x1_ref[...] + g2_ref[0] * peer, lng_ref[...], lnb_ref[...])


def _peer_dense(alpha, h2T, u_b, vT_b, tabs, x1, g2, ln_g, ln_b, rows_per_vec):
    d, n = h2T.shape
    tn = min(512, rows_per_vec) if g2.shape[1] == 1 else g2.shape[1]
    tiles_per_vec = rows_per_vec // tn
    i1_per_step = 16
    te = i1_per_step * N_KEYS
    n_exp = u_b.shape[0]
    tab_spec = pl.BlockSpec((PEER_HEADS, N_KEYS, tn), lambda i, j: (0, 0, i))
    tabb_spec = pl.BlockSpec((PEER_HEADS, N_KEYS // BF16_ROWS, BF16_ROWS, tn), lambda i, j: (0, 0, 0, i))
    return pl.pallas_call(
        functools.partial(_peer_dense_kernel, alpha, i1_per_step),
        grid=(n // tn, n_exp // te),
        in_specs=[pl.BlockSpec((d, tn), lambda i, j: (0, i)),
                  pl.BlockSpec((te, d), lambda i, j: (j, 0)),
                  pl.BlockSpec((d, te), lambda i, j: (0, j)),
                  tab_spec, tab_spec, tabb_spec, tabb_spec,
                  pl.BlockSpec((tn, d), lambda i, j: (i, 0)),
                  pl.BlockSpec((1,) + g2.shape[1:], lambda i, j: (i // tiles_per_vec, 0, 0)),
                  pl.BlockSpec(ln_g.shape, lambda i, j: (0, 0)),
                  pl.BlockSpec(ln_b.shape, lambda i, j: (0, 0))],
        out_specs=pl.BlockSpec((tn, d), lambda i, j: (i, 0)),
        out_shape=jax.ShapeDtypeStruct((n, d), F32),
        scratch_shapes=[pltpu.VMEM((d, tn), F32), pltpu.VMEM((te, tn), F32), pltpu.VMEM((te, tn), BF16)],
        compiler_params=_cparams("arbitrary", "arbitrary"),
        name="peer_dense_ln2",
    )(h2T, u_b, vT_b, *tabs, x1, g2, ln_g, ln_b)


def _transpose_cast_kernel(x_ref, o_ref):
    o_ref[...] = x_ref[...].T.astype(BF16)


def _transpose_cast(x):
    r, c = x.shape
    tr = min(512, r)
    return pl.pallas_call(
        _transpose_cast_kernel,
        grid=(r // tr,),
        in_specs=[pl.BlockSpec((tr, c), lambda i: (i, 0))],
        out_specs=pl.BlockSpec((c, tr), lambda i: (0, i)),
        out_shape=jax.ShapeDtypeStruct((c, r), BF16),
        compiler_params=_cparams("arbitrary"),
        name="transpose_cast",
    )(x)


def _pack_w_in(w):
    widths = (ATTN_WIDTH, KV_WIDTH, KV_WIDTH, N_IDX_HEADS * IDX_DIM, N_IDX_HEADS, IDX_DIM,
              D_CONV, D_CONV, D_CONV, w.shape[0], w.shape[0])
    parts, start = [], 0
    for wd in widths:
        parts.append(w[:, start:start + wd])
        start += wd
    q, k, v, qi, wi, ki, bg, cg, xin, ga, gb = parts
    pad = jnp.zeros((w.shape[0], _KIWI - IDX_DIM - N_IDX_HEADS), w.dtype)
    return jnp.concatenate([q, k, v, qi, ki, wi, pad, bg, cg, xin, ga, gb], axis=1).astype(BF16)


def kernel(x_prompt, x_sample, c_prompt, c_sample, cache_k, cache_v, cache_kidx, state_conv, page_table,
           rel_bias, w_ada, b_ada, w_in, conv_w, conv_b, w_o_attn, w_o_conv, w_out, ln1_g, ln1_b,
           ln2_g, ln2_b, peer_wq, peer_k1, peer_k2, peer_u, peer_v):
    depth = w_in.shape[0]
    alpha = (2 * depth) ** 0.25
    bp, sp, d = x_prompt.shape
    bs, ts, _ = x_sample.shape
    n_pool = cache_k.shape[1]
    past = page_table.shape[1] * PAGE_SIZE
    ns = bs * ts

    xp, xs = x_prompt, x_sample.reshape(ns, d)
    outs = [[] for _ in range(8)]
    for l in range(depth):
        c_all = jnp.concatenate([c_prompt, c_sample], axis=0)
        c_all = jnp.pad(c_all, ((0, (-c_all.shape[0]) % 16), (0, 0)))
        mod = _ada(c_all, w_ada[l], b_ada[l])[:bp + bs]
        sh1, sc1, g1, sh2, sc2, g2 = jnp.split(mod, 6, axis=-1)
        as_p = lambda a: a[:bp].reshape(bp, 1, d)
        as_s = lambda a: jnp.repeat(a[bp:], ts, axis=0).reshape(1, ns, d)

        w_all = _pack_w_in(w_in[l])
        woa, woc, wout = w_o_attn[l].astype(BF16), w_o_conv[l].astype(BF16), w_out[l].astype(BF16)
        wq = peer_wq[l].astype(BF16)
        k1, k2 = peer_k1[l].astype(BF16), peer_k2[l].astype(BF16)
        u_b = peer_u[l].astype(BF16)
        vT_b = _transpose_cast(peer_v[l])
        lg1, lb1 = ln1_g[l].reshape(1, d), ln1_b[l].reshape(1, d)
        lg2, lb2 = ln2_g[l].reshape(1, d), ln2_b[l].reshape(1, d)

        (qT, k_p, v_p, kb_p, vTb_p, qiT, ki_p, kib_p, wiT, oconv_p, ga_p, gb_p, ulast_p) = _inproj_prompt(
            xp, as_p(sc1), as_p(sh1), w_all, conv_w[l], conv_b[l])
        oattn_p = _attn_prompt(rel_bias, qT, qiT, wiT, kb_p, vTb_p, kib_p)
        x1_p, h2T_p, pqT_p = _merge(alpha, xp.reshape(bp * sp, d), oattn_p.reshape(bp * sp, ATTN_WIDTH),
                                    oconv_p.reshape(bp * sp, D_CONV), ga_p.reshape(bp * sp, d),
                                    gb_p.reshape(bp * sp, d), as_p(g1), as_p(sc2), as_p(sh2),
                                    woa, woc, wout, wq, lg1, lb1, sp)
        tabs_p = _peer_select(pqT_p, k1, k2)
        y_p = _peer_dense(alpha, h2T_p, u_b, vT_b, tabs_p, x1_p, as_p(g2), lg2, lb2, sp)
        xp = y_p.reshape(bp, sp, d)

        st = state_conv[l]
        st0 = jnp.repeat(st[:, 0], ts, axis=0)
        st1 = jnp.repeat(st[:, 1], ts, axis=0)
        (q_s, k_s, v_s, qi_s, kiwi_s, oconv_s, ga_s, gb_s, u_s) = _inproj_sample(
            xs, as_s(sc1)[0], as_s(sh1)[0], w_all, conv_w[l], conv_b[l], st0, st1, ts)
        ki_s = kiwi_s[:, :IDX_DIM]
        wi_s = kiwi_s[:, IDX_DIM:IDX_DIM + N_IDX_HEADS]
        q5 = q_s.reshape(bs, ts, N_KV_HEADS, GROUP, HEAD_DIM).transpose(0, 2, 3, 1, 4)
        qpad = jnp.zeros((bs, N_KV_HEADS, GROUP * ts, N_KV_HEADS, HEAD_DIM), F32)
        for g in range(N_KV_HEADS):
            qpad = qpad.at[:, g, :, g, :].set(q5[:, g].reshape(bs, GROUP * ts, HEAD_DIM))
        qpad = qpad.reshape(bs, N_HEADS * ts, KV_WIDTH).astype(BF16)
        qi_r = qi_s.reshape(bs, ts, N_IDX_HEADS, IDX_DIM).transpose(0, 2, 1, 3).reshape(
            bs, N_IDX_HEADS * ts, IDX_DIM).astype(BF16)
        wi_r = jnp.broadcast_to(wi_s.reshape(bs, ts, N_IDX_HEADS).transpose(0, 2, 1).reshape(
            bs, N_IDX_HEADS * ts, 1), (bs, N_IDX_HEADS * ts, LANES))
        o_s = _attn_sample(page_table, rel_bias, qpad, qi_r, wi_r,
                           k_s.reshape(bs, ts, KV_WIDTH), v_s.reshape(bs, ts, KV_WIDTH),
                           ki_s.reshape(bs, ts, IDX_DIM),
                           cache_k[l].reshape(n_pool, PAGE_SIZE, KV_WIDTH),
                           cache_v[l].reshape(n_pool, PAGE_SIZE, KV_WIDTH), cache_kidx[l], past)
        o5 = o_s.reshape(bs, N_KV_HEADS, GROUP, ts, N_KV_HEADS, HEAD_DIM)
        oattn_s = jnp.stack([o5[:, g, :, :, g, :] for g in range(N_KV_HEADS)], axis=1)
        oattn_s = oattn_s.transpose(0, 3, 1, 2, 4).reshape(ns, ATTN_WIDTH).astype(BF16)
        x1_s, h2T_s, pqT_s = _merge(alpha, xs, oattn_s, oconv_s, ga_s, gb_s, as_s(g1), as_s(sc2), as_s(sh2),
                                    woa, woc, wout, wq, lg1, lb1, ns)
        tabs_s = _peer_select(pqT_s, k1, k2)
        xs = _peer_dense(alpha, h2T_s, u_b, vT_b, tabs_s, x1_s, as_s(g2), lg2, lb2, ns)

        for lst, val in zip(outs, (
                k_p.reshape(bp, sp, N_KV_HEADS, HEAD_DIM), v_p.reshape(bp, sp, N_KV_HEADS, HEAD_DIM), ki_p,
                ulast_p[:, 8 - (CONV_WIDTH - 1):],
                k_s.reshape(bs, ts, N_KV_HEADS, HEAD_DIM), v_s.reshape(bs, ts, N_KV_HEADS, HEAD_DIM),
                ki_s.reshape(bs, ts, IDX_DIM),
                u_s.reshape(bs, ts, D_CONV)[:, ts - (CONV_WIDTH - 1):])):
            lst.append(val)

    return (xp, xs.reshape(bs, ts, d)) + tuple(jnp.stack(o) for o in outs)
```
